```python
import math
import jax, jax.numpy as jnp
from jax import lax
import numpy as np

D_MODEL = 1024
BATCH = 16
SEQ = 4096
DEPTH = 4

HEAD_DIM = 64
BLOCK = 128
A_HEADS = D_MODEL // (2 * HEAD_DIM)
A_KV_HEADS = 2
A_GROUP = A_HEADS // A_KV_HEADS
WINDOW = 128
B_HEADS = D_MODEL // (4 * HEAD_DIM)
B_VDIM = 2 * HEAD_DIM
C_HEADS = D_MODEL // HEAD_DIM
N_BUCKETS = 32
MAX_EXACT = N_BUCKETS // 2
MAX_DISTANCE = 128
SOFT_HEADS = A_HEADS + B_HEADS
A_Q = A_HEADS * HEAD_DIM
A_KV = A_KV_HEADS * HEAD_DIM
B_QK = B_HEADS * 2 * HEAD_DIM
B_V = B_HEADS * B_VDIM
EVEN_IN = A_Q + 2 * A_KV + 2 * B_QK + B_V
EVEN_OUT = A_Q + B_V
EVEN_SPLITS = [A_Q, A_Q + A_KV, A_Q + 2 * A_KV, A_Q + 2 * A_KV + B_QK, A_Q + 2 * A_KV + 2 * B_QK]
C_WIDTH = C_HEADS * HEAD_DIM
ODD_IN = 3 * C_WIDTH
D_FF = 2816
CONV_W = 3
N_EVEN = (DEPTH + 1) // 2
N_ODD = DEPTH // 2
EPS = 1e-6
SCALE = HEAD_DIM ** -0.5

kernel_name = "hybrid_swa_sink_diff_stickbreak_convffn"


def rms_norm(x, g):
    x32 = x.astype(jnp.float32)
    y = x32 * lax.rsqrt(jnp.mean(x32 * x32, axis=-1, keepdims=True) + EPS)
    return (y * g.astype(jnp.float32)).astype(x.dtype)


def t5_bucket(dist):
    n = jnp.maximum(dist, 0)
    nf = jnp.maximum(n, 1).astype(jnp.float32)
    large = MAX_EXACT + (jnp.log(nf / MAX_EXACT) / math.log(MAX_DISTANCE / MAX_EXACT)
                         * (N_BUCKETS - MAX_EXACT)).astype(jnp.int32)
    large = jnp.minimum(large, N_BUCKETS - 1)
    return jnp.where(n < MAX_EXACT, n, large)


def sliding_window_sink_attention(q, k, v, sinks, bias_table):
    bsz, seq = q.shape[:2]
    nb = seq // BLOCK
    qb = q.reshape(bsz, nb, BLOCK, A_KV_HEADS, A_GROUP, HEAD_DIM)

    def band(t):
        tb = t.reshape(bsz, nb, BLOCK, A_KV_HEADS, HEAD_DIM)
        prev = jnp.concatenate([jnp.zeros_like(tb[:, :1]), tb[:, :-1]], axis=1)
        return jnp.concatenate([prev, tb], axis=2)

    kb, vb = band(k), band(v)
    qi = jnp.arange(BLOCK)[:, None]
    kj = jnp.arange(2 * BLOCK)[None, :]
    dist = qi + BLOCK - kj
    in_window = (dist >= 0) & (dist < WINDOW)
    blk = jnp.arange(nb)[:, None, None]
    valid = in_window[None] & (blk * BLOCK - BLOCK + kj[None] >= 0)
    bias = bias_table[t5_bucket(dist)]
    bias = bias.reshape(BLOCK, 2 * BLOCK, A_KV_HEADS, A_GROUP).transpose(2, 3, 0, 1)
    s = jnp.einsum('bnqhgd,bnkhd->bnhgqk', qb, kb).astype(jnp.float32) * SCALE + bias
    s = jnp.where(valid[None, :, None, None], s, -jnp.inf)
    sink = sinks.astype(jnp.float32).reshape(A_KV_HEADS, A_GROUP)[None, None, :, :, None, None]
    m = jnp.maximum(jnp.max(s, axis=-1, keepdims=True), sink)
    p = jnp.exp(s - m)
    w = p / (jnp.sum(p, axis=-1, keepdims=True) + jnp.exp(sink - m))
    o = jnp.einsum('bnhgqk,bnkhd->bnqhgd', w.astype(v.dtype), vb)
    return o.reshape(bsz, seq, A_Q)


def differential_attention(q1, q2, k1, k2, v, lam, lambda_init, bias_table, gain):
    bsz, seq = q1.shape[:2]
    nb = seq // BLOCK
    kpos = jnp.arange(seq)

    def block(i):
        start = i * BLOCK
        qpos = start + jnp.arange(BLOCK)
        dist = qpos[:, None] - kpos[None, :]
        causal = dist >= 0
        bias = bias_table[t5_bucket(dist)].transpose(2, 0, 1)

        def probs(q, k):
            qb = lax.dynamic_slice_in_dim(q, start, BLOCK, axis=1)
            s = jnp.einsum('bqhd,bkhd->bhqk', qb, k).astype(jnp.float32) * SCALE + bias
            return jax.nn.softmax(jnp.where(causal, s, -jnp.inf), axis=-1)

        w = probs(q1, k1) - lam * probs(q2, k2)
        return jnp.einsum('bhqk,bkhe->bqhe', w.astype(v.dtype), v)

    o = lax.map(block, jnp.arange(nb))
    o = o.transpose(1, 0, 2, 3, 4).reshape(bsz, seq, B_HEADS, B_VDIM)
    o = rms_norm(o, gain) * (1.0 - lambda_init)
    return o.reshape(bsz, seq, B_V)


def stick_breaking_attention(q, k, v):
    bsz, seq = q.shape[:2]
    nb = seq // BLOCK
    kpos = jnp.arange(seq)

    def block(i):
        start = i * BLOCK
        qb = lax.dynamic_slice_in_dim(q, start, BLOCK, axis=1)
        qpos = start + jnp.arange(BLOCK)
        strict = (qpos[:, None] - kpos[None, :]) > 0
        z = jnp.einsum('bqhd,bkhd->bhqk', qb, k).astype(jnp.float32) * SCALE
        log_beta = jax.nn.log_sigmoid(z)
        log_1m_beta = jnp.where(strict, log_beta - z, 0.0)
        later = lax.cumsum(log_1m_beta, axis=3, reverse=True) - log_1m_beta
        a = jnp.where(strict, jnp.exp(log_beta + later), 0.0)
        return jnp.einsum('bhqk,bkhd->bqhd', a.astype(v.dtype), v)

    o = lax.map(block, jnp.arange(nb))
    return o.transpose(1, 0, 2, 3, 4).reshape(bsz, seq, C_WIDTH)


def conv_gated_ffn(h, w_up, w_conv, b_conv, w_down):
    seq = h.shape[1]
    u = h @ w_up
    up = jnp.pad(u, ((0, 0), (CONV_W - 1, 0), (0, 0)))
    u = sum(up[:, tap:tap + seq] * w_conv[tap] for tap in range(CONV_W)) + b_conv
    gate, val = jnp.split(u, 2, axis=-1)
    return (jax.nn.silu(gate) * val) @ w_down


def setup_inputs(seed: int = 0) -> dict:
    key = jax.random.key(seed)
    ks = jax.random.split(key, 22)

    def nrm(k, shape, scale):
        return jax.random.normal(k, shape, jnp.float32) * scale

    return {
        "x": nrm(ks[0], (BATCH, SEQ, D_MODEL), 1.0),
        "rel_bias": nrm(ks[1], (N_BUCKETS, SOFT_HEADS), 0.5),
        "norm_mix": 1.0 + nrm(ks[2], (DEPTH, D_MODEL), 0.02),
        "norm_ffn": 1.0 + nrm(ks[3], (DEPTH, D_MODEL), 0.02),
        "norm_final": 1.0 + nrm(ks[4], (D_MODEL,), 0.02),
        "w_in_even": nrm(ks[5], (N_EVEN, D_MODEL, EVEN_IN), D_MODEL ** -0.5),
        "w_out_even": nrm(ks[6], (N_EVEN, EVEN_OUT, D_MODEL), EVEN_OUT ** -0.5),
        "sinks": nrm(ks[7], (N_EVEN, A_HEADS), 0.5),
        "lam_q1": nrm(ks[8], (N_EVEN, HEAD_DIM), 0.1),
        "lam_k1": nrm(ks[9], (N_EVEN, HEAD_DIM), 0.1),
        "lam_q2": nrm(ks[10], (N_EVEN, HEAD_DIM), 0.1),
        "lam_k2": nrm(ks[11], (N_EVEN, HEAD_DIM), 0.1),
        "diff_norm": 1.0 + nrm(ks[12], (N_EVEN, B_VDIM), 0.02),
        "w_in_odd": nrm(ks[13], (N_ODD, D_MODEL, ODD_IN), D_MODEL ** -0.5),
        "w_out_odd": nrm(ks[14], (N_ODD, C_WIDTH, D_MODEL), C_WIDTH ** -0.5),
        "ffn_up": nrm(ks[15], (DEPTH, D_MODEL, 2 * D_FF), D_MODEL ** -0.5),
        "ffn_conv": nrm(ks[16], (DEPTH, CONV_W, 2 * D_FF), CONV_W ** -0.5),
        "ffn_conv_b": nrm(ks[17], (DEPTH, 2 * D_FF), 0.02),
        "ffn_down": nrm(ks[18], (DEPTH, D_FF, D_MODEL), D_FF ** -0.5),
    }


def reference(x, rel_bias, norm_mix, norm_ffn, norm_final, w_in_even, w_out_even, sinks,
              lam_q1, lam_k1, lam_q2, lam_k2, diff_norm, w_in_odd, w_out_odd,
              ffn_up, ffn_conv, ffn_conv_b, ffn_down):
    bsz, seq = x.shape[:2]
    for layer in range(DEPTH):
        h = rms_norm(x, norm_mix[layer])
        if layer % 2 == 0:
            e = layer // 2
            proj = h @ w_in_even[e]
            aq, ak, av, bq, bk, bv = jnp.split(proj, EVEN_SPLITS, axis=-1)
            aq = aq.reshape(bsz, seq, A_HEADS, HEAD_DIM)
            ak = ak.reshape(bsz, seq, A_KV_HEADS, HEAD_DIM)
            av = av.reshape(bsz, seq, A_KV_HEADS, HEAD_DIM)
            oa = sliding_window_sink_attention(aq, ak, av, sinks[e], rel_bias[:, :A_HEADS])
            bq = bq.reshape(bsz, seq, B_HEADS, 2, HEAD_DIM)
            bk = bk.reshape(bsz, seq, B_HEADS, 2, HEAD_DIM)
            bv = bv.reshape(bsz, seq, B_HEADS, B_VDIM)
            lambda_init = 0.8 - 0.6 * math.exp(-0.3 * layer)
            lam = (jnp.exp(jnp.sum(lam_q1[e].astype(jnp.float32) * lam_k1[e].astype(jnp.float32)))
                   - jnp.exp(jnp.sum(lam_q2[e].astype(jnp.float32) * lam_k2[e].astype(jnp.float32)))
                   + lambda_init)
            ob = differential_attention(bq[..., 0, :], bq[..., 1, :], bk[..., 0, :], bk[..., 1, :], bv,
                                        lam, lambda_init, rel_bias[:, A_HEADS:], diff_norm[e])
            x = x + jnp.concatenate([oa, ob], axis=-1) @ w_out_even[e]
        else:
            o = layer // 2
            cq, ck, cv = jnp.split(h @ w_in_odd[o], 3, axis=-1)
            shp = (bsz, seq, C_HEADS, HEAD_DIM)
            oc = stick_breaking_attention(cq.reshape(shp), ck.reshape(shp), cv.reshape(shp))
            x = x + oc @ w_out_odd[o]
        h = rms_norm(x, norm_ffn[layer])
        x = x + conv_gated_ffn(h, ffn_up[layer], ffn_conv[layer], ffn_conv_b[layer], ffn_down[layer])
    return rms_norm(x, norm_final)
```

```python
import functools
import math

import jax
import jax.numpy as jnp
from jax import lax
from jax.experimental import pallas as pl
from jax.experimental.pallas import tpu as pltpu

LANES = 128
SUBLANES = 8
VMEM_LIMIT_BYTES = 56 * 1024 * 1024

HEAD_DIM = 64
BLOCK = 128
WINDOW = 128
A_HEADS, A_KV_HEADS = 8, 2
B_HEADS = 4
N_BUCKETS, MAX_EXACT, MAX_DISTANCE = 32, 16, 128
CONV_W = 3
EPS = 1e-6
SCALE = HEAD_DIM ** -0.5

F32 = jnp.float32
BF16 = jnp.bfloat16
NEG_INF = float("-inf")


def _cparams(n_axes):
    return pltpu.CompilerParams(
        dimension_semantics=("arbitrary",) * n_axes, vmem_limit_bytes=VMEM_LIMIT_BYTES)


def _rms(x, g):
    ms = jnp.mean(x * x, axis=-1, keepdims=True)
    return x * lax.rsqrt(ms + EPS) * g


def _lane_halves(rows):
    lane = lax.broadcasted_iota(jnp.int32, (rows, LANES), 1)
    return lane < HEAD_DIM


def _block_diag_rows(t):
    lo = _lane_halves(t.shape[0])
    zero = jnp.zeros_like(t)
    return jnp.concatenate([jnp.where(lo, t, zero), jnp.where(lo, zero, t)], axis=0)


def _dot_nt(a, b):
    return lax.dot_general(a, b, (((1,), (1,)), ((), ())), preferred_element_type=F32)


def _norm_proj_kernel(x_ref, g_ref, w_ref, o_ref, *, col_chunk):
    h = _rms(x_ref[...], g_ref[...]).astype(BF16)
    for n0 in range(0, o_ref.shape[1], col_chunk):
        o_ref[:, n0:n0 + col_chunk] = jnp.dot(
            h, w_ref[:, n0:n0 + col_chunk], preferred_element_type=F32).astype(o_ref.dtype)


def _norm_proj(x, g, w, *, tm=512, col_chunk=512):
    n, d = x.shape
    nout = w.shape[1]
    return pl.pallas_call(
        functools.partial(_norm_proj_kernel, col_chunk=col_chunk),
        grid=(n // tm,),
        in_specs=[pl.BlockSpec((tm, d), lambda i: (i, 0)),
                  pl.BlockSpec((1, d), lambda i: (0, 0)),
                  pl.BlockSpec((d, nout), lambda i: (0, 0))],
        out_specs=pl.BlockSpec((tm, nout), lambda i: (i, 0)),
        out_shape=jax.ShapeDtypeStruct((n, nout), BF16),
        compiler_params=_cparams(1),
        name="norm_proj",
    )(x, g.reshape(1, d), w)


def _out_proj_kernel(*refs):
    x_ref, out_ref = refs[0], refs[-1]
    mids = refs[1:-1]
    n_in = len(mids) // 2
    acc = x_ref[...]
    for o_ref, w_ref in zip(mids[:n_in], mids[n_in:]):
        acc = acc + jnp.dot(o_ref[...], w_ref[...], preferred_element_type=F32)
    out_ref[...] = acc


def _out_proj(x, outs, ws, *, tm=512):
    n, d = x.shape
    in_specs = [pl.BlockSpec((tm, d), lambda i: (i, 0))]
    in_specs += [pl.BlockSpec((tm, o.shape[1]), lambda i: (i, 0)) for o in outs]
    in_specs += [pl.BlockSpec(w.shape, lambda i: (0, 0)) for w in ws]
    return pl.pallas_call(
        _out_proj_kernel,
        grid=(n // tm,),
        in_specs=in_specs,
        out_specs=pl.BlockSpec((tm, d), lambda i: (i, 0)),
        out_shape=jax.ShapeDtypeStruct((n, d), F32),
        compiler_params=_cparams(1),
        name="out_proj",
    )(x, *outs, *ws)


def _ffn_kernel(x_ref, g_ref, wup_ref, wc_ref, bc_ref, wdn_ref, gf_ref, out_ref,
                ubuf_ref, tail_ref, act_ref, *, tiles_per_seq, n_chunks, cw, final_norm):
    tm = x_ref.shape[0]
    seq_start = (pl.program_id(0) % tiles_per_seq) == 0
    x = x_ref[...]
    h = _rms(x, g_ref[...]).astype(BF16)
    for c in range(n_chunks):
        cols = slice(c * 2 * cw, (c + 1) * 2 * cw)
        u = jnp.dot(h, wup_ref[:, cols], preferred_element_type=F32)
        @pl.when(seq_start)
        def _():
            ubuf_ref[0:SUBLANES, :] = jnp.zeros((SUBLANES, 2 * cw), F32)

        @pl.when(jnp.logical_not(seq_start))
        def _():
            ubuf_ref[0:SUBLANES, :] = tail_ref[c]

        ubuf_ref[SUBLANES:, :] = u
        tail_ref[c] = u[tm - SUBLANES:, :]
        wc = wc_ref[:, cols]
        conv = (ubuf_ref[SUBLANES - 2:SUBLANES - 2 + tm, :] * wc[0:1]
                + ubuf_ref[SUBLANES - 1:SUBLANES - 1 + tm, :] * wc[1:2]
                + u * wc[2:3]) + bc_ref[:, cols]
        gate, val = conv[:, :cw], conv[:, cw:]
        act_ref[:, c * cw:(c + 1) * cw] = (gate * jax.nn.sigmoid(gate) * val).astype(BF16)
    y = x + jnp.dot(act_ref[...], wdn_ref[...], preferred_element_type=F32)
    if final_norm:
        y = _rms(y, gf_ref[...])
    out_ref[...] = y


def _ffn(x, g, wup, wc, bc, wdn, g_final, *, seq, cw, tm=512, final_norm=False):
    n, d = x.shape
    dff = wdn.shape[0]
    n_chunks = dff // cw
    const = lambda i: (0, 0)
    return pl.pallas_call(
        functools.partial(_ffn_kernel, tiles_per_seq=seq // tm, n_chunks=n_chunks, cw=cw,
                          final_norm=final_norm),
        grid=(n // tm,),
        in_specs=[pl.BlockSpec((tm, d), lambda i: (i, 0)),
                  pl.BlockSpec((1, d), const),
                  pl.BlockSpec(wup.shape, const, pipeline_mode=pl.Buffered(1)),
                  pl.BlockSpec(wc.shape, const),
                  pl.BlockSpec(bc.shape, const),
                  pl.BlockSpec(wdn.shape, const, pipeline_mode=pl.Buffered(1)),
                  pl.BlockSpec((1, d), const)],
        out_specs=pl.BlockSpec((tm, d), lambda i: (i, 0)),
        out_shape=jax.ShapeDtypeStruct((n, d), F32),
        scratch_shapes=[pltpu.VMEM((tm + SUBLANES, 2 * cw), F32),
                        pltpu.VMEM((n_chunks, SUBLANES, 2 * cw), F32),
                        pltpu.VMEM((tm, dff), BF16)],
        compiler_params=_cparams(1),
        name="conv_ffn",
    )(x, g.reshape(1, d), wup, wc, bc, wdn, g_final.reshape(1, d))


def _swa_kernel(sink_ref, q_ref, k_ref, v_ref, bias_ref, o_ref):
    c = pl.program_id(1)
    nb = q_ref.shape[0] // BLOCK
    sink_a = sink_ref[2 * c]
    sink_b = sink_ref[2 * c + 1]

    def body(j, carry):
        cur = pl.multiple_of(j * BLOCK, BLOCK)
        prev = pl.multiple_of(jnp.maximum(j - 1, 0) * BLOCK, BLOCK)
        q2 = q_ref[pl.ds(cur, BLOCK), :]
        kk = jnp.concatenate([k_ref[pl.ds(prev, BLOCK), :], k_ref[pl.ds(cur, BLOCK), :]], axis=0)
        vv = jnp.concatenate([v_ref[pl.ds(prev, BLOCK), :], v_ref[pl.ds(cur, BLOCK), :]], axis=0)
        s = _dot_nt(q2, _block_diag_rows(kk)) + bias_ref[jnp.minimum(j, 1)]
        w_halves = []
        for half, sink in ((s[:, :2 * BLOCK], sink_a), (s[:, 2 * BLOCK:], sink_b)):
            m = jnp.maximum(jnp.max(half, axis=-1, keepdims=True), sink)
            p = jnp.exp(half - m)
            w_halves.append(p / (jnp.sum(p, axis=-1, keepdims=True) + jnp.exp(sink - m)))
        w = jnp.concatenate(w_halves, axis=1).astype(BF16)
        o = jnp.dot(w, _block_diag_rows(vv), preferred_element_type=F32)
        o_ref[pl.ds(cur, BLOCK), :] = o.astype(o_ref.dtype)
        return carry

    lax.fori_loop(0, nb, body, 0)


def _swa(proj, sinks, bias, *, bsz, seq):
    n_pairs = A_HEADS // 2
    k_col0, v_col0 = n_pairs, n_pairs + A_KV_HEADS
    pairs_per_kv = n_pairs // A_KV_HEADS
    return pl.pallas_call(
        _swa_kernel,
        grid=(bsz, n_pairs),
        in_specs=[pl.BlockSpec(memory_space=pltpu.SMEM),
                  pl.BlockSpec((None, seq, LANES), lambda b, c: (b, 0, c)),
                  pl.BlockSpec((None, seq, LANES), lambda b, c: (b, 0, k_col0 + c // pairs_per_kv)),
                  pl.BlockSpec((None, seq, LANES), lambda b, c: (b, 0, v_col0 + c // pairs_per_kv)),
                  pl.BlockSpec((2, None, BLOCK, 4 * BLOCK), lambda b, c: (0, c, 0, 0))],
        out_specs=pl.BlockSpec((None, seq, LANES), lambda b, c: (b, 0, c)),
        out_shape=jax.ShapeDtypeStruct((bsz, seq, n_pairs * LANES), BF16),
        compiler_params=_cparams(2),
        name="swa_sink",
    )(sinks, proj, proj, proj, bias)


def _diff_kernel(q_ref, k_ref, v_ref, bias_ref, lam_ref, gain_ref, o_ref, m_ref, acc_ref, *, lambda_init):
    nb = q_ref.shape[0] // BLOCK
    lp = lam_ref[...]
    lam = (jnp.exp(jnp.sum(lp[0:1] * lp[1:2], axis=-1, keepdims=True))
           - jnp.exp(jnp.sum(lp[2:3] * lp[3:4], axis=-1, keepdims=True)) + lambda_init)
    ones = jnp.ones((BLOCK, LANES), BF16)

    def q_block(qi, carry):
        q0 = pl.multiple_of(qi * BLOCK, BLOCK)
        q2 = q_ref[pl.ds(q0, BLOCK), :]
        m_ref[...] = jnp.full(m_ref.shape, NEG_INF, F32)
        acc_ref[...] = jnp.zeros(acc_ref.shape, F32)

        def step(kb, bias):
            k0 = pl.multiple_of(kb * BLOCK, BLOCK)
            s = _dot_nt(q2, _block_diag_rows(k_ref[pl.ds(k0, BLOCK), :]))
            if bias is not None:
                s = s + bias
            m_old = m_ref[...]
            m_blk = jnp.concatenate(
                [jnp.broadcast_to(jnp.max(s[:, :BLOCK], axis=-1, keepdims=True), (BLOCK, BLOCK)),
                 jnp.broadcast_to(jnp.max(s[:, BLOCK:], axis=-1, keepdims=True), (BLOCK, BLOCK))], axis=1)
            m_new = jnp.maximum(m_old, m_blk)
            alpha = jnp.exp(m_old - m_new)
            p = jnp.exp(s - m_new).astype(BF16)
            m_ref[...] = m_new
            p_stack = jnp.concatenate([p[:, :BLOCK], p[:, BLOCK:]], axis=0)
            a_stack = jnp.concatenate([alpha[:, :BLOCK], alpha[:, BLOCK:]], axis=0)
            v_aug = jnp.concatenate([v_ref[pl.ds(k0, BLOCK), :], ones], axis=1)
            pv = jnp.dot(p_stack, v_aug, preferred_element_type=F32)
            acc_ref[...] = acc_ref[...] * jnp.concatenate([a_stack, a_stack], axis=1) + pv

        def far(kb, c2):
            step(kb, None)
            return c2

        lax.fori_loop(0, jnp.maximum(qi - 1, 0), far, 0)

        @pl.when(qi >= 1)
        def _():
            step(qi - 1, bias_ref[1])

        step(qi, bias_ref[0])

        acc = acc_ref[...]
        o1 = acc[:BLOCK, :LANES] / acc[:BLOCK, LANES:]
        o2 = acc[BLOCK:, :LANES] / acc[BLOCK:, LANES:]
        o = _rms(o1 - lam * o2, gain_ref[...]) * (1.0 - lambda_init)
        o_ref[pl.ds(q0, BLOCK), :] = o.astype(o_ref.dtype)
        return carry

    lax.fori_loop(0, nb, q_block, 0)


def _diff(proj, bias, lam_rows, gain, *, bsz, seq, lambda_init):
    q_col0 = A_HEADS // 2 + 2 * A_KV_HEADS
    k_col0, v_col0 = q_col0 + B_HEADS, q_col0 + 2 * B_HEADS
    return pl.pallas_call(
        functools.partial(_diff_kernel, lambda_init=lambda_init),
        grid=(bsz, B_HEADS),
        in_specs=[pl.BlockSpec((None, seq, LANES), lambda b, h: (b, 0, q_col0 + h)),
                  pl.BlockSpec((None, seq, LANES), lambda b, h: (b, 0, k_col0 + h)),
                  pl.BlockSpec((None, seq, LANES), lambda b, h: (b, 0, v_col0 + h)),
                  pl.BlockSpec((None, 2, BLOCK, 2 * BLOCK), lambda b, h: (h, 0, 0, 0)),
                  pl.BlockSpec((SUBLANES, LANES), lambda b, h: (0, 0)),
                  pl.BlockSpec((1, LANES), lambda b, h: (0, 0))],
        out_specs=pl.BlockSpec((None, seq, LANES), lambda b, h: (b, 0, h)),
        out_shape=jax.ShapeDtypeStruct((bsz, seq, B_HEADS * LANES), BF16),
        scratch_shapes=[pltpu.VMEM((BLOCK, 2 * BLOCK), F32),
                        pltpu.VMEM((2 * BLOCK, 2 * LANES), F32)],
        compiler_params=_cparams(2),
        name="diff_attn",
    )(proj, proj, proj, bias, lam_rows, gain.reshape(1, LANES))


def _stick_kernel(q_ref, k_ref, v_ref, tri_ref, o_ref, r_ref, acc_ref):
    nb = q_ref.shape[0] // BLOCK
    row = lax.broadcasted_iota(jnp.int32, (BLOCK, 2 * BLOCK), 0)
    col = lax.broadcasted_iota(jnp.int32, (BLOCK, 2 * BLOCK), 1) % BLOCK
    strict = col < row

    def q_block(qi, carry):
        q0 = pl.multiple_of(qi * BLOCK, BLOCK)
        q2 = q_ref[pl.ds(q0, BLOCK), :]
        r_ref[...] = jnp.zeros(r_ref.shape, F32)
        acc_ref[...] = jnp.zeros(acc_ref.shape, F32)

        def step(kb, diag):
            k0 = pl.multiple_of(kb * BLOCK, BLOCK)
            z = _dot_nt(q2, _block_diag_rows(k_ref[pl.ds(k0, BLOCK), :]))
            sp = jnp.maximum(z, 0.0) + jnp.log1p(jnp.exp(-jnp.abs(z)))
            if diag:
                sp = jnp.where(strict, sp, 0.0)
            hi = sp.astype(BF16)
            lo = (sp - hi.astype(F32)).astype(BF16)
            tri = tri_ref[...]
            ca = jnp.dot(jnp.concatenate([hi[:, :BLOCK], lo[:, :BLOCK]], axis=1), tri,
                         preferred_element_type=F32)
            cb = jnp.dot(jnp.concatenate([hi[:, BLOCK:], lo[:, BLOCK:]], axis=1), tri,
                         preferred_element_type=F32)
            csum = jnp.concatenate([ca[:, :BLOCK], cb[:, :BLOCK]], axis=1)
            total = jnp.concatenate([ca[:, BLOCK:], cb[:, BLOCK:]], axis=1)
            r = r_ref[...]
            a = jnp.exp(z - csum - r)
            if diag:
                a = jnp.where(strict, a, 0.0)
            r_ref[...] = r + total
            vbd = _block_diag_rows(v_ref[pl.ds(k0, BLOCK), :])
            acc_ref[...] += jnp.dot(a.astype(BF16), vbd, preferred_element_type=F32)

        step(qi, True)

        def back(t, c2):
            step(qi - 1 - t, False)
            return c2

        lax.fori_loop(0, qi, back, 0)
        o_ref[pl.ds(q0, BLOCK), :] = acc_ref[...].astype(o_ref.dtype)
        return carry

    lax.fori_loop(0, nb, q_block, 0)


def _stick(proj, tri, *, bsz, seq, n_pairs):
    return pl.pallas_call(
        _stick_kernel,
        grid=(bsz, n_pairs),
        in_specs=[pl.BlockSpec((None, seq, LANES), lambda b, p: (b, 0, p)),
                  pl.BlockSpec((None, seq, LANES), lambda b, p: (b, 0, n_pairs + p)),
                  pl.BlockSpec((None, seq, LANES), lambda b, p: (b, 0, 2 * n_pairs + p)),
                  pl.BlockSpec((2 * BLOCK, 2 * BLOCK), lambda b, p: (0, 0))],
        out_specs=pl.BlockSpec((None, seq, LANES), lambda b, p: (b, 0, p)),
        out_shape=jax.ShapeDtypeStruct((bsz, seq, n_pairs * LANES), BF16),
        scratch_shapes=[pltpu.VMEM((BLOCK, 2 * BLOCK), F32),
                        pltpu.VMEM((BLOCK, LANES), F32)],
        compiler_params=_cparams(2),
        name="stick_breaking",
    )(proj, proj, proj, tri)


def _t5_bucket(dist):
    n = jnp.maximum(dist, 0)
    nf = jnp.maximum(n, 1).astype(F32)
    large = MAX_EXACT + (jnp.log(nf / MAX_EXACT) / math.log(MAX_DISTANCE / MAX_EXACT)
                         * (N_BUCKETS - MAX_EXACT)).astype(jnp.int32)
    large = jnp.minimum(large, N_BUCKETS - 1)
    return jnp.where(n < MAX_EXACT, n, large)


def _band_bias(rel_bias):
    qi = jnp.arange(BLOCK)[:, None]
    kj = jnp.arange(2 * BLOCK)[None, :]
    dist = qi + BLOCK - kj
    return rel_bias[_t5_bucket(dist)].transpose(2, 0, 1), dist


def _swa_bias(rel_bias_a):
    band, dist = _band_bias(rel_bias_a)
    in_window = (dist >= 0) & (dist < WINDOW)
    general = jnp.where(in_window[None], band, NEG_INF)
    first = jnp.where((in_window & (jnp.arange(2 * BLOCK)[None, :] >= BLOCK))[None], band, NEG_INF)
    both = jnp.stack([first, general])
    return both.reshape(2, A_HEADS // 2, 2, BLOCK, 2 * BLOCK).transpose(0, 1, 3, 2, 4).reshape(
        2, A_HEADS // 2, BLOCK, 4 * BLOCK)


def _diff_bias(rel_bias_b):
    band, dist = _band_bias(rel_bias_b)
    band = band - rel_bias_b[N_BUCKETS - 1][:, None, None]
    diag = jnp.where((dist[:, BLOCK:] >= 0)[None], band[:, :, BLOCK:], NEG_INF)
    prev = band[:, :, :BLOCK]
    both = jnp.stack([diag, prev], axis=1)
    return jnp.concatenate([both, both], axis=-1)


def _even_in_weight(w):
    aq, ak, av, bq, bk, bv = jnp.split(w, [512, 640, 768, 1280, 1792], axis=1)
    dup = lambda t: jnp.concatenate(
        [t[:, h * HEAD_DIM:(h + 1) * HEAD_DIM] for h in range(A_KV_HEADS) for _ in range(2)], axis=1)
    return jnp.concatenate([aq * SCALE, dup(ak), dup(av), bq * SCALE, bk, bv], axis=1).astype(BF16)


def _suffix_matrix():
    j = jnp.arange(BLOCK)[:, None]
    s = jnp.arange(BLOCK)[None, :]
    half = jnp.concatenate([(j >= s).astype(BF16), jnp.ones((BLOCK, BLOCK), BF16)], axis=1)
    return jnp.concatenate([half, half], axis=0)


def _ffn_perm(t, dff, cw):
    lead = t.shape[:-1]
    return t.reshape(*lead, 2, dff // cw, cw).swapaxes(-3, -2).reshape(*lead, 2 * dff)


def kernel(x, rel_bias, norm_mix, norm_ffn, norm_final, w_in_even, w_out_even, sinks, lam_q1, lam_k1, lam_q2,
           lam_k2, diff_norm, w_in_odd, w_out_odd, ffn_up, ffn_conv, ffn_conv_b, ffn_down):
    bsz, seq, d = x.shape
    depth = norm_mix.shape[0]
    dff = ffn_down.shape[1]
    cw = 256
    n_pairs_c = d // LANES
    n = bsz * seq

    swa_bias = _swa_bias(rel_bias[:, :A_HEADS])
    diff_bias = _diff_bias(rel_bias[:, A_HEADS:])
    tri = _suffix_matrix()

    xf = x.reshape(n, d)
    for layer in range(depth):
        if layer % 2 == 0:
            e = layer // 2
            proj = _norm_proj(xf, norm_mix[layer], _even_in_weight(w_in_even[e]))
            proj = proj.reshape(bsz, seq, proj.shape[1])
            oa = _swa(proj, sinks[e], swa_bias, bsz=bsz, seq=seq)
            lambda_init = 0.8 - 0.6 * math.exp(-0.3 * layer)
            pad = lambda v: jnp.pad(v, (0, LANES - HEAD_DIM))
            lam_rows = jnp.stack([pad(lam_q1[e]), pad(lam_k1[e]), pad(lam_q2[e]), pad(lam_k2[e])]
                                 + [jnp.zeros((LANES,), F32)] * (SUBLANES - 4))
            ob = _diff(proj, diff_bias, lam_rows, diff_norm[e], bsz=bsz, seq=seq, lambda_init=lambda_init)
            w_out = w_out_even[e].astype(BF16)
            half = oa.shape[-1]
            xf = _out_proj(xf, [oa.reshape(n, half), ob.reshape(n, half)], [w_out[:half], w_out[half:]])
        else:
            o = layer // 2
            w_in = jnp.concatenate([w_in_odd[o][:, :d] * SCALE, w_in_odd[o][:, d:]], axis=1).astype(BF16)
            proj = _norm_proj(xf, norm_mix[layer], w_in).reshape(bsz, seq, 3 * d)
            oc = _stick(proj, tri, bsz=bsz, seq=seq, n_pairs=n_pairs_c)
            xf = _out_proj(xf, [oc.reshape(n, d)], [w_out_odd[o].astype(BF16)])
        xf = _ffn(xf, norm_ffn[layer],
                  _ffn_perm(ffn_up[layer], dff, cw).astype(BF16),
                  _ffn_perm(ffn_conv[layer], dff, cw),
                  _ffn_perm(ffn_conv_b[layer], dff, cw).reshape(1, 2 * dff),
                  ffn_down[layer].astype(BF16), norm_final,
                  seq=seq, cw=cw, final_norm=(layer == depth - 1))
    return xf.reshape(bsz, seq, d)
```

```python
import functools
import math

import jax
import jax.numpy as jnp
from jax import lax
from jax.experimental import pallas as pl
from jax.experimental.pallas import tpu as pltpu

LANES = 128
SUBLANES = 8
VMEM_LIMIT_BYTES = 56 * 1024 * 1024

HEAD_DIM = 64
BLOCK = 128
TILE = 256
WINDOW = 128
A_HEADS, A_KV_HEADS = 8, 2
B_HEADS = 4
N_BUCKETS, MAX_EXACT, MAX_DISTANCE = 32, 16, 128
CONV_W = 3
EPS = 1e-6
SCALE = HEAD_DIM ** -0.5

F32 = jnp.float32
BF16 = jnp.bfloat16
NEG_INF = float("-inf")


def _cparams(n_axes):
    return pltpu.CompilerParams(
        dimension_semantics=("arbitrary",) * n_axes, vmem_limit_bytes=VMEM_LIMIT_BYTES)


def _rms(x, g):
    ms = jnp.mean(x * x, axis=-1, keepdims=True)
    return x * lax.rsqrt(ms + EPS) * g


def _lane_halves(rows):
    lane = lax.broadcasted_iota(jnp.int32, (rows, LANES), 1)
    return lane < HEAD_DIM


def _block_diag_rows(t):
    lo = _lane_halves(t.shape[0])
    zero = jnp.zeros_like(t)
    return jnp.concatenate([jnp.where(lo, t, zero), jnp.where(lo, zero, t)], axis=0)


def _dot_nt(a, b):
    return lax.dot_general(a, b, (((1,), (1,)), ((), ())), preferred_element_type=F32)


def _norm_proj_kernel(x_ref, g_ref, w_ref, o_ref, *, col_chunk):
    h = _rms(x_ref[...], g_ref[...]).astype(BF16)
    for n0 in range(0, o_ref.shape[1], col_chunk):
        o_ref[:, n0:n0 + col_chunk] = jnp.dot(
            h, w_ref[:, n0:n0 + col_chunk], preferred_element_type=F32).astype(o_ref.dtype)


def _norm_proj(x, g, w, *, tm=512, col_chunk=512):
    n, d = x.shape
    nout = w.shape[1]
    return pl.pallas_call(
        functools.partial(_norm_proj_kernel, col_chunk=col_chunk),
        grid=(n // tm,),
        in_specs=[pl.BlockSpec((tm, d), lambda i: (i, 0)),
                  pl.BlockSpec((1, d), lambda i: (0, 0)),
                  pl.BlockSpec((d, nout), lambda i: (0, 0))],
        out_specs=pl.BlockSpec((tm, nout), lambda i: (i, 0)),
        out_shape=jax.ShapeDtypeStruct((n, nout), BF16),
        compiler_params=_cparams(1),
        name="norm_proj",
    )(x, g.reshape(1, d), w)


def _out_proj_kernel(*refs):
    x_ref, out_ref = refs[0], refs[-1]
    mids = refs[1:-1]
    n_in = len(mids) // 2
    acc = x_ref[...]
    for o_ref, w_ref in zip(mids[:n_in], mids[n_in:]):
        acc = acc + jnp.dot(o_ref[...], w_ref[...], preferred_element_type=F32)
    out_ref[...] = acc


def _out_proj(x, outs, ws, *, tm=512):
    n, d = x.shape
    in_specs = [pl.BlockSpec((tm, d), lambda i: (i, 0))]
    in_specs += [pl.BlockSpec((tm, o.shape[1]), lambda i: (i, 0)) for o in outs]
    in_specs += [pl.BlockSpec(w.shape, lambda i: (0, 0)) for w in ws]
    return pl.pallas_call(
        _out_proj_kernel,
        grid=(n // tm,),
        in_specs=in_specs,
        out_specs=pl.BlockSpec((tm, d), lambda i: (i, 0)),
        out_shape=jax.ShapeDtypeStruct((n, d), F32),
        compiler_params=_cparams(1),
        name="out_proj",
    )(x, *outs, *ws)


def _ffn_kernel(x_ref, g_ref, wup_ref, wc_ref, bc_ref, wdn_ref, gf_ref, out_ref,
                ubuf_ref, tail_ref, act_ref, *, tiles_per_seq, n_chunks, cw, final_norm):
    tm = x_ref.shape[0]
    seq_start = (pl.program_id(0) % tiles_per_seq) == 0
    x = x_ref[...]
    h = _rms(x, g_ref[...]).astype(BF16)
    for c in range(n_chunks):
        cols = slice(c * 2 * cw, (c + 1) * 2 * cw)
        u = jnp.dot(h, wup_ref[:, cols], preferred_element_type=F32)
        @pl.when(seq_start)
        def _():
            ubuf_ref[0:SUBLANES, :] = jnp.zeros((SUBLANES, 2 * cw), F32)

        @pl.when(jnp.logical_not(seq_start))
        def _():
            ubuf_ref[0:SUBLANES, :] = tail_ref[c]

        ubuf_ref[SUBLANES:, :] = u
        tail_ref[c] = u[tm - SUBLANES:, :]
        wc = wc_ref[:, cols]
        conv = (ubuf_ref[SUBLANES - 2:SUBLANES - 2 + tm, :] * wc[0:1]
                + ubuf_ref[SUBLANES - 1:SUBLANES - 1 + tm, :] * wc[1:2]
                + u * wc[2:3]) + bc_ref[:, cols]
        gate, val = conv[:, :cw], conv[:, cw:]
        act_ref[:, c * cw:(c + 1) * cw] = (gate * jax.nn.sigmoid(gate) * val).astype(BF16)
    y = x + jnp.dot(act_ref[...], wdn_ref[...], preferred_element_type=F32)
    if final_norm:
        y = _rms(y, gf_ref[...])
    out_ref[...] = y


def _ffn(x, g, wup, wc, bc, wdn, g_final, *, seq, cw, tm=512, final_norm=False):
    n, d = x.shape
    dff = wdn.shape[0]
    n_chunks = dff // cw
    const = lambda i: (0, 0)
    return pl.pallas_call(
        functools.partial(_ffn_kernel, tiles_per_seq=seq // tm, n_chunks=n_chunks, cw=cw,
                          final_norm=final_norm),
        grid=(n // tm,),
        in_specs=[pl.BlockSpec((tm, d), lambda i: (i, 0)),
                  pl.BlockSpec((1, d), const),
                  pl.BlockSpec(wup.shape, const, pipeline_mode=pl.Buffered(1)),
                  pl.BlockSpec(wc.shape, const),
                  pl.BlockSpec(bc.shape, const),
                  pl.BlockSpec(wdn.shape, const, pipeline_mode=pl.Buffered(1)),
                  pl.BlockSpec((1, d), const)],
        out_specs=pl.BlockSpec((tm, d), lambda i: (i, 0)),
        out_shape=jax.ShapeDtypeStruct((n, d), F32),
        scratch_shapes=[pltpu.VMEM((tm + SUBLANES, 2 * cw), F32),
                        pltpu.VMEM((n_chunks, SUBLANES, 2 * cw), F32),
                        pltpu.VMEM((tm, dff), BF16)],
        compiler_params=_cparams(1),
        name="conv_ffn",
    )(x, g.reshape(1, d), wup, wc, bc, wdn, g_final.reshape(1, d))


def _swa_kernel(sink_ref, q_ref, k_ref, v_ref, bias_ref, o_ref, *, n_pairs, pairs_per_kv):
    nb = q_ref.shape[0] // BLOCK

    def body(j, carry):
        cur = pl.multiple_of(j * BLOCK, BLOCK)
        prev = pl.multiple_of(jnp.maximum(j - 1, 0) * BLOCK, BLOCK)
        for c in range(n_pairs):
            cols = slice(c * LANES, (c + 1) * LANES)
            kv_cols = slice((c // pairs_per_kv) * LANES, (c // pairs_per_kv + 1) * LANES)
            q2 = q_ref[pl.ds(cur, BLOCK), cols]
            kk = jnp.concatenate([k_ref[pl.ds(prev, BLOCK), kv_cols], k_ref[pl.ds(cur, BLOCK), kv_cols]],
                                 axis=0)
            vv = jnp.concatenate([v_ref[pl.ds(prev, BLOCK), kv_cols], v_ref[pl.ds(cur, BLOCK), kv_cols]],
                                 axis=0)
            s = _dot_nt(q2, _block_diag_rows(kk)) + bias_ref[jnp.minimum(j, 1), c]
            w_halves = []
            for i in range(2):
                half = s[:, i * 2 * BLOCK:(i + 1) * 2 * BLOCK]
                sink = sink_ref[2 * c + i]
                m = jnp.maximum(jnp.max(half, axis=-1, keepdims=True), sink)
                p = jnp.exp(half - m)
                w_halves.append(p / (jnp.sum(p, axis=-1, keepdims=True) + jnp.exp(sink - m)))
            w = jnp.concatenate(w_halves, axis=1).astype(BF16)
            o = jnp.dot(w, _block_diag_rows(vv), preferred_element_type=F32)
            o_ref[pl.ds(cur, BLOCK), cols] = o.astype(o_ref.dtype)
        return carry

    lax.fori_loop(0, nb, body, 0)


def _swa(proj, sinks, bias, *, bsz, seq):
    n_pairs = A_HEADS // 2
    pairs_per_kv = n_pairs // A_KV_HEADS
    q_w, kv_w = n_pairs * LANES, A_KV_HEADS * LANES
    return pl.pallas_call(
        functools.partial(_swa_kernel, n_pairs=n_pairs, pairs_per_kv=pairs_per_kv),
        grid=(bsz,),
        in_specs=[pl.BlockSpec(memory_space=pltpu.SMEM),
                  pl.BlockSpec((None, seq, q_w), lambda b: (b, 0, 0)),
                  pl.BlockSpec((None, seq, kv_w), lambda b: (b, 0, q_w // kv_w)),
                  pl.BlockSpec((None, seq, kv_w), lambda b: (b, 0, q_w // kv_w + 1)),
                  pl.BlockSpec(bias.shape, lambda b: (0, 0, 0, 0))],
        out_specs=pl.BlockSpec((None, seq, q_w), lambda b: (b, 0, 0)),
        out_shape=jax.ShapeDtypeStruct((bsz, seq, q_w), BF16),
        compiler_params=_cparams(1),
        name="swa_sink",
    )(sinks, proj, proj, proj, bias)


def _diff_kernel(q_ref, k_ref, v_ref, bias_ref, lam_ref, gain_ref, o_ref, m_ref, acc_ref, *,
                 lambda_init, n_heads):
    nt = q_ref.shape[0] // TILE
    reps = TILE // LANES
    lp = lam_ref[...]
    lam = (jnp.exp(jnp.sum(lp[0:1] * lp[1:2], axis=-1, keepdims=True))
           - jnp.exp(jnp.sum(lp[2:3] * lp[3:4], axis=-1, keepdims=True)) + lambda_init)
    ones = jnp.ones((TILE, LANES), BF16)

    def q_tile(qi, carry):
        q0 = pl.multiple_of(qi * TILE, TILE)
        m_ref[...] = jnp.full(m_ref.shape, NEG_INF, F32)
        acc_ref[...] = jnp.zeros(acc_ref.shape, F32)

        def step(kt, bias_idx):
            k0 = pl.multiple_of(kt * TILE, TILE)
            for h in range(n_heads):
                cols = slice(h * LANES, (h + 1) * LANES)
                s = _dot_nt(q_ref[pl.ds(q0, TILE), cols],
                            _block_diag_rows(k_ref[pl.ds(k0, TILE), cols]))
                halves = [s[:, :TILE], s[:, TILE:]]
                if bias_idx is not None:
                    halves = [x + bias_ref[h, bias_idx] for x in halves]
                m_old = m_ref[h]
                m_blk = jnp.concatenate(
                    [jnp.broadcast_to(jnp.max(x, axis=-1, keepdims=True), (TILE, LANES)) for x in halves],
                    axis=1)
                m_new = jnp.maximum(m_old, m_blk)
                alpha = jnp.exp(m_old - m_new)
                m_ref[h] = m_new
                p = [jnp.exp(x - jnp.concatenate([m_new[:, i * LANES:(i + 1) * LANES]] * reps, axis=1)
                             ).astype(BF16) for i, x in enumerate(halves)]
                p_stack = jnp.concatenate(p, axis=0)
                a_stack = jnp.concatenate([alpha[:, :LANES], alpha[:, LANES:]], axis=0)
                v_aug = jnp.concatenate([v_ref[pl.ds(k0, TILE), cols], ones], axis=1)
                pv = jnp.dot(p_stack, v_aug, preferred_element_type=F32)
                acc_ref[h] = acc_ref[h] * jnp.concatenate([a_stack, a_stack], axis=1) + pv

        def far(kt, c2):
            step(kt, None)
            return c2

        lax.fori_loop(0, jnp.maximum(qi - 1, 0), far, 0)

        @pl.when(qi >= 1)
        def _():
            step(qi - 1, 1)

        step(qi, 0)

        for h in range(n_heads):
            acc = acc_ref[h]
            o1 = acc[:TILE, :LANES] / acc[:TILE, LANES:]
            o2 = acc[TILE:, :LANES] / acc[TILE:, LANES:]
            o = _rms(o1 - lam * o2, gain_ref[...]) * (1.0 - lambda_init)
            o_ref[pl.ds(q0, TILE), h * LANES:(h + 1) * LANES] = o.astype(o_ref.dtype)
        return carry

    lax.fori_loop(0, nt, q_tile, 0)


def _diff(proj, bias, lam_rows, gain, *, bsz, seq, lambda_init, n_heads=4):
    q_col0 = A_HEADS // 2 + 2 * A_KV_HEADS
    groups = B_HEADS // n_heads
    w = n_heads * LANES
    qb, kb, vb = (c // n_heads for c in (q_col0, q_col0 + B_HEADS, q_col0 + 2 * B_HEADS))
    return pl.pallas_call(
        functools.partial(_diff_kernel, lambda_init=lambda_init, n_heads=n_heads),
        grid=(bsz, groups),
        in_specs=[pl.BlockSpec((None, seq, w), lambda b, g: (b, 0, qb + g)),
                  pl.BlockSpec((None, seq, w), lambda b, g: (b, 0, kb + g)),
                  pl.BlockSpec((None, seq, w), lambda b, g: (b, 0, vb + g)),
                  pl.BlockSpec((n_heads, 2, TILE, TILE), lambda b, g: (g, 0, 0, 0)),
                  pl.BlockSpec((SUBLANES, LANES), lambda b, g: (0, 0)),
                  pl.BlockSpec((1, LANES), lambda b, g: (0, 0))],
        out_specs=pl.BlockSpec((None, seq, w), lambda b, g: (b, 0, g)),
        out_shape=jax.ShapeDtypeStruct((bsz, seq, B_HEADS * LANES), BF16),
        scratch_shapes=[pltpu.VMEM((n_heads, TILE, 2 * LANES), F32),
                        pltpu.VMEM((n_heads, 2 * TILE, 2 * LANES), F32)],
        compiler_params=_cparams(2),
        name="diff_attn",
    )(proj, proj, proj, bias, lam_rows, gain.reshape(1, LANES))


def _stick_kernel(q_ref, k_ref, v_ref, tri_ref, o_ref, r_ref, acc_ref, *, n_pairs):
    nt = q_ref.shape[0] // TILE
    nsub = TILE // BLOCK
    row = lax.broadcasted_iota(jnp.int32, (TILE, 2 * TILE), 0)
    col = lax.broadcasted_iota(jnp.int32, (TILE, 2 * TILE), 1) & (TILE - 1)
    strict = col < row

    def q_tile(qi, carry):
        q0 = pl.multiple_of(qi * TILE, TILE)
        r_ref[...] = jnp.zeros(r_ref.shape, F32)
        acc_ref[...] = jnp.zeros(acc_ref.shape, F32)

        def step(kt, diag):
            k0 = pl.multiple_of(kt * TILE, TILE)
            tri = tri_ref[...]
            for p in range(n_pairs):
                cols = slice(p * LANES, (p + 1) * LANES)
                z = _dot_nt(q_ref[pl.ds(q0, TILE), cols],
                            _block_diag_rows(k_ref[pl.ds(k0, TILE), cols]))
                sp = jnp.maximum(z, 0.0) + jnp.log(1.0 + jnp.exp(-jnp.abs(z)))
                if diag:
                    sp = jnp.where(strict, sp, 0.0)
                hi = sp.astype(BF16)
                lo = (sp - hi.astype(F32)).astype(BF16)
                r = r_ref[p]
                pieces = [None] * (2 * nsub)
                r_heads = []
                for head in range(2):
                    rh = r[:, head * LANES:(head + 1) * LANES]
                    for sub in reversed(range(nsub)):
                        c0 = head * TILE + sub * BLOCK
                        res = jnp.dot(jnp.concatenate([hi[:, c0:c0 + BLOCK], lo[:, c0:c0 + BLOCK]], axis=1),
                                      tri, preferred_element_type=F32)
                        pieces[head * nsub + sub] = res[:, :BLOCK] + rh
                        rh = rh + res[:, BLOCK:]
                    r_heads.append(rh)
                r_ref[p] = jnp.concatenate(r_heads, axis=1)
                a = jnp.exp(z - jnp.concatenate(pieces, axis=1))
                if diag:
                    a = jnp.where(strict, a, 0.0)
                vbd = _block_diag_rows(v_ref[pl.ds(k0, TILE), cols])
                acc_ref[p] += jnp.dot(a.astype(BF16), vbd, preferred_element_type=F32)

        step(qi, True)

        def back(t, c2):
            step(qi - 1 - t, False)
            return c2

        lax.fori_loop(0, qi, back, 0)
        for p in range(n_pairs):
            o_ref[pl.ds(q0, TILE), p * LANES:(p + 1) * LANES] = acc_ref[p].astype(o_ref.dtype)
        return carry

    lax.fori_loop(0, nt, q_tile, 0)


def _stick(proj, tri, *, bsz, seq, n_pairs_total, n_pairs=2):
    groups = n_pairs_total // n_pairs
    w = n_pairs * LANES
    return pl.pallas_call(
        functools.partial(_stick_kernel, n_pairs=n_pairs),
        grid=(bsz, groups),
        in_specs=[pl.BlockSpec((None, seq, w), lambda b, g: (b, 0, g)),
                  pl.BlockSpec((None, seq, w), lambda b, g: (b, 0, groups + g)),
                  pl.BlockSpec((None, seq, w), lambda b, g: (b, 0, 2 * groups + g)),
                  pl.BlockSpec((2 * BLOCK, 2 * BLOCK), lambda b, g: (0, 0))],
        out_specs=pl.BlockSpec((None, seq, w), lambda b, g: (b, 0, g)),
        out_shape=jax.ShapeDtypeStruct((bsz, seq, n_pairs_total * LANES), BF16),
        scratch_shapes=[pltpu.VMEM((n_pairs, TILE, 2 * LANES), F32),
                        pltpu.VMEM((n_pairs, TILE, LANES), F32)],
        compiler_params=_cparams(2),
        name="stick_breaking",
    )(proj, proj, proj, tri)


def _t5_bucket(dist):
    n = jnp.maximum(dist, 0)
    nf = jnp.maximum(n, 1).astype(F32)
    large = MAX_EXACT + (jnp.log(nf / MAX_EXACT) / math.log(MAX_DISTANCE / MAX_EXACT)
                         * (N_BUCKETS - MAX_EXACT)).astype(jnp.int32)
    large = jnp.minimum(large, N_BUCKETS - 1)
    return jnp.where(n < MAX_EXACT, n, large)


def _band_bias(rel_bias, blk):
    qi = jnp.arange(blk)[:, None]
    kj = jnp.arange(2 * blk)[None, :]
    dist = qi + blk - kj
    return rel_bias[_t5_bucket(dist)].transpose(2, 0, 1), dist


def _swa_bias(rel_bias_a):
    band, dist = _band_bias(rel_bias_a, BLOCK)
    in_window = (dist >= 0) & (dist < WINDOW)
    general = jnp.where(in_window[None], band, NEG_INF)
    first = jnp.where((in_window & (jnp.arange(2 * BLOCK)[None, :] >= BLOCK))[None], band, NEG_INF)
    both = jnp.stack([first, general])
    return both.reshape(2, A_HEADS // 2, 2, BLOCK, 2 * BLOCK).transpose(0, 1, 3, 2, 4).reshape(
        2, A_HEADS // 2, BLOCK, 4 * BLOCK)


def _diff_bias(rel_bias_b):
    band, dist = _band_bias(rel_bias_b, TILE)
    band = band - rel_bias_b[N_BUCKETS - 1][:, None, None]
    diag = jnp.where((dist[:, TILE:] >= 0)[None], band[:, :, TILE:], NEG_INF)
    return jnp.stack([diag, band[:, :, :TILE]], axis=1)


def _even_in_weight(w):
    aq, ak, av, bq, bk, bv = jnp.split(w, [512, 640, 768, 1280, 1792], axis=1)
    dup = lambda t: jnp.concatenate(
        [t[:, h * HEAD_DIM:(h + 1) * HEAD_DIM] for h in range(A_KV_HEADS) for _ in range(2)], axis=1)
    return jnp.concatenate([aq * SCALE, dup(ak), dup(av), bq * SCALE, bk, bv], axis=1).astype(BF16)


def _suffix_matrix():
    j = jnp.arange(BLOCK)[:, None]
    s = jnp.arange(BLOCK)[None, :]
    half = jnp.concatenate([(j >= s).astype(BF16), jnp.ones((BLOCK, BLOCK), BF16)], axis=1)
    return jnp.concatenate([half, half], axis=0)


def _ffn_perm(t, dff, cw):
    lead = t.shape[:-1]
    return t.reshape(*lead, 2, dff // cw, cw).swapaxes(-3, -2).reshape(*lead, 2 * dff)


def kernel(x, rel_bias, norm_mix, norm_ffn, norm_final, w_in_even, w_out_even, sinks, lam_q1, lam_k1, lam_q2,
           lam_k2, diff_norm, w_in_odd, w_out_odd, ffn_up, ffn_conv, ffn_conv_b, ffn_down):
    bsz, seq, d = x.shape
    depth = norm_mix.shape[0]
    dff = ffn_down.shape[1]
    cw = 256
    n_pairs_c = d // LANES
    n = bsz * seq

    swa_bias = _swa_bias(rel_bias[:, :A_HEADS])
    diff_bias = _diff_bias(rel_bias[:, A_HEADS:])
    tri = _suffix_matrix()

    xf = x.reshape(n, d)
    for layer in range(depth):
        if layer % 2 == 0:
            e = layer // 2
            proj = _norm_proj(xf, norm_mix[layer], _even_in_weight(w_in_even[e]))
            proj = proj.reshape(bsz, seq, proj.shape[1])
            oa = _swa(proj, sinks[e], swa_bias, bsz=bsz, seq=seq)
            lambda_init = 0.8 - 0.6 * math.exp(-0.3 * layer)
            pad = lambda v: jnp.pad(v, (0, LANES - HEAD_DIM))
            lam_rows = jnp.stack([pad(lam_q1[e]), pad(lam_k1[e]), pad(lam_q2[e]), pad(lam_k2[e])]
                                 + [jnp.zeros((LANES,), F32)] * (SUBLANES - 4))
            ob = _diff(proj, diff_bias, lam_rows, diff_norm[e], bsz=bsz, seq=seq, lambda_init=lambda_init)
            w_out = w_out_even[e].astype(BF16)
            half = oa.shape[-1]
            xf = _out_proj(xf, [oa.reshape(n, half), ob.reshape(n, half)], [w_out[:half], w_out[half:]])
        else:
            o = layer // 2
            w_in = jnp.concatenate([w_in_odd[o][:, :d] * SCALE, w_in_odd[o][:, d:]], axis=1).astype(BF16)
            proj = _norm_proj(xf, norm_mix[layer], w_in).reshape(bsz, seq, 3 * d)
            oc = _stick(proj, tri, bsz=bsz, seq=seq, n_pairs_total=n_pairs_c)
            xf = _out_proj(xf, [oc.reshape(n, d)], [w_out_odd[o].astype(BF16)])
        xf = _ffn(xf, norm_ffn[layer],
                  _ffn_perm(ffn_up[layer], dff, cw).astype(BF16),
                  _ffn_perm(ffn_conv[layer], dff, cw),
                  _ffn_perm(ffn_conv_b[layer], dff, cw).reshape(1, 2 * dff),
                  ffn_down[layer].astype(BF16), norm_final,
                  seq=seq, cw=cw, final_norm=(layer == depth - 1))
    return xf.reshape(bsz, seq, d)
```

```python
import functools
import math

import jax
import jax.numpy as jnp
from jax import lax
from jax.experimental import pallas as pl
from jax.experimental.pallas import tpu as pltpu

LANES = 128
SUBLANES = 8
VMEM_LIMIT_BYTES = 56 * 1024 * 1024

HEAD_DIM = 64
BLOCK = 128
TILE = 256
WINDOW = 128
A_HEADS, A_KV_HEADS = 8, 2
B_HEADS = 4
N_BUCKETS, MAX_EXACT, MAX_DISTANCE = 32, 16, 128
CONV_W = 3
EPS = 1e-6
SCALE = HEAD_DIM ** -0.5

R_STOP = 110.0

F32 = jnp.float32
BF16 = jnp.bfloat16
NEG_INF = float("-inf")


def _cparams(n_axes):
    return pltpu.CompilerParams(
        dimension_semantics=("arbitrary",) * n_axes, vmem_limit_bytes=VMEM_LIMIT_BYTES)


def _rms(x, g):
    ms = jnp.mean(x * x, axis=-1, keepdims=True)
    return x * lax.rsqrt(ms + EPS) * g


def _lane_halves(rows):
    lane = lax.broadcasted_iota(jnp.int32, (rows, LANES), 1)
    return lane < HEAD_DIM


def _block_diag_rows(t):
    lo = _lane_halves(t.shape[0])
    zero = jnp.zeros_like(t)
    return jnp.concatenate([jnp.where(lo, t, zero), jnp.where(lo, zero, t)], axis=0)


def _dot_nt(a, b):
    return lax.dot_general(a, b, (((1,), (1,)), ((), ())), preferred_element_type=F32)


def _norm_proj_kernel(x_ref, g_ref, w_ref, o_ref, *, col_chunk):
    h = _rms(x_ref[...], g_ref[...]).astype(BF16)
    for n0 in range(0, o_ref.shape[1], col_chunk):
        o_ref[:, n0:n0 + col_chunk] = jnp.dot(
            h, w_ref[:, n0:n0 + col_chunk], preferred_element_type=F32).astype(o_ref.dtype)


def _norm_proj(x, g, w, *, tm=512, col_chunk=512):
    n, d = x.shape
    nout = w.shape[1]
    return pl.pallas_call(
        functools.partial(_norm_proj_kernel, col_chunk=col_chunk),
        grid=(n // tm,),
        in_specs=[pl.BlockSpec((tm, d), lambda i: (i, 0)),
                  pl.BlockSpec((1, d), lambda i: (0, 0)),
                  pl.BlockSpec((d, nout), lambda i: (0, 0))],
        out_specs=pl.BlockSpec((tm, nout), lambda i: (i, 0)),
        out_shape=jax.ShapeDtypeStruct((n, nout), BF16),
        compiler_params=_cparams(1),
        name="norm_proj",
    )(x, g.reshape(1, d), w)


def _out_proj_kernel(*refs):
    x_ref, out_ref = refs[0], refs[-1]
    mids = refs[1:-1]
    n_in = len(mids) // 2
    acc = x_ref[...]
    for o_ref, w_ref in zip(mids[:n_in], mids[n_in:]):
        acc = acc + jnp.dot(o_ref[...], w_ref[...], preferred_element_type=F32)
    out_ref[...] = acc


def _out_proj(x, outs, ws, *, tm=512):
    n, d = x.shape
    in_specs = [pl.BlockSpec((tm, d), lambda i: (i, 0))]
    in_specs += [pl.BlockSpec((tm, o.shape[1]), lambda i: (i, 0)) for o in outs]
    in_specs += [pl.BlockSpec(w.shape, lambda i: (0, 0)) for w in ws]
    return pl.pallas_call(
        _out_proj_kernel,
        grid=(n // tm,),
        in_specs=in_specs,
        out_specs=pl.BlockSpec((tm, d), lambda i: (i, 0)),
        out_shape=jax.ShapeDtypeStruct((n, d), F32),
        compiler_params=_cparams(1),
        name="out_proj",
    )(x, *outs, *ws)


def _ffn_kernel(x_ref, g_ref, wup_ref, wc_ref, bc_ref, wdn_ref, gf_ref, out_ref,
                ubuf_ref, tail_ref, act_ref, *, tiles_per_seq, n_chunks, cw, final_norm):
    tm = x_ref.shape[0]
    seq_start = (pl.program_id(0) % tiles_per_seq) == 0
    x = x_ref[...]
    h = _rms(x, g_ref[...]).astype(BF16)
    for c in range(n_chunks):
        cols = slice(c * 2 * cw, (c + 1) * 2 * cw)
        u = jnp.dot(h, wup_ref[:, cols], preferred_element_type=F32)
        @pl.when(seq_start)
        def _():
            ubuf_ref[0:SUBLANES, :] = jnp.zeros((SUBLANES, 2 * cw), F32)

        @pl.when(jnp.logical_not(seq_start))
        def _():
            ubuf_ref[0:SUBLANES, :] = tail_ref[c]

        ubuf_ref[SUBLANES:, :] = u
        tail_ref[c] = u[tm - SUBLANES:, :]
        wc = wc_ref[:, cols]
        conv = (ubuf_ref[SUBLANES - 2:SUBLANES - 2 + tm, :] * wc[0:1]
                + ubuf_ref[SUBLANES - 1:SUBLANES - 1 + tm, :] * wc[1:2]
                + u * wc[2:3]) + bc_ref[:, cols]
        gate, val = conv[:, :cw], conv[:, cw:]
        act_ref[:, c * cw:(c + 1) * cw] = (gate * jax.nn.sigmoid(gate) * val).astype(BF16)
    y = x + jnp.dot(act_ref[...], wdn_ref[...], preferred_element_type=F32)
    if final_norm:
        y = _rms(y, gf_ref[...])
    out_ref[...] = y


def _ffn(x, g, wup, wc, bc, wdn, g_final, *, seq, cw, tm=512, final_norm=False):
    n, d = x.shape
    dff = wdn.shape[0]
    n_chunks = dff // cw
    const = lambda i: (0, 0)
    return pl.pallas_call(
        functools.partial(_ffn_kernel, tiles_per_seq=seq // tm, n_chunks=n_chunks, cw=cw,
                          final_norm=final_norm),
        grid=(n // tm,),
        in_specs=[pl.BlockSpec((tm, d), lambda i: (i, 0)),
                  pl.BlockSpec((1, d), const),
                  pl.BlockSpec(wup.shape, const, pipeline_mode=pl.Buffered(1)),
                  pl.BlockSpec(wc.shape, const),
                  pl.BlockSpec(bc.shape, const),
                  pl.BlockSpec(wdn.shape, const, pipeline_mode=pl.Buffered(1)),
                  pl.BlockSpec((1, d), const)],
        out_specs=pl.BlockSpec((tm, d), lambda i: (i, 0)),
        out_shape=jax.ShapeDtypeStruct((n, d), F32),
        scratch_shapes=[pltpu.VMEM((tm + SUBLANES, 2 * cw), F32),
                        pltpu.VMEM((n_chunks, SUBLANES, 2 * cw), F32),
                        pltpu.VMEM((tm, dff), BF16)],
        compiler_params=_cparams(1),
        name="conv_ffn",
    )(x, g.reshape(1, d), wup, wc, bc, wdn, g_final.reshape(1, d))


def _swa_kernel(sink_ref, q_ref, k_ref, v_ref, bias_ref, o_ref, *, n_pairs, pairs_per_kv):
    nb = q_ref.shape[0] // BLOCK

    def body(j, carry):
        cur = pl.multiple_of(j * BLOCK, BLOCK)
        prev = pl.multiple_of(jnp.maximum(j - 1, 0) * BLOCK, BLOCK)
        for c in range(n_pairs):
            cols = slice(c * LANES, (c + 1) * LANES)
            kv_cols = slice((c // pairs_per_kv) * LANES, (c // pairs_per_kv + 1) * LANES)
            q2 = q_ref[pl.ds(cur, BLOCK), cols]
            kk = jnp.concatenate([k_ref[pl.ds(prev, BLOCK), kv_cols], k_ref[pl.ds(cur, BLOCK), kv_cols]],
                                 axis=0)
            vv = jnp.concatenate([v_ref[pl.ds(prev, BLOCK), kv_cols], v_ref[pl.ds(cur, BLOCK), kv_cols]],
                                 axis=0)
            s = _dot_nt(q2, _block_diag_rows(kk)) + bias_ref[jnp.minimum(j, 1), c]
            w_halves = []
            for i in range(2):
                half = s[:, i * 2 * BLOCK:(i + 1) * 2 * BLOCK]
                sink = sink_ref[2 * c + i]
                m = jnp.maximum(jnp.max(half, axis=-1, keepdims=True), sink)
                p = jnp.exp(half - m)
                w_halves.append(p / (jnp.sum(p, axis=-1, keepdims=True) + jnp.exp(sink - m)))
            w = jnp.concatenate(w_halves, axis=1).astype(BF16)
            o = jnp.dot(w, _block_diag_rows(vv), preferred_element_type=F32)
            o_ref[pl.ds(cur, BLOCK), cols] = o.astype(o_ref.dtype)
        return carry

    lax.fori_loop(0, nb, body, 0)


def _swa(proj, sinks, bias, *, bsz, seq):
    n_pairs = A_HEADS // 2
    pairs_per_kv = n_pairs // A_KV_HEADS
    q_w, kv_w = n_pairs * LANES, A_KV_HEADS * LANES
    return pl.pallas_call(
        functools.partial(_swa_kernel, n_pairs=n_pairs, pairs_per_kv=pairs_per_kv),
        grid=(bsz,),
        in_specs=[pl.BlockSpec(memory_space=pltpu.SMEM),
                  pl.BlockSpec((None, seq, q_w), lambda b: (b, 0, 0)),
                  pl.BlockSpec((None, seq, kv_w), lambda b: (b, 0, q_w // kv_w)),
                  pl.BlockSpec((None, seq, kv_w), lambda b: (b, 0, q_w // kv_w + 1)),
                  pl.BlockSpec(bias.shape, lambda b: (0, 0, 0, 0))],
        out_specs=pl.BlockSpec((None, seq, q_w), lambda b: (b, 0, 0)),
        out_shape=jax.ShapeDtypeStruct((bsz, seq, q_w), BF16),
        compiler_params=_cparams(1),
        name="swa_sink",
    )(sinks, proj, proj, proj, bias)


def _diff_kernel(q_ref, k_ref, v_ref, bias_ref, lam_ref, gain_ref, o_ref, m_ref, acc_ref, *,
                 lambda_init, n_heads):
    nt = q_ref.shape[0] // TILE
    reps = TILE // LANES
    lp = lam_ref[...]
    lam = (jnp.exp(jnp.sum(lp[0:1] * lp[1:2], axis=-1, keepdims=True))
           - jnp.exp(jnp.sum(lp[2:3] * lp[3:4], axis=-1, keepdims=True)) + lambda_init)
    ones = jnp.ones((TILE, LANES), BF16)

    def q_tile(qi, carry):
        q0 = pl.multiple_of(qi * TILE, TILE)
        m_ref[...] = jnp.full(m_ref.shape, NEG_INF, F32)
        acc_ref[...] = jnp.zeros(acc_ref.shape, F32)

        def step(kt, bias_idx):
            k0 = pl.multiple_of(kt * TILE, TILE)
            for h in range(n_heads):
                cols = slice(h * LANES, (h + 1) * LANES)
                s = _dot_nt(q_ref[pl.ds(q0, TILE), cols],
                            _block_diag_rows(k_ref[pl.ds(k0, TILE), cols]))
                halves = [s[:, :TILE], s[:, TILE:]]
                if bias_idx is not None:
                    halves = [x + bias_ref[h, bias_idx] for x in halves]
                m_old = m_ref[h]
                m_blk = jnp.concatenate(
                    [jnp.broadcast_to(jnp.max(x, axis=-1, keepdims=True), (TILE, LANES)) for x in halves],
                    axis=1)
                m_new = jnp.maximum(m_old, m_blk)
                alpha = jnp.exp(m_old - m_new)
                m_ref[h] = m_new
                p = [jnp.exp(x - jnp.concatenate([m_new[:, i * LANES:(i + 1) * LANES]] * reps, axis=1)
                             ).astype(BF16) for i, x in enumerate(halves)]
                p_stack = jnp.concatenate(p, axis=0)
                a_stack = jnp.concatenate([alpha[:, :LANES], alpha[:, LANES:]], axis=0)
                v_aug = jnp.concatenate([v_ref[pl.ds(k0, TILE), cols], ones], axis=1)
                pv = jnp.dot(p_stack, v_aug, preferred_element_type=F32)
                acc_ref[h] = acc_ref[h] * jnp.concatenate([a_stack, a_stack], axis=1) + pv

        def far(kt, c2):
            step(kt, None)
            return c2

        lax.fori_loop(0, jnp.maximum(qi - 1, 0), far, 0)

        @pl.when(qi >= 1)
        def _():
            step(qi - 1, 1)

        step(qi, 0)

        for h in range(n_heads):
            acc = acc_ref[h]
            o1 = acc[:TILE, :LANES] / acc[:TILE, LANES:]
            o2 = acc[TILE:, :LANES] / acc[TILE:, LANES:]
            o = _rms(o1 - lam * o2, gain_ref[...]) * (1.0 - lambda_init)
            o_ref[pl.ds(q0, TILE), h * LANES:(h + 1) * LANES] = o.astype(o_ref.dtype)
        return carry

    lax.fori_loop(0, nt, q_tile, 0)


def _diff(proj, bias, lam_rows, gain, *, bsz, seq, lambda_init, n_heads=4):
    q_col0 = A_HEADS // 2 + 2 * A_KV_HEADS
    groups = B_HEADS // n_heads
    w = n_heads * LANES
    qb, kb, vb = (c // n_heads for c in (q_col0, q_col0 + B_HEADS, q_col0 + 2 * B_HEADS))
    return pl.pallas_call(
        functools.partial(_diff_kernel, lambda_init=lambda_init, n_heads=n_heads),
        grid=(bsz, groups),
        in_specs=[pl.BlockSpec((None, seq, w), lambda b, g: (b, 0, qb + g)),
                  pl.BlockSpec((None, seq, w), lambda b, g: (b, 0, kb + g)),
                  pl.BlockSpec((None, seq, w), lambda b, g: (b, 0, vb + g)),
                  pl.BlockSpec((n_heads, 2, TILE, TILE), lambda b, g: (g, 0, 0, 0)),
                  pl.BlockSpec((SUBLANES, LANES), lambda b, g: (0, 0)),
                  pl.BlockSpec((1, LANES), lambda b, g: (0, 0))],
        out_specs=pl.BlockSpec((None, seq, w), lambda b, g: (b, 0, g)),
        out_shape=jax.ShapeDtypeStruct((bsz, seq, B_HEADS * LANES), BF16),
        scratch_shapes=[pltpu.VMEM((n_heads, TILE, 2 * LANES), F32),
                        pltpu.VMEM((n_heads, 2 * TILE, 2 * LANES), F32)],
        compiler_params=_cparams(2),
        name="diff_attn",
    )(proj, proj, proj, bias, lam_rows, gain.reshape(1, LANES))


def _stick_kernel(q_ref, k_ref, v_ref, tri_ref, o_ref, r_ref, acc_ref, *, n_pairs):
    nt = q_ref.shape[0] // TILE
    nsub = TILE // BLOCK
    row = lax.broadcasted_iota(jnp.int32, (TILE, 2 * TILE), 0)
    col = lax.broadcasted_iota(jnp.int32, (TILE, 2 * TILE), 1) & (TILE - 1)
    strict = col < row

    def q_tile(qi, carry):
        q0 = pl.multiple_of(qi * TILE, TILE)
        r_ref[...] = jnp.zeros(r_ref.shape, F32)
        acc_ref[...] = jnp.zeros(acc_ref.shape, F32)

        def step(kt, diag):
            k0 = pl.multiple_of(kt * TILE, TILE)
            tri = tri_ref[...]
            for p in range(n_pairs):
                cols = slice(p * LANES, (p + 1) * LANES)
                z = _dot_nt(q_ref[pl.ds(q0, TILE), cols],
                            _block_diag_rows(k_ref[pl.ds(k0, TILE), cols]))
                sp = jnp.maximum(z, 0.0) + jnp.log(1.0 + jnp.exp(-jnp.abs(z)))
                if diag:
                    sp = jnp.where(strict, sp, 0.0)
                hi = sp.astype(BF16)
                lo = (sp - hi.astype(F32)).astype(BF16)
                r = r_ref[p]
                pieces = [None] * (2 * nsub)
                r_heads = []
                for head in range(2):
                    rh = r[:, head * LANES:(head + 1) * LANES]
                    for sub in reversed(range(nsub)):
                        c0 = head * TILE + sub * BLOCK
                        res = jnp.dot(jnp.concatenate([hi[:, c0:c0 + BLOCK], lo[:, c0:c0 + BLOCK]], axis=1),
                                      tri, preferred_element_type=F32)
                        pieces[head * nsub + sub] = res[:, :BLOCK] + rh
                        rh = rh + res[:, BLOCK:]
                    r_heads.append(rh)
                r_ref[p] = jnp.concatenate(r_heads, axis=1)
                r_low = functools.reduce(jnp.minimum, r_heads + ([r_low] if p else []))
                a = jnp.exp(z - jnp.concatenate(pieces, axis=1))
                if diag:
                    a = jnp.where(strict, a, 0.0)
                vbd = _block_diag_rows(v_ref[pl.ds(k0, TILE), cols])
                acc_ref[p] += jnp.dot(a.astype(BF16), vbd, preferred_element_type=F32)
            return jnp.min(r_low)

        def back(state):
            kt, _ = state
            return kt - 1, step(kt, False)

        lax.while_loop(lambda state: (state[0] >= 0) & (state[1] < R_STOP), back, (qi - 1, step(qi, True)))
        for p in range(n_pairs):
            o_ref[pl.ds(q0, TILE), p * LANES:(p + 1) * LANES] = acc_ref[p].astype(o_ref.dtype)
        return carry

    lax.fori_loop(0, nt, q_tile, 0)


def _stick(proj, tri, *, bsz, seq, n_pairs_total, n_pairs=2):
    groups = n_pairs_total // n_pairs
    w = n_pairs * LANES
    return pl.pallas_call(
        functools.partial(_stick_kernel, n_pairs=n_pairs),
        grid=(bsz, groups),
        in_specs=[pl.BlockSpec((None, seq, w), lambda b, g: (b, 0, g)),
                  pl.BlockSpec((None, seq, w), lambda b, g: (b, 0, groups + g)),
                  pl.BlockSpec((None, seq, w), lambda b, g: (b, 0, 2 * groups + g)),
                  pl.BlockSpec((2 * BLOCK, 2 * BLOCK), lambda b, g: (0, 0))],
        out_specs=pl.BlockSpec((None, seq, w), lambda b, g: (b, 0, g)),
        out_shape=jax.ShapeDtypeStruct((bsz, seq, n_pairs_total * LANES), BF16),
        scratch_shapes=[pltpu.VMEM((n_pairs, TILE, 2 * LANES), F32),
                        pltpu.VMEM((n_pairs, TILE, LANES), F32)],
        compiler_params=_cparams(2),
        name="stick_breaking",
    )(proj, proj, proj, tri)


def _t5_bucket(dist):
    n = jnp.maximum(dist, 0)
    nf = jnp.maximum(n, 1).astype(F32)
    large = MAX_EXACT + (jnp.log(nf / MAX_EXACT) / math.log(MAX_DISTANCE / MAX_EXACT)
                         * (N_BUCKETS - MAX_EXACT)).astype(jnp.int32)
    large = jnp.minimum(large, N_BUCKETS - 1)
    return jnp.where(n < MAX_EXACT, n, large)


def _band_bias(rel_bias, blk):
    qi = jnp.arange(blk)[:, None]
    kj = jnp.arange(2 * blk)[None, :]
    dist = qi + blk - kj
    onehot = jax.nn.one_hot(_t5_bucket(dist), N_BUCKETS, dtype=F32)
    return jnp.einsum("qkn,nh->hqk", onehot, rel_bias, precision=lax.Precision.HIGHEST), dist


def _swa_bias(rel_bias_a):
    band, dist = _band_bias(rel_bias_a, BLOCK)
    in_window = (dist >= 0) & (dist < WINDOW)
    general = jnp.where(in_window[None], band, NEG_INF)
    first = jnp.where((in_window & (jnp.arange(2 * BLOCK)[None, :] >= BLOCK))[None], band, NEG_INF)
    both = jnp.stack([first, general])
    return both.reshape(2, A_HEADS // 2, 2, BLOCK, 2 * BLOCK).transpose(0, 1, 3, 2, 4).reshape(
        2, A_HEADS // 2, BLOCK, 4 * BLOCK)


def _diff_bias(rel_bias_b):
    band, dist = _band_bias(rel_bias_b, TILE)
    band = band - rel_bias_b[N_BUCKETS - 1][:, None, None]
    diag = jnp.where((dist[:, TILE:] >= 0)[None], band[:, :, TILE:], NEG_INF)
    return jnp.stack([diag, band[:, :, :TILE]], axis=1)


def _even_in_weight(w):
    aq, ak, av, bq, bk, bv = jnp.split(w, [512, 640, 768, 1280, 1792], axis=1)
    dup = lambda t: jnp.concatenate(
        [t[:, h * HEAD_DIM:(h + 1) * HEAD_DIM] for h in range(A_KV_HEADS) for _ in range(2)], axis=1)
    return jnp.concatenate([aq * SCALE, dup(ak), dup(av), bq * SCALE, bk, bv], axis=1).astype(BF16)


def _suffix_matrix():
    j = jnp.arange(BLOCK)[:, None]
    s = jnp.arange(BLOCK)[None, :]
    half = jnp.concatenate([(j >= s).astype(BF16), jnp.ones((BLOCK, BLOCK), BF16)], axis=1)
    return jnp.concatenate([half, half], axis=0)


def _ffn_perm(t, dff, cw):
    lead = t.shape[:-1]
    return t.reshape(*lead, 2, dff // cw, cw).swapaxes(-3, -2).reshape(*lead, 2 * dff)


def kernel(x, rel_bias, norm_mix, norm_ffn, norm_final, w_in_even, w_out_even, sinks, lam_q1, lam_k1, lam_q2,
           lam_k2, diff_norm, w_in_odd, w_out_odd, ffn_up, ffn_conv, ffn_conv_b, ffn_down):
    bsz, seq, d = x.shape
    depth = norm_mix.shape[0]
    dff = ffn_down.shape[1]
    cw = 256
    n_pairs_c = d // LANES
    n = bsz * seq

    swa_bias = _swa_bias(rel_bias[:, :A_HEADS])
    diff_bias = _diff_bias(rel_bias[:, A_HEADS:])
    tri = _suffix_matrix()

    xf = x.reshape(n, d)
    for layer in range(depth):
        if layer % 2 == 0:
            e = layer // 2
            proj = _norm_proj(xf, norm_mix[layer], _even_in_weight(w_in_even[e]))
            proj = proj.reshape(bsz, seq, proj.shape[1])
            oa = _swa(proj, sinks[e], swa_bias, bsz=bsz, seq=seq)
            lambda_init = 0.8 - 0.6 * math.exp(-0.3 * layer)
            pad = lambda v: jnp.pad(v, (0, LANES - HEAD_DIM))
            lam_rows = jnp.stack([pad(lam_q1[e]), pad(lam_k1[e]), pad(lam_q2[e]), pad(lam_k2[e])]
                                 + [jnp.zeros((LANES,), F32)] * (SUBLANES - 4))
            ob = _diff(proj, diff_bias, lam_rows, diff_norm[e], bsz=bsz, seq=seq, lambda_init=lambda_init)
            w_out = w_out_even[e].astype(BF16)
            half = oa.shape[-1]
            xf = _out_proj(xf, [oa.reshape(n, half), ob.reshape(n, half)], [w_out[:half], w_out[half:]])
        else:
            o = layer // 2
            w_in = jnp.concatenate([w_in_odd[o][:, :d] * SCALE, w_in_odd[o][:, d:]], axis=1).astype(BF16)
            proj = _norm_proj(xf, norm_mix[layer], w_in).reshape(bsz, seq, 3 * d)
            oc = _stick(proj, tri, bsz=bsz, seq=seq, n_pairs_total=n_pairs_c)
            xf = _out_proj(xf, [oc.reshape(n, d)], [w_out_odd[o].astype(BF16)])
        xf = _ffn(xf, norm_ffn[layer],
                  _ffn_perm(ffn_up[layer], dff, cw).astype(BF16),
                  _ffn_perm(ffn_conv[layer], dff, cw),
                  _ffn_perm(ffn_conv_b[layer], dff, cw).reshape(1, 2 * dff),
                  ffn_down[layer].astype(BF16), norm_final,
                  seq=seq, cw=cw, final_norm=(layer == depth - 1))
    return xf.reshape(bsz, seq, d)
```

```python
import functools
import math

import jax
import jax.numpy as jnp
from jax import lax
from jax.experimental import pallas as pl
from jax.experimental.pallas import tpu as pltpu

LANES = 128
SUBLANES = 8
VMEM_LIMIT_BYTES = 56 * 1024 * 1024

HEAD_DIM = 64
BLOCK = 128
TILE = 256
SWA_UNROLL = 2
WINDOW = 128
A_HEADS, A_KV_HEADS = 8, 2
B_HEADS = 4
N_BUCKETS, MAX_EXACT, MAX_DISTANCE = 32, 16, 128
CONV_W = 3
EPS = 1e-6
SCALE = HEAD_DIM ** -0.5

LOG2E = math.log2(math.e)
R_STOP = 152.0

F32 = jnp.float32
BF16 = jnp.bfloat16
NEG_INF = float("-inf")


def _cparams(n_axes):
    return pltpu.CompilerParams(
        dimension_semantics=("arbitrary",) * n_axes, vmem_limit_bytes=VMEM_LIMIT_BYTES)


def _rms(x, g):
    ms = jnp.mean(x * x, axis=-1, keepdims=True)
    return x * lax.rsqrt(ms + EPS) * g


def _lane_halves(rows):
    lane = lax.broadcasted_iota(jnp.int32, (rows, LANES), 1)
    return lane < HEAD_DIM


def _block_diag_rows(t):
    lo = _lane_halves(t.shape[0])
    zero = jnp.zeros_like(t)
    return jnp.concatenate([jnp.where(lo, t, zero), jnp.where(lo, zero, t)], axis=0)


def _dot_nt(a, b):
    return lax.dot_general(a, b, (((1,), (1,)), ((), ())), preferred_element_type=F32)


def _norm_proj_kernel(x_ref, g_ref, w_ref, o_ref, *, col_chunk):
    h = _rms(x_ref[...], g_ref[...]).astype(BF16)
    for n0 in range(0, o_ref.shape[1], col_chunk):
        o_ref[:, n0:n0 + col_chunk] = jnp.dot(
            h, w_ref[:, n0:n0 + col_chunk], preferred_element_type=F32).astype(o_ref.dtype)


def _norm_proj(x, g, w, *, tm=512, col_chunk=512):
    n, d = x.shape
    nout = w.shape[1]
    return pl.pallas_call(
        functools.partial(_norm_proj_kernel, col_chunk=col_chunk),
        grid=(n // tm,),
        in_specs=[pl.BlockSpec((tm, d), lambda i: (i, 0)),
                  pl.BlockSpec((1, d), lambda i: (0, 0)),
                  pl.BlockSpec((d, nout), lambda i: (0, 0))],
        out_specs=pl.BlockSpec((tm, nout), lambda i: (i, 0)),
        out_shape=jax.ShapeDtypeStruct((n, nout), BF16),
        compiler_params=_cparams(1),
        name="norm_proj",
    )(x, g.reshape(1, d), w)


def _out_proj_kernel(*refs):
    x_ref, out_ref = refs[0], refs[-1]
    mids = refs[1:-1]
    n_in = len(mids) // 2
    acc = x_ref[...]
    for o_ref, w_ref in zip(mids[:n_in], mids[n_in:]):
        acc = acc + jnp.dot(o_ref[...], w_ref[...], preferred_element_type=F32)
    out_ref[...] = acc


def _out_proj(x, outs, ws, *, tm=512):
    n, d = x.shape
    in_specs = [pl.BlockSpec((tm, d), lambda i: (i, 0))]
    in_specs += [pl.BlockSpec((tm, o.shape[1]), lambda i: (i, 0)) for o in outs]
    in_specs += [pl.BlockSpec(w.shape, lambda i: (0, 0)) for w in ws]
    return pl.pallas_call(
        _out_proj_kernel,
        grid=(n // tm,),
        in_specs=in_specs,
        out_specs=pl.BlockSpec((tm, d), lambda i: (i, 0)),
        out_shape=jax.ShapeDtypeStruct((n, d), F32),
        compiler_params=_cparams(1),
        name="out_proj",
    )(x, *outs, *ws)


def _ffn_kernel(x_ref, g_ref, wup_ref, wc_ref, bc_ref, wdn_ref, gf_ref, out_ref,
                tail_ref, act_ref, *, tiles_per_seq, n_chunks, cw, final_norm):
    tm = x_ref.shape[0]
    dff = wdn_ref.shape[0]

    @pl.when((pl.program_id(0) % tiles_per_seq) == 0)
    def _():
        tail_ref[...] = jnp.zeros(tail_ref.shape, F32)

    x = x_ref[...]
    h = _rms(x, g_ref[...]).astype(BF16)
    for c in range(n_chunks):
        conv = []
        for part in range(2):
            cols = slice(part * dff + c * cw, part * dff + (c + 1) * cw)
            u = jnp.dot(h, wup_ref[:, cols], preferred_element_type=F32)
            ext = jnp.concatenate([tail_ref[2 * c + part], u], axis=0)
            tail_ref[2 * c + part] = u[tm - SUBLANES:, :]
            u1 = pltpu.roll(ext, 1, axis=0)[SUBLANES:]
            u2 = pltpu.roll(ext, 2, axis=0)[SUBLANES:]
            wc = wc_ref[:, cols]
            conv.append((u2 * wc[0:1] + u1 * wc[1:2] + u * wc[2:3]) + bc_ref[:, cols])
        gate, val = conv
        act_ref[:, c * cw:(c + 1) * cw] = (gate * jax.nn.sigmoid(gate) * val).astype(BF16)
    y = x + jnp.dot(act_ref[...], wdn_ref[...], preferred_element_type=F32)
    if final_norm:
        y = _rms(y, gf_ref[...])
    out_ref[...] = y


def _ffn(x, g, wup, wc, bc, wdn, g_final, *, seq, cw, tm=512, final_norm=False):
    n, d = x.shape
    dff = wdn.shape[0]
    n_chunks = dff // cw
    const = lambda i: (0, 0)
    return pl.pallas_call(
        functools.partial(_ffn_kernel, tiles_per_seq=seq // tm, n_chunks=n_chunks, cw=cw,
                          final_norm=final_norm),
        grid=(n // tm,),
        in_specs=[pl.BlockSpec((tm, d), lambda i: (i, 0)),
                  pl.BlockSpec((1, d), const),
                  pl.BlockSpec(wup.shape, const, pipeline_mode=pl.Buffered(1)),
                  pl.BlockSpec(wc.shape, const),
                  pl.BlockSpec(bc.shape, const),
                  pl.BlockSpec(wdn.shape, const, pipeline_mode=pl.Buffered(1)),
                  pl.BlockSpec((1, d), const)],
        out_specs=pl.BlockSpec((tm, d), lambda i: (i, 0)),
        out_shape=jax.ShapeDtypeStruct((n, d), F32),
        scratch_shapes=[pltpu.VMEM((2 * n_chunks, SUBLANES, cw), F32),
                        pltpu.VMEM((tm, dff), BF16)],
        compiler_params=_cparams(1),
        name="conv_ffn",
    )(x, g.reshape(1, d), wup, wc, bc, wdn, g_final.reshape(1, d))


def _swa_kernel(sink_ref, q_ref, k_ref, v_ref, bias_ref, o_ref, *, n_pairs, pairs_per_kv):
    nb = q_ref.shape[0] // BLOCK

    def band(ref, j, kv_cols):
        cur = pl.multiple_of(j * BLOCK, BLOCK)
        prev = pl.multiple_of(jnp.maximum(j - 1, 0) * BLOCK, BLOCK)
        return jnp.concatenate([ref[pl.ds(prev, BLOCK), kv_cols], ref[pl.ds(cur, BLOCK), kv_cols]], axis=0)

    def body(i, carry):
        chains = [(i * SWA_UNROLL + u, c) for u in range(SWA_UNROLL) for c in range(n_pairs)]
        kv = lambda c: slice((c // pairs_per_kv) * LANES, (c // pairs_per_kv + 1) * LANES)
        scores = []
        for j, c in chains:
            q2 = q_ref[pl.ds(pl.multiple_of(j * BLOCK, BLOCK), BLOCK), c * LANES:(c + 1) * LANES]
            scores.append(_dot_nt(q2, _block_diag_rows(band(k_ref, j, kv(c)))) + bias_ref[jnp.minimum(j, 1), c])
        weights = []
        for (j, c), s in zip(chains, scores):
            w_halves = []
            for i2 in range(2):
                half = s[:, i2 * 2 * BLOCK:(i2 + 1) * 2 * BLOCK]
                sink = sink_ref[2 * c + i2]
                m = jnp.maximum(jnp.max(half, axis=-1, keepdims=True), sink)
                p = jnp.exp(half - m)
                w_halves.append(p / (jnp.sum(p, axis=-1, keepdims=True) + jnp.exp(sink - m)))
            weights.append(jnp.concatenate(w_halves, axis=1).astype(BF16))
        for (j, c), w in zip(chains, weights):
            o = jnp.dot(w, _block_diag_rows(band(v_ref, j, kv(c))), preferred_element_type=F32)
            o_ref[pl.ds(pl.multiple_of(j * BLOCK, BLOCK), BLOCK), c * LANES:(c + 1) * LANES] = (
                o.astype(o_ref.dtype))
        return carry

    lax.fori_loop(0, nb // SWA_UNROLL, body, 0)


def _swa(proj, sinks, bias, *, bsz, seq):
    n_pairs = A_HEADS // 2
    pairs_per_kv = n_pairs // A_KV_HEADS
    q_w, kv_w = n_pairs * LANES, A_KV_HEADS * LANES
    return pl.pallas_call(
        functools.partial(_swa_kernel, n_pairs=n_pairs, pairs_per_kv=pairs_per_kv),
        grid=(bsz,),
        in_specs=[pl.BlockSpec(memory_space=pltpu.SMEM),
                  pl.BlockSpec((None, seq, q_w), lambda b: (b, 0, 0)),
                  pl.BlockSpec((None, seq, kv_w), lambda b: (b, 0, q_w // kv_w)),
                  pl.BlockSpec((None, seq, kv_w), lambda b: (b, 0, q_w // kv_w + 1)),
                  pl.BlockSpec(bias.shape, lambda b: (0, 0, 0, 0))],
        out_specs=pl.BlockSpec((None, seq, q_w), lambda b: (b, 0, 0)),
        out_shape=jax.ShapeDtypeStruct((bsz, seq, q_w), BF16),
        compiler_params=_cparams(1),
        name="swa_sink",
    )(sinks, proj, proj, proj, bias)


def _diff_kernel(q_ref, k_ref, v_ref, bias_ref, lam_ref, gain_ref, o_ref, m_ref, acc_ref, *,
                 lambda_init, n_heads):
    nt = q_ref.shape[0] // TILE
    reps = TILE // LANES
    lp = lam_ref[...]
    lam = (jnp.exp(jnp.sum(lp[0:1] * lp[1:2], axis=-1, keepdims=True))
           - jnp.exp(jnp.sum(lp[2:3] * lp[3:4], axis=-1, keepdims=True)) + lambda_init)
    ones = jnp.ones((TILE, LANES), BF16)

    def q_tile(qi, carry):
        q0 = pl.multiple_of(qi * TILE, TILE)
        m_ref[...] = jnp.full(m_ref.shape, NEG_INF, F32)
        acc_ref[...] = jnp.zeros(acc_ref.shape, F32)

        def step(tiles):
            k0s = [pl.multiple_of(kt * TILE, TILE) for kt, _ in tiles]
            scores = {}
            for h in range(n_heads):
                cols = slice(h * LANES, (h + 1) * LANES)
                qh = q_ref[pl.ds(q0, TILE), cols]
                for t, (_, bias_idx) in enumerate(tiles):
                    s = _dot_nt(qh, _block_diag_rows(k_ref[pl.ds(k0s[t], TILE), cols]))
                    halves = [s[:, :TILE], s[:, TILE:]]
                    if bias_idx is not None:
                        halves = [x + bias_ref[h, bias_idx] for x in halves]
                    scores[h, t] = halves
            p_stacks, a_stacks = [], []
            for h in range(n_heads):
                m_old = m_ref[h]
                m_blk = jnp.concatenate(
                    [functools.reduce(jnp.maximum, [
                        jnp.broadcast_to(jnp.max(scores[h, t][i], axis=-1, keepdims=True), (TILE, LANES))
                        for t in range(len(tiles))]) for i in range(2)], axis=1)
                m_new = jnp.maximum(m_old, m_blk)
                alpha = jnp.exp2(m_old - m_new)
                m_ref[h] = m_new
                p = [jnp.concatenate(
                    [jnp.exp2(scores[h, t][i]
                              - jnp.concatenate([m_new[:, i * LANES:(i + 1) * LANES]] * reps, axis=1)
                              ).astype(BF16) for t in range(len(tiles))], axis=1) for i in range(2)]
                p_stacks.append(jnp.concatenate(p, axis=0))
                a_stack = jnp.concatenate([alpha[:, :LANES], alpha[:, LANES:]], axis=0)
                a_stacks.append(jnp.concatenate([a_stack, a_stack], axis=1))
            for h in range(n_heads):
                cols = slice(h * LANES, (h + 1) * LANES)
                v_aug = jnp.concatenate(
                    [jnp.concatenate([v_ref[pl.ds(k0, TILE), cols], ones], axis=1) for k0 in k0s],
                    axis=0)
                pv = jnp.dot(p_stacks[h], v_aug, preferred_element_type=F32)
                acc_ref[h] = acc_ref[h] * a_stacks[h] + pv

        n_far = jnp.maximum(qi - 1, 0)

        @pl.when(n_far % 2 == 1)
        def _():
            step([(0, None)])

        def far_pair(i, c2):
            kt = n_far % 2 + 2 * i
            step([(kt, None), (kt + 1, None)])
            return c2

        lax.fori_loop(0, n_far // 2, far_pair, 0)

        @pl.when(qi == 0)
        def _():
            step([(0, 0)])

        @pl.when(qi >= 1)
        def _():
            step([(qi - 1, 1), (qi, 0)])

        for h in range(n_heads):
            acc = acc_ref[h]
            o1 = acc[:TILE, :LANES] / acc[:TILE, LANES:]
            o2 = acc[TILE:, :LANES] / acc[TILE:, LANES:]
            o = _rms(o1 - lam * o2, gain_ref[...]) * (1.0 - lambda_init)
            o_ref[pl.ds(q0, TILE), h * LANES:(h + 1) * LANES] = o.astype(o_ref.dtype)
        return carry

    lax.fori_loop(0, nt, q_tile, 0)


def _diff(proj, bias, lam_rows, gain, *, bsz, seq, lambda_init, n_heads=4):
    q_col0 = A_HEADS // 2 + 2 * A_KV_HEADS
    groups = B_HEADS // n_heads
    w = n_heads * LANES
    qb, kb, vb = (c // n_heads for c in (q_col0, q_col0 + B_HEADS, q_col0 + 2 * B_HEADS))
    return pl.pallas_call(
        functools.partial(_diff_kernel, lambda_init=lambda_init, n_heads=n_heads),
        grid=(bsz, groups),
        in_specs=[pl.BlockSpec((None, seq, w), lambda b, g: (b, 0, qb + g)),
                  pl.BlockSpec((None, seq, w), lambda b, g: (b, 0, kb + g)),
                  pl.BlockSpec((None, seq, w), lambda b, g: (b, 0, vb + g)),
                  pl.BlockSpec((n_heads, 2, TILE, TILE), lambda b, g: (g, 0, 0, 0)),
                  pl.BlockSpec((SUBLANES, LANES), lambda b, g: (0, 0)),
                  pl.BlockSpec((1, LANES), lambda b, g: (0, 0))],
        out_specs=pl.BlockSpec((None, seq, w), lambda b, g: (b, 0, g)),
        out_shape=jax.ShapeDtypeStruct((bsz, seq, B_HEADS * LANES), BF16),
        scratch_shapes=[pltpu.VMEM((n_heads, TILE, 2 * LANES), F32),
                        pltpu.VMEM((n_heads, 2 * TILE, 2 * LANES), F32)],
        compiler_params=_cparams(2),
        name="diff_attn",
    )(proj, proj, proj, bias, lam_rows, gain.reshape(1, LANES))


def _stick_kernel(q_ref, k_ref, v_ref, tri_ref, o_ref, r_ref, acc_ref, *, n_pairs):
    nt = q_ref.shape[0] // TILE
    nsub = TILE // BLOCK
    row = lax.broadcasted_iota(jnp.int32, (TILE, 2 * TILE), 0)
    col = lax.broadcasted_iota(jnp.int32, (TILE, 2 * TILE), 1) & (TILE - 1)
    strict = col < row

    def q_tile(qi, carry):
        q0 = pl.multiple_of(qi * TILE, TILE)
        r_ref[...] = jnp.zeros(r_ref.shape, F32)
        acc_ref[...] = jnp.zeros(acc_ref.shape, F32)

        def step(tiles):
            k0s = [pl.multiple_of(kt * TILE, TILE) for kt, _ in tiles]
            work = [(p, t) for p in range(n_pairs) for t in range(len(tiles))]
            tri = tri_ref[...]
            z, hi, lo, res = {}, {}, {}, {}
            for p, t in work:
                cols = slice(p * LANES, (p + 1) * LANES)
                z[p, t] = _dot_nt(q_ref[pl.ds(q0, TILE), cols],
                                  _block_diag_rows(k_ref[pl.ds(k0s[t], TILE), cols]))
            for p, t in work:
                sp = jnp.maximum(z[p, t], 0.0) + jnp.log2(1.0 + jnp.exp2(-jnp.abs(z[p, t])))
                if tiles[t][1]:
                    sp = jnp.where(strict, sp, 0.0)
                hi[p, t] = sp.astype(BF16)
                lo[p, t] = (sp - hi[p, t].astype(F32)).astype(BF16)
            for p, t in work:
                for head in range(2):
                    for sub in range(nsub):
                        c0 = head * TILE + sub * BLOCK
                        res[p, t, head, sub] = jnp.dot(
                            jnp.concatenate([hi[p, t][:, c0:c0 + BLOCK], lo[p, t][:, c0:c0 + BLOCK]], axis=1),
                            tri, preferred_element_type=F32)
            weights, r_all = [], []
            for p in range(n_pairs):
                r = r_ref[p]
                r_heads = [r[:, :LANES], r[:, LANES:]]
                a_tiles = []
                for t in range(len(tiles)):
                    pieces = [None] * (2 * nsub)
                    for head in range(2):
                        for sub in reversed(range(nsub)):
                            pieces[head * nsub + sub] = res[p, t, head, sub][:, :BLOCK] + r_heads[head]
                            r_heads[head] = r_heads[head] + res[p, t, head, sub][:, BLOCK:]
                    a = jnp.exp2(z[p, t] - jnp.concatenate(pieces, axis=1))
                    if tiles[t][1]:
                        a = jnp.where(strict, a, 0.0)
                    a_tiles.append(a.astype(BF16))
                r_ref[p] = jnp.concatenate(r_heads, axis=1)
                r_all += r_heads
                weights.append(jnp.concatenate(a_tiles, axis=1))
            for p in range(n_pairs):
                cols = slice(p * LANES, (p + 1) * LANES)
                vbd = jnp.concatenate([_block_diag_rows(v_ref[pl.ds(k0, TILE), cols]) for k0 in k0s], axis=0)
                acc_ref[p] += jnp.dot(weights[p], vbd, preferred_element_type=F32)
            return jnp.min(functools.reduce(jnp.minimum, r_all))

        def back(state):
            kt, _ = state
            return kt - 1, step([(kt, False)])

        @pl.when(qi == 0)
        def _():
            step([(0, True)])

        @pl.when(qi >= 1)
        def _():
            r_low = step([(qi, True), (qi - 1, False)])
            lax.while_loop(lambda state: (state[0] >= 0) & (state[1] < R_STOP), back, (qi - 2, r_low))
        for p in range(n_pairs):
            o_ref[pl.ds(q0, TILE), p * LANES:(p + 1) * LANES] = acc_ref[p].astype(o_ref.dtype)
        return carry

    lax.fori_loop(0, nt, q_tile, 0)


def _stick(proj, tri, *, bsz, seq, n_pairs_total, n_pairs=2):
    groups = n_pairs_total // n_pairs
    w = n_pairs * LANES
    return pl.pallas_call(
        functools.partial(_stick_kernel, n_pairs=n_pairs),
        grid=(bsz, groups),
        in_specs=[pl.BlockSpec((None, seq, w), lambda b, g: (b, 0, g)),
                  pl.BlockSpec((None, seq, w), lambda b, g: (b, 0, groups + g)),
                  pl.BlockSpec((None, seq, w), lambda b, g: (b, 0, 2 * groups + g)),
                  pl.BlockSpec((2 * BLOCK, 2 * BLOCK), lambda b, g: (0, 0))],
        out_specs=pl.BlockSpec((None, seq, w), lambda b, g: (b, 0, g)),
        out_shape=jax.ShapeDtypeStruct((bsz, seq, n_pairs_total * LANES), BF16),
        scratch_shapes=[pltpu.VMEM((n_pairs, TILE, 2 * LANES), F32),
                        pltpu.VMEM((n_pairs, TILE, LANES), F32)],
        compiler_params=_cparams(2),
        name="stick_breaking",
    )(proj, proj, proj, tri)


def _t5_bucket(dist):
    n = jnp.maximum(dist, 0)
    nf = jnp.maximum(n, 1).astype(F32)
    large = MAX_EXACT + (jnp.log(nf / MAX_EXACT) / math.log(MAX_DISTANCE / MAX_EXACT)
                         * (N_BUCKETS - MAX_EXACT)).astype(jnp.int32)
    large = jnp.minimum(large, N_BUCKETS - 1)
    return jnp.where(n < MAX_EXACT, n, large)


def _band_bias(rel_bias, blk):
    qi = jnp.arange(blk)[:, None]
    kj = jnp.arange(2 * blk)[None, :]
    dist = qi + blk - kj
    onehot = jax.nn.one_hot(_t5_bucket(dist), N_BUCKETS, dtype=F32)
    return jnp.einsum("qkn,nh->hqk", onehot, rel_bias, precision=lax.Precision.HIGHEST), dist


def _swa_bias(rel_bias_a):
    band, dist = _band_bias(rel_bias_a, BLOCK)
    in_window = (dist >= 0) & (dist < WINDOW)
    general = jnp.where(in_window[None], band, NEG_INF)
    first = jnp.where((in_window & (jnp.arange(2 * BLOCK)[None, :] >= BLOCK))[None], band, NEG_INF)
    both = jnp.stack([first, general])
    return both.reshape(2, A_HEADS // 2, 2, BLOCK, 2 * BLOCK).transpose(0, 1, 3, 2, 4).reshape(
        2, A_HEADS // 2, BLOCK, 4 * BLOCK)


def _diff_bias(rel_bias_b):
    band, dist = _band_bias(rel_bias_b, TILE)
    band = (band - rel_bias_b[N_BUCKETS - 1][:, None, None]) * LOG2E
    diag = jnp.where((dist[:, TILE:] >= 0)[None], band[:, :, TILE:], NEG_INF)
    return jnp.stack([diag, band[:, :, :TILE]], axis=1)


def _even_in_weight(w):
    aq, ak, av, bq, bk, bv = jnp.split(w, [512, 640, 768, 1280, 1792], axis=1)
    dup = lambda t: jnp.concatenate(
        [t[:, h * HEAD_DIM:(h + 1) * HEAD_DIM] for h in range(A_KV_HEADS) for _ in range(2)], axis=1)
    return jnp.concatenate([aq * SCALE, dup(ak), dup(av), bq * (SCALE * LOG2E), bk, bv], axis=1).astype(BF16)


def _suffix_matrix():
    j = jnp.arange(BLOCK)[:, None]
    s = jnp.arange(BLOCK)[None, :]
    half = jnp.concatenate([(j >= s).astype(BF16), jnp.ones((BLOCK, BLOCK), BF16)], axis=1)
    return jnp.concatenate([half, half], axis=0)


def kernel(x, rel_bias, norm_mix, norm_ffn, norm_final, w_in_even, w_out_even, sinks, lam_q1, lam_k1, lam_q2,
           lam_k2, diff_norm, w_in_odd, w_out_odd, ffn_up, ffn_conv, ffn_conv_b, ffn_down):
    bsz, seq, d = x.shape
    depth = norm_mix.shape[0]
    dff = ffn_down.shape[1]
    cw = 256
    n_pairs_c = d // LANES
    n = bsz * seq

    swa_bias = _swa_bias(rel_bias[:, :A_HEADS])
    diff_bias = _diff_bias(rel_bias[:, A_HEADS:])
    tri = _suffix_matrix()

    xf = x.reshape(n, d)
    for layer in range(depth):
        if layer % 2 == 0:
            e = layer // 2
            proj = _norm_proj(xf, norm_mix[layer], _even_in_weight(w_in_even[e]))
            proj = proj.reshape(bsz, seq, proj.shape[1])
            oa = _swa(proj, sinks[e], swa_bias, bsz=bsz, seq=seq)
            lambda_init = 0.8 - 0.6 * math.exp(-0.3 * layer)
            pad = lambda v: jnp.pad(v, (0, LANES - HEAD_DIM))
            lam_rows = jnp.stack([pad(lam_q1[e]), pad(lam_k1[e]), pad(lam_q2[e]), pad(lam_k2[e])]
                                 + [jnp.zeros((LANES,), F32)] * (SUBLANES - 4))
            ob = _diff(proj, diff_bias, lam_rows, diff_norm[e], bsz=bsz, seq=seq, lambda_init=lambda_init)
            w_out = w_out_even[e].astype(BF16)
            half = oa.shape[-1]
            xf = _out_proj(xf, [oa.reshape(n, half), ob.reshape(n, half)], [w_out[:half], w_out[half:]])
        else:
            o = layer // 2
            w_in = jnp.concatenate([w_in_odd[o][:, :d] * (SCALE * LOG2E), w_in_odd[o][:, d:]],
                                   axis=1).astype(BF16)
            proj = _norm_proj(xf, norm_mix[layer], w_in).reshape(bsz, seq, 3 * d)
            oc = _stick(proj, tri, bsz=bsz, seq=seq, n_pairs_total=n_pairs_c)
            xf = _out_proj(xf, [oc.reshape(n, d)], [w_out_odd[o].astype(BF16)])
        xf = _ffn(xf, norm_ffn[layer], ffn_up[layer].astype(BF16), ffn_conv[layer],
                  ffn_conv_b[layer].reshape(1, 2 * dff), ffn_down[layer].astype(BF16), norm_final,
                  seq=seq, cw=cw, final_norm=(layer == depth - 1))
    return xf.reshape(bsz, seq, d)
```

```python
import functools
import math

import jax
import jax.numpy as jnp
from jax import lax
from jax.experimental import pallas as pl
from jax.experimental.pallas import tpu as pltpu

LANES = 128
SUBLANES = 8
VMEM_LIMIT_BYTES = 56 * 1024 * 1024

HEAD_DIM = 64
BLOCK = 128
TILE = 256
STICK_TILE = 128
STICK_MERGE = 3
SWA_UNROLL = 2
V_ROWS = LANES + 2 * SUBLANES
WINDOW = 128
A_HEADS, A_KV_HEADS = 8, 2
B_HEADS = 4
N_BUCKETS, MAX_EXACT, MAX_DISTANCE = 32, 16, 128
CONV_W = 3
EPS = 1e-6
SCALE = HEAD_DIM ** -0.5

LOG2E = math.log2(math.e)
EXP2_MAX = 126.0
R_STOP = 152.0

F32 = jnp.float32
BF16 = jnp.bfloat16
NEG_INF = float("-inf")


def _cparams(n_axes):
    return pltpu.CompilerParams(
        dimension_semantics=("arbitrary",) * n_axes, vmem_limit_bytes=VMEM_LIMIT_BYTES)


def _rms(x, g):
    ms = jnp.mean(x * x, axis=-1, keepdims=True)
    return x * lax.rsqrt(ms + EPS) * g


def _lane_halves(rows):
    lane = lax.broadcasted_iota(jnp.int32, (rows, LANES), 1)
    return lane < HEAD_DIM


def _block_diag_rows(t):
    lo = _lane_halves(t.shape[0])
    zero = jnp.zeros_like(t)
    return jnp.concatenate([jnp.where(lo, t, zero), jnp.where(lo, zero, t)], axis=0)


def _dot_nt(a, b):
    return lax.dot_general(a, b, (((1,), (1,)), ((), ())), preferred_element_type=F32)


def _norm_proj_kernel(x_ref, g_ref, w_ref, o_ref, *, col_chunk):
    h = _rms(x_ref[...], g_ref[...]).astype(BF16)
    for n0 in range(0, o_ref.shape[1], col_chunk):
        o_ref[:, n0:n0 + col_chunk] = jnp.dot(
            h, w_ref[:, n0:n0 + col_chunk], preferred_element_type=F32).astype(o_ref.dtype)


def _norm_proj(x, g, w, *, tm=512, col_chunk=512):
    n, d = x.shape
    nout = w.shape[1]
    return pl.pallas_call(
        functools.partial(_norm_proj_kernel, col_chunk=col_chunk),
        grid=(n // tm,),
        in_specs=[pl.BlockSpec((tm, d), lambda i: (i, 0)),
                  pl.BlockSpec((1, d), lambda i: (0, 0)),
                  pl.BlockSpec((d, nout), lambda i: (0, 0))],
        out_specs=pl.BlockSpec((tm, nout), lambda i: (i, 0)),
        out_shape=jax.ShapeDtypeStruct((n, nout), BF16),
        compiler_params=_cparams(1),
        name="norm_proj",
    )(x, g.reshape(1, d), w)


def _mix_ffn_kernel(*refs, n_mix, tiles_per_seq, n_chunks, cw, final_norm):
    x_ref = refs[0]
    o_refs, wo_refs = refs[1:1 + n_mix], refs[1 + n_mix:1 + 2 * n_mix]
    g_ref, wup_ref, wc_ref, bc_ref, wdn_ref, gf_ref, out_ref, tail_ref, act_ref = refs[1 + 2 * n_mix:]
    tm = x_ref.shape[0]
    dff = wdn_ref.shape[0]

    @pl.when((pl.program_id(0) % tiles_per_seq) == 0)
    def _():
        tail_ref[...] = jnp.zeros(tail_ref.shape, F32)

    x = x_ref[...]
    for o_ref, wo_ref in zip(o_refs, wo_refs):
        x = x + jnp.dot(o_ref[...], wo_ref[...], preferred_element_type=F32)
    h = _rms(x, g_ref[...]).astype(BF16)
    for c in range(n_chunks):
        conv = []
        for part in range(2):
            cols = slice(part * dff + c * cw, part * dff + (c + 1) * cw)
            u = jnp.dot(h, wup_ref[:, cols], preferred_element_type=F32)
            ext = jnp.concatenate([tail_ref[2 * c + part], u], axis=0)
            tail_ref[2 * c + part] = u[tm - SUBLANES:, :]
            u1 = pltpu.roll(ext, 1, axis=0)[SUBLANES:]
            u2 = pltpu.roll(ext, 2, axis=0)[SUBLANES:]
            wc = wc_ref[:, cols]
            conv.append((u2 * wc[0:1] + u1 * wc[1:2] + u * wc[2:3]) + bc_ref[:, cols])
        gate, val = conv
        act_ref[:, c * cw:(c + 1) * cw] = (gate * jax.nn.sigmoid(gate) * val).astype(BF16)
    y = x + jnp.dot(act_ref[...], wdn_ref[...], preferred_element_type=F32)
    if final_norm:
        y = _rms(y, gf_ref[...])
    out_ref[...] = y


def _mix_ffn(x, outs, w_outs, g, wup, wc, bc, wdn, g_final, *, seq, cw, tm=512, final_norm=False):
    n, d = x.shape
    dff = wdn.shape[0]
    n_chunks = dff // cw
    const = lambda i: (0, 0)
    rows = lambda i: (i, 0)
    resident = lambda a: pl.BlockSpec(a.shape, const, pipeline_mode=pl.Buffered(1))
    return pl.pallas_call(
        functools.partial(_mix_ffn_kernel, n_mix=len(outs), tiles_per_seq=seq // tm, n_chunks=n_chunks,
                          cw=cw, final_norm=final_norm),
        grid=(n // tm,),
        in_specs=([pl.BlockSpec((tm, d), rows)]
                  + [pl.BlockSpec((tm, o.shape[1]), rows) for o in outs]
                  + [resident(w) for w in w_outs]
                  + [pl.BlockSpec((1, d), const), resident(wup), pl.BlockSpec(wc.shape, const),
                     pl.BlockSpec(bc.shape, const), resident(wdn), pl.BlockSpec((1, d), const)]),
        out_specs=pl.BlockSpec((tm, d), rows),
        out_shape=jax.ShapeDtypeStruct((n, d), F32),
        scratch_shapes=[pltpu.VMEM((2 * n_chunks, SUBLANES, cw), F32),
                        pltpu.VMEM((tm, dff), BF16)],
        compiler_params=_cparams(1),
        name="mix_ffn",
    )(x, *outs, *w_outs, g.reshape(1, d), wup, wc, bc, wdn, g_final.reshape(1, d))


def _swa_kernel(sink_ref, q_ref, k_ref, v_ref, bias_ref, o_ref, *, n_pairs, pairs_per_kv):
    nb = q_ref.shape[0] // BLOCK

    def band(ref, j, kv_cols):
        cur = pl.multiple_of(j * BLOCK, BLOCK)
        prev = pl.multiple_of(jnp.maximum(j - 1, 0) * BLOCK, BLOCK)
        return jnp.concatenate([ref[pl.ds(prev, BLOCK), kv_cols], ref[pl.ds(cur, BLOCK), kv_cols]], axis=0)

    def body(i, carry):
        chains = [(i * SWA_UNROLL + u, c) for u in range(SWA_UNROLL) for c in range(n_pairs)]
        kv = lambda c: slice((c // pairs_per_kv) * LANES, (c // pairs_per_kv + 1) * LANES)
        scores = []
        for j, c in chains:
            q2 = q_ref[pl.ds(pl.multiple_of(j * BLOCK, BLOCK), BLOCK), c * LANES:(c + 1) * LANES]
            scores.append(_dot_nt(q2, _block_diag_rows(band(k_ref, j, kv(c)))) + bias_ref[jnp.minimum(j, 1), c])
        weights = []
        for (j, c), s in zip(chains, scores):
            w_halves = []
            for i2 in range(2):
                half = s[:, i2 * 2 * BLOCK:(i2 + 1) * 2 * BLOCK]
                sink = sink_ref[2 * c + i2]
                m = jnp.maximum(jnp.max(half, axis=-1, keepdims=True), sink)
                p = jnp.exp(half - m)
                w_halves.append(p / (jnp.sum(p, axis=-1, keepdims=True) + jnp.exp(sink - m)))
            weights.append(jnp.concatenate(w_halves, axis=1).astype(BF16))
        for (j, c), w in zip(chains, weights):
            o = jnp.dot(w, _block_diag_rows(band(v_ref, j, kv(c))), preferred_element_type=F32)
            o_ref[pl.ds(pl.multiple_of(j * BLOCK, BLOCK), BLOCK), c * LANES:(c + 1) * LANES] = (
                o.astype(o_ref.dtype))
        return carry

    lax.fori_loop(0, nb // SWA_UNROLL, body, 0)


def _swa(proj, sinks, bias, *, bsz, seq):
    n_pairs = A_HEADS // 2
    pairs_per_kv = n_pairs // A_KV_HEADS
    q_w, kv_w = n_pairs * LANES, A_KV_HEADS * LANES
    return pl.pallas_call(
        functools.partial(_swa_kernel, n_pairs=n_pairs, pairs_per_kv=pairs_per_kv),
        grid=(bsz,),
        in_specs=[pl.BlockSpec(memory_space=pltpu.SMEM),
                  pl.BlockSpec((None, seq, q_w), lambda b: (b, 0, 0)),
                  pl.BlockSpec((None, seq, kv_w), lambda b: (b, 0, q_w // kv_w)),
                  pl.BlockSpec((None, seq, kv_w), lambda b: (b, 0, q_w // kv_w + 1)),
                  pl.BlockSpec(bias.shape, lambda b: (0, 0, 0, 0))],
        out_specs=pl.BlockSpec((None, seq, q_w), lambda b: (b, 0, 0)),
        out_shape=jax.ShapeDtypeStruct((bsz, seq, q_w), BF16),
        compiler_params=_cparams(1),
        name="swa_sink",
    )(sinks, proj, proj, proj, bias)


def _diff_kernel(q_ref, k_ref, v_ref, bias_ref, lam_ref, gain_ref, o_ref, vt_ref, m_ref, acc_ref, *,
                 lambda_init, n_heads):
    nt = q_ref.shape[0] // TILE
    lp = lam_ref[...]
    lam = (jnp.exp(jnp.sum(lp[0:1] * lp[1:2], axis=-1, keepdims=True))
           - jnp.exp(jnp.sum(lp[2:3] * lp[3:4], axis=-1, keepdims=True)) + lambda_init)

    def transpose_v(kt, carry):
        k0 = pl.multiple_of(kt * TILE, TILE)
        for h in range(n_heads):
            vt = v_ref[pl.ds(k0, TILE), h * LANES:(h + 1) * LANES].astype(F32).T
            vt_ref[h, kt] = jnp.concatenate([vt.astype(BF16), jnp.ones((V_ROWS - LANES, TILE), BF16)], axis=0)
        return carry

    lax.fori_loop(0, nt, transpose_v, 0)

    def q_tile(qi, carry):
        q0 = pl.multiple_of(qi * TILE, TILE)
        m_ref[...] = jnp.full(m_ref.shape, NEG_INF, F32)
        acc_ref[...] = jnp.zeros(acc_ref.shape, F32)

        def step(tiles):
            scores = {}
            for h in range(n_heads):
                cols = slice(h * LANES, (h + 1) * LANES)
                qh = q_ref[pl.ds(q0, TILE), cols]
                for t, (kt, bias_idx) in enumerate(tiles):
                    kbd = _block_diag_rows(k_ref[pl.ds(pl.multiple_of(kt * TILE, TILE), TILE), cols])
                    st = _dot_nt(kbd, qh)
                    halves = [st[:TILE], st[TILE:]]
                    if bias_idx is not None:
                        halves = [x + bias_ref[h, bias_idx] for x in halves]
                    scores[h, t] = halves
            probs, alphas = [], []
            for h in range(n_heads):
                m_old = m_ref[h]
                m_blk = jnp.concatenate(
                    [functools.reduce(jnp.maximum, [jnp.max(scores[h, t][i], axis=0, keepdims=True)
                                                    for t in range(len(tiles))]) for i in range(2)], axis=1)
                m_new = jnp.maximum(m_old, m_blk)
                alphas.append(jnp.exp2(m_old - m_new)[0:1])
                m_ref[h] = m_new
                probs.append(jnp.concatenate(
                    [jnp.concatenate([jnp.exp2(scores[h, t][i] - m_new[0:1, i * TILE:(i + 1) * TILE]
                                               ).astype(BF16) for i in range(2)], axis=1)
                     for t in range(len(tiles))], axis=0))
            for h in range(n_heads):
                vt = jnp.concatenate([vt_ref[h, kt] for kt, _ in tiles], axis=1)
                pv = jnp.dot(vt, probs[h], preferred_element_type=F32)
                acc_ref[h] = acc_ref[h] * alphas[h] + pv

        n_far = jnp.maximum(qi - 1, 0)

        @pl.when(n_far % 2 == 1)
        def _():
            step([(0, None)])

        def far_pair(i, c2):
            kt = n_far % 2 + 2 * i
            step([(kt, None), (kt + 1, None)])
            return c2

        lax.fori_loop(0, n_far // 2, far_pair, 0)

        @pl.when(qi == 0)
        def _():
            step([(0, 0)])

        @pl.when(qi >= 1)
        def _():
            step([(qi - 1, 1), (qi, 0)])

        for h in range(n_heads):
            acc = acc_ref[h]
            o1 = acc[:LANES, :TILE] / acc[LANES:LANES + 1, :TILE]
            o2 = acc[:LANES, TILE:] / acc[LANES:LANES + 1, TILE:]
            o = _rms((o1 - lam * o2).T, gain_ref[...]) * (1.0 - lambda_init)
            o_ref[pl.ds(q0, TILE), h * LANES:(h + 1) * LANES] = o.astype(o_ref.dtype)
        return carry

    lax.fori_loop(0, nt, q_tile, 0)


def _diff(proj, bias, lam_rows, gain, *, bsz, seq, lambda_init, n_heads=2):
    q_col0 = A_HEADS // 2 + 2 * A_KV_HEADS
    groups = B_HEADS // n_heads
    w = n_heads * LANES
    qb, kb, vb = (c // n_heads for c in (q_col0, q_col0 + B_HEADS, q_col0 + 2 * B_HEADS))
    return pl.pallas_call(
        functools.partial(_diff_kernel, lambda_init=lambda_init, n_heads=n_heads),
        grid=(bsz, groups),
        in_specs=[pl.BlockSpec((None, seq, w), lambda b, g: (b, 0, qb + g)),
                  pl.BlockSpec((None, seq, w), lambda b, g: (b, 0, kb + g)),
                  pl.BlockSpec((None, seq, w), lambda b, g: (b, 0, vb + g)),
                  pl.BlockSpec((n_heads, 2, TILE, TILE), lambda b, g: (g, 0, 0, 0)),
                  pl.BlockSpec((SUBLANES, LANES), lambda b, g: (0, 0)),
                  pl.BlockSpec((1, LANES), lambda b, g: (0, 0))],
        out_specs=pl.BlockSpec((None, seq, w), lambda b, g: (b, 0, g)),
        out_shape=jax.ShapeDtypeStruct((bsz, seq, B_HEADS * LANES), BF16),
        scratch_shapes=[pltpu.VMEM((n_heads, seq // TILE, V_ROWS, TILE), BF16),
                        pltpu.VMEM((n_heads, SUBLANES, 2 * TILE), F32),
                        pltpu.VMEM((n_heads, V_ROWS, 2 * TILE), F32)],
        compiler_params=_cparams(2),
        name="diff_attn",
    )(proj, proj, proj, bias, lam_rows, gain.reshape(1, LANES))


def _stick_kernel(q_ref, k_ref, v_ref, tri_ref, o_ref, r_ref, acc_ref, *, n_pairs):
    TILE = STICK_TILE
    nt = q_ref.shape[0] // TILE
    nsub = TILE // BLOCK
    row = lax.broadcasted_iota(jnp.int32, (TILE, 2 * TILE), 0)
    col = lax.broadcasted_iota(jnp.int32, (TILE, 2 * TILE), 1) & (TILE - 1)
    strict = col < row

    def q_tile(qi, carry):
        q0 = pl.multiple_of(qi * TILE, TILE)
        r_ref[...] = jnp.zeros(r_ref.shape, F32)
        acc_ref[...] = jnp.zeros(acc_ref.shape, F32)

        def step(tiles):
            k0s = [pl.multiple_of(kt * TILE, TILE) for kt, _ in tiles]
            work = [(p, t) for p in range(n_pairs) for t in range(len(tiles))]
            tri = tri_ref[...]
            z, hi, lo, res = {}, {}, {}, {}
            for p, t in work:
                cols = slice(p * LANES, (p + 1) * LANES)
                z[p, t] = _dot_nt(q_ref[pl.ds(q0, TILE), cols],
                                  _block_diag_rows(k_ref[pl.ds(k0s[t], TILE), cols]))
            for p, t in work:
                sp = jnp.maximum(z[p, t], jnp.log2(1.0 + jnp.exp2(jnp.minimum(z[p, t], EXP2_MAX))))
                if tiles[t][1]:
                    sp = jnp.where(strict, sp, 0.0)
                hi[p, t] = sp.astype(BF16)
                lo[p, t] = (sp - hi[p, t].astype(F32)).astype(BF16)
            for p, t in work:
                for head in range(2):
                    for sub in range(nsub):
                        c0 = head * TILE + sub * BLOCK
                        res[p, t, head, sub] = jnp.dot(
                            jnp.concatenate([hi[p, t][:, c0:c0 + BLOCK], lo[p, t][:, c0:c0 + BLOCK]], axis=1),
                            tri, preferred_element_type=F32)
            weights, r_all = [], []
            for p in range(n_pairs):
                r = r_ref[p]
                r_heads = [r[:, :LANES], r[:, LANES:]]
                a_tiles = []
                for t in range(len(tiles)):
                    pieces = [None] * (2 * nsub)
                    for head in range(2):
                        for sub in reversed(range(nsub)):
                            pieces[head * nsub + sub] = res[p, t, head, sub][:, :BLOCK] + r_heads[head]
                            r_heads[head] = r_heads[head] + res[p, t, head, sub][:, BLOCK:]
                    a = jnp.exp2(z[p, t] - jnp.concatenate(pieces, axis=1))
                    if tiles[t][1]:
                        a = jnp.where(strict, a, 0.0)
                    a_tiles.append(a.astype(BF16))
                r_ref[p] = jnp.concatenate(r_heads, axis=1)
                r_all += r_heads
                weights.append(jnp.concatenate(a_tiles, axis=1))
            for p in range(n_pairs):
                cols = slice(p * LANES, (p + 1) * LANES)
                vbd = jnp.concatenate([_block_diag_rows(v_ref[pl.ds(k0, TILE), cols]) for k0 in k0s], axis=0)
                acc_ref[p] += jnp.dot(weights[p], vbd, preferred_element_type=F32)
            return jnp.min(functools.reduce(jnp.minimum, r_all))

        def back(state):
            kt, _ = state
            return kt - 1, step([(kt, False)])

        for n in range(1, STICK_MERGE):
            @pl.when(qi == n - 1)
            def _(n=n):
                step([(n - 1 - i, i == 0) for i in range(n)])

        @pl.when(qi >= STICK_MERGE - 1)
        def _():
            r_low = step([(qi - i, i == 0) for i in range(STICK_MERGE)])
            lax.while_loop(lambda state: (state[0] >= 0) & (state[1] < R_STOP), back, (qi - STICK_MERGE, r_low))
        for p in range(n_pairs):
            o_ref[pl.ds(q0, TILE), p * LANES:(p + 1) * LANES] = acc_ref[p].astype(o_ref.dtype)
        return carry

    lax.fori_loop(0, nt, q_tile, 0)


def _stick(proj, tri, *, bsz, seq, n_pairs_total, n_pairs=4):
    groups = n_pairs_total // n_pairs
    w = n_pairs * LANES
    return pl.pallas_call(
        functools.partial(_stick_kernel, n_pairs=n_pairs),
        grid=(bsz, groups),
        in_specs=[pl.BlockSpec((None, seq, w), lambda b, g: (b, 0, g)),
                  pl.BlockSpec((None, seq, w), lambda b, g: (b, 0, groups + g)),
                  pl.BlockSpec((None, seq, w), lambda b, g: (b, 0, 2 * groups + g)),
                  pl.BlockSpec((2 * BLOCK, 2 * BLOCK), lambda b, g: (0, 0))],
        out_specs=pl.BlockSpec((None, seq, w), lambda b, g: (b, 0, g)),
        out_shape=jax.ShapeDtypeStruct((bsz, seq, n_pairs_total * LANES), BF16),
        scratch_shapes=[pltpu.VMEM((n_pairs, STICK_TILE, 2 * LANES), F32),
                        pltpu.VMEM((n_pairs, STICK_TILE, LANES), F32)],
        compiler_params=_cparams(2),
        name="stick_breaking",
    )(proj, proj, proj, tri)


def _t5_bucket(dist):
    n = jnp.maximum(dist, 0)
    nf = jnp.maximum(n, 1).astype(F32)
    large = MAX_EXACT + (jnp.log(nf / MAX_EXACT) / math.log(MAX_DISTANCE / MAX_EXACT)
                         * (N_BUCKETS - MAX_EXACT)).astype(jnp.int32)
    large = jnp.minimum(large, N_BUCKETS - 1)
    return jnp.where(n < MAX_EXACT, n, large)


def _band_bias(rel_bias, blk):
    qi = jnp.arange(blk)[:, None]
    kj = jnp.arange(2 * blk)[None, :]
    dist = qi + blk - kj
    onehot = jax.nn.one_hot(_t5_bucket(dist), N_BUCKETS, dtype=F32)
    return jnp.einsum("qkn,nh->hqk", onehot, rel_bias, precision=lax.Precision.HIGHEST), dist


def _swa_bias(rel_bias_a):
    band, dist = _band_bias(rel_bias_a, BLOCK)
    in_window = (dist >= 0) & (dist < WINDOW)
    general = jnp.where(in_window[None], band, NEG_INF)
    first = jnp.where((in_window & (jnp.arange(2 * BLOCK)[None, :] >= BLOCK))[None], band, NEG_INF)
    both = jnp.stack([first, general])
    return both.reshape(2, A_HEADS // 2, 2, BLOCK, 2 * BLOCK).transpose(0, 1, 3, 2, 4).reshape(
        2, A_HEADS // 2, BLOCK, 4 * BLOCK)


def _diff_bias(rel_bias_b):
    band, dist = _band_bias(rel_bias_b, TILE)
    band = (band - rel_bias_b[N_BUCKETS - 1][:, None, None]) * LOG2E
    diag = jnp.where((dist[:, TILE:] >= 0)[None], band[:, :, TILE:], NEG_INF)
    return jnp.stack([diag, band[:, :, :TILE]], axis=1).swapaxes(-1, -2)


def _even_in_weight(w):
    aq, ak, av, bq, bk, bv = jnp.split(w, [512, 640, 768, 1280, 1792], axis=1)
    dup = lambda t: jnp.concatenate(
        [t[:, h * HEAD_DIM:(h + 1) * HEAD_DIM] for h in range(A_KV_HEADS) for _ in range(2)], axis=1)
    return jnp.concatenate([aq * SCALE, dup(ak), dup(av), bq * (SCALE * LOG2E), bk, bv], axis=1).astype(BF16)


def _suffix_matrix():
    j = jnp.arange(BLOCK)[:, None]
    s = jnp.arange(BLOCK)[None, :]
    half = jnp.concatenate([(j >= s).astype(BF16), jnp.ones((BLOCK, BLOCK), BF16)], axis=1)
    return jnp.concatenate([half, half], axis=0)


def kernel(x, rel_bias, norm_mix, norm_ffn, norm_final, w_in_even, w_out_even, sinks, lam_q1, lam_k1, lam_q2,
           lam_k2, diff_norm, w_in_odd, w_out_odd, ffn_up, ffn_conv, ffn_conv_b, ffn_down):
    bsz, seq, d = x.shape
    depth = norm_mix.shape[0]
    dff = ffn_down.shape[1]
    cw = 256
    n_pairs_c = d // LANES
    n = bsz * seq

    swa_bias = _swa_bias(rel_bias[:, :A_HEADS])
    diff_bias = _diff_bias(rel_bias[:, A_HEADS:])
    tri = _suffix_matrix()

    xf = x.reshape(n, d)
    for layer in range(depth):
        if layer % 2 == 0:
            e = layer // 2
            proj = _norm_proj(xf, norm_mix[layer], _even_in_weight(w_in_even[e]))
            proj = proj.reshape(bsz, seq, proj.shape[1])
            oa = _swa(proj, sinks[e], swa_bias, bsz=bsz, seq=seq)
            lambda_init = 0.8 - 0.6 * math.exp(-0.3 * layer)
            pad = lambda v: jnp.pad(v, (0, LANES - HEAD_DIM))
            lam_rows = jnp.stack([pad(lam_q1[e]), pad(lam_k1[e]), pad(lam_q2[e]), pad(lam_k2[e])]
                                 + [jnp.zeros((LANES,), F32)] * (SUBLANES - 4))
            ob = _diff(proj, diff_bias, lam_rows, diff_norm[e], bsz=bsz, seq=seq, lambda_init=lambda_init)
            w_out = w_out_even[e].astype(BF16)
            half = oa.shape[-1]
            mixed, w_outs = [oa.reshape(n, half), ob.reshape(n, half)], [w_out[:half], w_out[half:]]
        else:
            o = layer // 2
            w_in = jnp.concatenate([w_in_odd[o][:, :d] * (SCALE * LOG2E), w_in_odd[o][:, d:]],
                                   axis=1).astype(BF16)
            proj = _norm_proj(xf, norm_mix[layer], w_in).reshape(bsz, seq, 3 * d)
            oc = _stick(proj, tri, bsz=bsz, seq=seq, n_pairs_total=n_pairs_c)
            mixed, w_outs = [oc.reshape(n, d)], [w_out_odd[o].astype(BF16)]
        xf = _mix_ffn(xf, mixed, w_outs, norm_ffn[layer], ffn_up[layer].astype(BF16), ffn_conv[layer],
                      ffn_conv_b[layer].reshape(1, 2 * dff), ffn_down[layer].astype(BF16), norm_final,
                      seq=seq, cw=cw, final_norm=(layer == depth - 1))
    return xf.reshape(bsz, seq, d)
```

```python
import functools
import math

import jax
import jax.numpy as jnp
from jax import lax
from jax.experimental import pallas as pl
from jax.experimental.pallas import tpu as pltpu

LANES = 128
SUBLANES = 8
VMEM_LIMIT_BYTES = 56 * 1024 * 1024

HEAD_DIM = 64
BLOCK = 128
TILE = 256
STICK_TILE = 128
STICK_MERGE = 3
SWA_UNROLL = 2
V_ROWS = LANES + 2 * SUBLANES
WINDOW = 128
A_HEADS, A_KV_HEADS = 8, 2
B_HEADS = 4
N_BUCKETS, MAX_EXACT, MAX_DISTANCE = 32, 16, 128
CONV_W = 3
EPS = 1e-6
SCALE = HEAD_DIM ** -0.5

LOG2E = math.log2(math.e)
EXP2_MAX = 126.0
R_STOP = 152.0

F32 = jnp.float32
BF16 = jnp.bfloat16
NEG_INF = float("-inf")


def _cparams(n_axes):
    return pltpu.CompilerParams(
        dimension_semantics=("arbitrary",) * n_axes, vmem_limit_bytes=VMEM_LIMIT_BYTES)


def _rms(x, g):
    ms = jnp.mean(x * x, axis=-1, keepdims=True)
    return x * lax.rsqrt(ms + EPS) * g


def _lane_halves(rows):
    lane = lax.broadcasted_iota(jnp.int32, (rows, LANES), 1)
    return lane < HEAD_DIM


def _block_diag_rows(t):
    lo = _lane_halves(t.shape[0])
    zero = jnp.zeros_like(t)
    return jnp.concatenate([jnp.where(lo, t, zero), jnp.where(lo, zero, t)], axis=0)


def _dot_nt(a, b):
    return lax.dot_general(a, b, (((1,), (1,)), ((), ())), preferred_element_type=F32)


def _norm_proj_kernel(x_ref, g_ref, w_ref, o_ref, *, col_chunk):
    h = _rms(x_ref[...], g_ref[...]).astype(BF16)
    for n0 in range(0, o_ref.shape[1], col_chunk):
        o_ref[:, n0:n0 + col_chunk] = jnp.dot(
            h, w_ref[:, n0:n0 + col_chunk], preferred_element_type=F32).astype(o_ref.dtype)


def _norm_proj(x, g, w, *, tm=512, col_chunk=512):
    n, d = x.shape
    nout = w.shape[1]
    return pl.pallas_call(
        functools.partial(_norm_proj_kernel, col_chunk=col_chunk),
        grid=(n // tm,),
        in_specs=[pl.BlockSpec((tm, d), lambda i: (i, 0)),
                  pl.BlockSpec((1, d), lambda i: (0, 0)),
                  pl.BlockSpec((d, nout), lambda i: (0, 0))],
        out_specs=pl.BlockSpec((tm, nout), lambda i: (i, 0)),
        out_shape=jax.ShapeDtypeStruct((n, nout), BF16),
        compiler_params=_cparams(1),
        name="norm_proj",
    )(x, g.reshape(1, d), w)


def _mix_ffn_kernel(*refs, n_mix, tiles_per_seq, n_chunks, cw, final_norm):
    x_ref = refs[0]
    o_refs, wo_refs = refs[1:1 + n_mix], refs[1 + n_mix:1 + 2 * n_mix]
    g_ref, wup_ref, wc_ref, bc_ref, wdn_ref, gf_ref, out_ref, tail_ref, act_ref = refs[1 + 2 * n_mix:]
    tm = x_ref.shape[0]
    dff = wdn_ref.shape[0]

    @pl.when((pl.program_id(0) % tiles_per_seq) == 0)
    def _():
        tail_ref[...] = jnp.zeros(tail_ref.shape, F32)

    x = x_ref[...]
    for o_ref, wo_ref in zip(o_refs, wo_refs):
        x = x + jnp.dot(o_ref[...], wo_ref[...], preferred_element_type=F32)
    h = _rms(x, g_ref[...]).astype(BF16)
    for c in range(n_chunks):
        conv = []
        for part in range(2):
            cols = slice(part * dff + c * cw, part * dff + (c + 1) * cw)
            u = jnp.dot(h, wup_ref[:, cols], preferred_element_type=F32)
            ext = jnp.concatenate([tail_ref[2 * c + part], u], axis=0)
            tail_ref[2 * c + part] = u[tm - SUBLANES:, :]
            u1 = pltpu.roll(ext, 1, axis=0)[SUBLANES:]
            u2 = pltpu.roll(ext, 2, axis=0)[SUBLANES:]
            wc = wc_ref[:, cols]
            conv.append((u2 * wc[0:1] + u1 * wc[1:2] + u * wc[2:3]) + bc_ref[:, cols])
        gate, val = conv
        act_ref[:, c * cw:(c + 1) * cw] = (gate * jax.nn.sigmoid(gate) * val).astype(BF16)
    y = x + jnp.dot(act_ref[...], wdn_ref[...], preferred_element_type=F32)
    if final_norm:
        y = _rms(y, gf_ref[...])
    out_ref[...] = y


def _mix_ffn(x, outs, w_outs, g, wup, wc, bc, wdn, g_final, *, seq, cw, tm=512, final_norm=False):
    n, d = x.shape
    dff = wdn.shape[0]
    n_chunks = dff // cw
    const = lambda i: (0, 0)
    rows = lambda i: (i, 0)
    resident = lambda a: pl.BlockSpec(a.shape, const, pipeline_mode=pl.Buffered(1))
    return pl.pallas_call(
        functools.partial(_mix_ffn_kernel, n_mix=len(outs), tiles_per_seq=seq // tm, n_chunks=n_chunks,
                          cw=cw, final_norm=final_norm),
        grid=(n // tm,),
        in_specs=([pl.BlockSpec((tm, d), rows)]
                  + [pl.BlockSpec((tm, o.shape[1]), rows) for o in outs]
                  + [resident(w) for w in w_outs]
                  + [pl.BlockSpec((1, d), const), resident(wup), pl.BlockSpec(wc.shape, const),
                     pl.BlockSpec(bc.shape, const), resident(wdn), pl.BlockSpec((1, d), const)]),
        out_specs=pl.BlockSpec((tm, d), rows),
        out_shape=jax.ShapeDtypeStruct((n, d), F32),
        scratch_shapes=[pltpu.VMEM((2 * n_chunks, SUBLANES, cw), F32),
                        pltpu.VMEM((tm, dff), BF16)],
        compiler_params=_cparams(1),
        name="mix_ffn",
    )(x, *outs, *w_outs, g.reshape(1, d), wup, wc, bc, wdn, g_final.reshape(1, d))


def _swa_kernel(sink_ref, q_ref, k_ref, v_ref, bias_ref, o_ref, *, n_pairs, pairs_per_kv):
    nb = q_ref.shape[0] // BLOCK

    def band(ref, j, kv_cols):
        cur = pl.multiple_of(j * BLOCK, BLOCK)
        prev = pl.multiple_of(jnp.maximum(j - 1, 0) * BLOCK, BLOCK)
        return jnp.concatenate([ref[pl.ds(prev, BLOCK), kv_cols], ref[pl.ds(cur, BLOCK), kv_cols]], axis=0)

    def body(i, carry):
        chains = [(i * SWA_UNROLL + u, c) for u in range(SWA_UNROLL) for c in range(n_pairs)]
        kv = lambda c: slice((c // pairs_per_kv) * LANES, (c // pairs_per_kv + 1) * LANES)
        scores = []
        for j, c in chains:
            q2 = q_ref[pl.ds(pl.multiple_of(j * BLOCK, BLOCK), BLOCK), c * LANES:(c + 1) * LANES]
            scores.append(_dot_nt(q2, _block_diag_rows(band(k_ref, j, kv(c)))) + bias_ref[jnp.minimum(j, 1), c])
        weights = []
        for (j, c), s in zip(chains, scores):
            w_halves = []
            for i2 in range(2):
                half = s[:, i2 * 2 * BLOCK:(i2 + 1) * 2 * BLOCK]
                sink = sink_ref[2 * c + i2]
                m = jnp.maximum(jnp.max(half, axis=-1, keepdims=True), sink)
                p = jnp.exp(half - m)
                w_halves.append(p / (jnp.sum(p, axis=-1, keepdims=True) + jnp.exp(sink - m)))
            weights.append(jnp.concatenate(w_halves, axis=1).astype(BF16))
        for (j, c), w in zip(chains, weights):
            o = jnp.dot(w, _block_diag_rows(band(v_ref, j, kv(c))), preferred_element_type=F32)
            o_ref[pl.ds(pl.multiple_of(j * BLOCK, BLOCK), BLOCK), c * LANES:(c + 1) * LANES] = (
                o.astype(o_ref.dtype))
        return carry

    lax.fori_loop(0, nb // SWA_UNROLL, body, 0)


def _swa(proj, sinks, bias, *, bsz, seq):
    n_pairs = A_HEADS // 2
    pairs_per_kv = n_pairs // A_KV_HEADS
    q_w, kv_w = n_pairs * LANES, A_KV_HEADS * LANES
    return pl.pallas_call(
        functools.partial(_swa_kernel, n_pairs=n_pairs, pairs_per_kv=pairs_per_kv),
        grid=(bsz,),
        in_specs=[pl.BlockSpec(memory_space=pltpu.SMEM),
                  pl.BlockSpec((None, seq, q_w), lambda b: (b, 0, 0)),
                  pl.BlockSpec((None, seq, kv_w), lambda b: (b, 0, q_w // kv_w)),
                  pl.BlockSpec((None, seq, kv_w), lambda b: (b, 0, q_w // kv_w + 1)),
                  pl.BlockSpec(bias.shape, lambda b: (0, 0, 0, 0))],
        out_specs=pl.BlockSpec((None, seq, q_w), lambda b: (b, 0, 0)),
        out_shape=jax.ShapeDtypeStruct((bsz, seq, q_w), BF16),
        compiler_params=_cparams(1),
        name="swa_sink",
    )(sinks, proj, proj, proj, bias)


def _diff_kernel(q_ref, k_ref, v_ref, bias_ref, lam_ref, gain_ref, o_ref, vt_ref, m_ref, acc_ref, s_ref, *,
                 lambda_init, n_heads):
    nt = q_ref.shape[0] // TILE
    lp = lam_ref[...]
    lam = (jnp.exp(jnp.sum(lp[0:1] * lp[1:2], axis=-1, keepdims=True))
           - jnp.exp(jnp.sum(lp[2:3] * lp[3:4], axis=-1, keepdims=True)) + lambda_init)

    def transpose_v(kt, carry):
        k0 = pl.multiple_of(kt * TILE, TILE)
        for h in range(n_heads):
            vt = v_ref[pl.ds(k0, TILE), h * LANES:(h + 1) * LANES].astype(F32).T
            vt_ref[h, kt] = jnp.concatenate([vt.astype(BF16), jnp.ones((V_ROWS - LANES, TILE), BF16)], axis=0)
        return carry

    lax.fori_loop(0, nt, transpose_v, 0)

    def q_tile(qi, carry):
        q0 = pl.multiple_of(qi * TILE, TILE)
        m_ref[...] = jnp.full(m_ref.shape, NEG_INF, F32)
        acc_ref[...] = jnp.zeros(acc_ref.shape, F32)

        def raw_scores(h, kt):
            cols = slice(h * LANES, (h + 1) * LANES)
            kbd = _block_diag_rows(k_ref[pl.ds(pl.multiple_of(kt * TILE, TILE), TILE), cols])
            return _dot_nt(kbd, q_ref[pl.ds(q0, TILE), cols])

        def update(get_scores, tiles):
            scores = {}
            for h in range(n_heads):
                for t, (_, bias_idx) in enumerate(tiles):
                    st = get_scores(h, t)
                    halves = [st[:TILE], st[TILE:]]
                    if bias_idx is not None:
                        halves = [x + bias_ref[h, bias_idx] for x in halves]
                    scores[h, t] = halves
            probs, alphas = [], []
            for h in range(n_heads):
                m_old = m_ref[h]
                m_blk = jnp.concatenate(
                    [functools.reduce(jnp.maximum, [jnp.max(scores[h, t][i], axis=0, keepdims=True)
                                                    for t in range(len(tiles))]) for i in range(2)], axis=1)
                m_new = jnp.maximum(m_old, m_blk)
                alphas.append(jnp.exp2(m_old - m_new)[0:1])
                m_ref[h] = m_new
                probs.append(jnp.concatenate(
                    [jnp.concatenate([jnp.exp2(scores[h, t][i] - m_new[0:1, i * TILE:(i + 1) * TILE]
                                               ).astype(BF16) for i in range(2)], axis=1)
                     for t in range(len(tiles))], axis=0))
            for h in range(n_heads):
                vt = jnp.concatenate([vt_ref[h, kt] for kt, _ in tiles], axis=1)
                pv = jnp.dot(vt, probs[h], preferred_element_type=F32)
                acc_ref[h] = acc_ref[h] * alphas[h] + pv

        def step(tiles):
            update(lambda h, t: raw_scores(h, tiles[t][0]), tiles)

        def fill(slot, kt):
            for h in range(n_heads):
                for t in range(2):
                    s_ref[slot, h, t] = raw_scores(h, kt + t)

        n_far = jnp.maximum(qi - 1, 0)
        first = n_far % 2
        n_pairs = n_far // 2

        @pl.when(first == 1)
        def _():
            step([(0, None)])

        @pl.when(qi == 0)
        def _():
            step([(0, 0)])

        @pl.when(qi >= 1)
        def _():
            from_slot = lambda slot: (lambda h, t: s_ref[slot, h, t])
            last = [(qi - 1, 1), (qi, 0)]
            fill(0, first)

            def two_pairs(j, c2):
                kt = first + 4 * j
                fill(1, kt + 2)
                update(from_slot(0), [(kt, None), (kt + 1, None)])
                fill(0, kt + 4)
                update(from_slot(1), [(kt + 2, None), (kt + 3, None)])
                return c2

            lax.fori_loop(0, n_pairs // 2, two_pairs, 0)

            @pl.when(n_pairs % 2 == 1)
            def _():
                kt = qi - 3
                fill(1, kt + 2)
                update(from_slot(0), [(kt, None), (kt + 1, None)])
                update(from_slot(1), last)

            @pl.when(n_pairs % 2 == 0)
            def _():
                update(from_slot(0), last)

        for h in range(n_heads):
            acc = acc_ref[h]
            o1 = acc[:LANES, :TILE] / acc[LANES:LANES + 1, :TILE]
            o2 = acc[:LANES, TILE:] / acc[LANES:LANES + 1, TILE:]
            o = _rms((o1 - lam * o2).T, gain_ref[...]) * (1.0 - lambda_init)
            o_ref[pl.ds(q0, TILE), h * LANES:(h + 1) * LANES] = o.astype(o_ref.dtype)
        return carry

    lax.fori_loop(0, nt, q_tile, 0)


def _diff(proj, bias, lam_rows, gain, *, bsz, seq, lambda_init, n_heads=4):
    q_col0 = A_HEADS // 2 + 2 * A_KV_HEADS
    groups = B_HEADS // n_heads
    w = n_heads * LANES
    qb, kb, vb = (c // n_heads for c in (q_col0, q_col0 + B_HEADS, q_col0 + 2 * B_HEADS))
    return pl.pallas_call(
        functools.partial(_diff_kernel, lambda_init=lambda_init, n_heads=n_heads),
        grid=(bsz, groups),
        in_specs=[pl.BlockSpec((None, seq, w), lambda b, g: (b, 0, qb + g)),
                  pl.BlockSpec((None, seq, w), lambda b, g: (b, 0, kb + g)),
                  pl.BlockSpec((None, seq, w), lambda b, g: (b, 0, vb + g), pipeline_mode=pl.Buffered(1)),
                  pl.BlockSpec((n_heads, 2, TILE, TILE), lambda b, g: (g, 0, 0, 0),
                               pipeline_mode=pl.Buffered(1)),
                  pl.BlockSpec((SUBLANES, LANES), lambda b, g: (0, 0)),
                  pl.BlockSpec((1, LANES), lambda b, g: (0, 0))],
        out_specs=pl.BlockSpec((None, seq, w), lambda b, g: (b, 0, g)),
        out_shape=jax.ShapeDtypeStruct((bsz, seq, B_HEADS * LANES), BF16),
        scratch_shapes=[pltpu.VMEM((n_heads, seq // TILE, V_ROWS, TILE), BF16),
                        pltpu.VMEM((n_heads, SUBLANES, 2 * TILE), F32),
                        pltpu.VMEM((n_heads, V_ROWS, 2 * TILE), F32),
                        pltpu.VMEM((2, n_heads, 2, 2 * TILE, TILE), F32)],
        compiler_params=_cparams(2),
        name="diff_attn",
    )(proj, proj, proj, bias, lam_rows, gain.reshape(1, LANES))


def _stick_kernel(q_ref, k_ref, v_ref, tri_ref, o_ref, r_ref, acc_ref, *, n_pairs):
    TILE = STICK_TILE
    nt = q_ref.shape[0] // TILE
    nsub = TILE // BLOCK
    row = lax.broadcasted_iota(jnp.int32, (TILE, 2 * TILE), 0)
    col = lax.broadcasted_iota(jnp.int32, (TILE, 2 * TILE), 1) & (TILE - 1)
    strict = col < row

    def q_tile(qi, carry):
        q0 = pl.multiple_of(qi * TILE, TILE)
        r_ref[...] = jnp.zeros(r_ref.shape, F32)
        acc_ref[...] = jnp.zeros(acc_ref.shape, F32)

        def step(tiles):
            k0s = [pl.multiple_of(kt * TILE, TILE) for kt, _ in tiles]
            work = [(p, t) for p in range(n_pairs) for t in range(len(tiles))]
            tri = tri_ref[...]
            z, hi, lo, res = {}, {}, {}, {}
            for p, t in work:
                cols = slice(p * LANES, (p + 1) * LANES)
                z[p, t] = _dot_nt(q_ref[pl.ds(q0, TILE), cols],
                                  _block_diag_rows(k_ref[pl.ds(k0s[t], TILE), cols]))
            for p, t in work:
                sp = jnp.maximum(z[p, t], jnp.log2(1.0 + jnp.exp2(jnp.minimum(z[p, t], EXP2_MAX))))
                if tiles[t][1]:
                    sp = jnp.where(strict, sp, 0.0)
                hi[p, t] = sp.astype(BF16)
                lo[p, t] = (sp - hi[p, t].astype(F32)).astype(BF16)
            for p, t in work:
                for head in range(2):
                    for sub in range(nsub):
                        c0 = head * TILE + sub * BLOCK
                        res[p, t, head, sub] = jnp.dot(
                            jnp.concatenate([hi[p, t][:, c0:c0 + BLOCK], lo[p, t][:, c0:c0 + BLOCK]], axis=1),
                            tri, preferred_element_type=F32)
            weights, r_all = [], []
            for p in range(n_pairs):
                r = r_ref[p]
                r_heads = [r[:, :LANES], r[:, LANES:]]
                a_tiles = []
                for t in range(len(tiles)):
                    pieces = [None] * (2 * nsub)
                    for head in range(2):
                        for sub in reversed(range(nsub)):
                            pieces[head * nsub + sub] = res[p, t, head, sub][:, :BLOCK] + r_heads[head]
                            r_heads[head] = r_heads[head] + res[p, t, head, sub][:, BLOCK:]
                    a = jnp.exp2(z[p, t] - jnp.concatenate(pieces, axis=1))
                    if tiles[t][1]:
                        a = jnp.where(strict, a, 0.0)
                    a_tiles.append(a.astype(BF16))
                r_ref[p] = jnp.concatenate(r_heads, axis=1)
                r_all += r_heads
                weights.append(jnp.concatenate(a_tiles, axis=1))
            for p in range(n_pairs):
                cols = slice(p * LANES, (p + 1) * LANES)
                vbd = jnp.concatenate([_block_diag_rows(v_ref[pl.ds(k0, TILE), cols]) for k0 in k0s], axis=0)
                acc_ref[p] += jnp.dot(weights[p], vbd, preferred_element_type=F32)
            return jnp.min(functools.reduce(jnp.minimum, r_all))

        def back(state):
            kt, _ = state
            return kt - 1, step([(kt, False)])

        for n in range(1, STICK_MERGE):
            @pl.when(qi == n - 1)
            def _(n=n):
                step([(n - 1 - i, i == 0) for i in range(n)])

        @pl.when(qi >= STICK_MERGE - 1)
        def _():
            r_low = step([(qi - i, i == 0) for i in range(STICK_MERGE)])
            lax.while_loop(lambda state: (state[0] >= 0) & (state[1] < R_STOP), back, (qi - STICK_MERGE, r_low))
        for p in range(n_pairs):
            o_ref[pl.ds(q0, TILE), p * LANES:(p + 1) * LANES] = acc_ref[p].astype(o_ref.dtype)
        return carry

    lax.fori_loop(0, nt, q_tile, 0)


def _stick(proj, tri, *, bsz, seq, n_pairs_total, n_pairs=4):
    groups = n_pairs_total // n_pairs
    w = n_pairs * LANES
    return pl.pallas_call(
        functools.partial(_stick_kernel, n_pairs=n_pairs),
        grid=(bsz, groups),
        in_specs=[pl.BlockSpec((None, seq, w), lambda b, g: (b, 0, g)),
                  pl.BlockSpec((None, seq, w), lambda b, g: (b, 0, groups + g)),
                  pl.BlockSpec((None, seq, w), lambda b, g: (b, 0, 2 * groups + g)),
                  pl.BlockSpec((2 * BLOCK, 2 * BLOCK), lambda b, g: (0, 0))],
        out_specs=pl.BlockSpec((None, seq, w), lambda b, g: (b, 0, g)),
        out_shape=jax.ShapeDtypeStruct((bsz, seq, n_pairs_total * LANES), BF16),
        scratch_shapes=[pltpu.VMEM((n_pairs, STICK_TILE, 2 * LANES), F32),
                        pltpu.VMEM((n_pairs, STICK_TILE, LANES), F32)],
        compiler_params=_cparams(2),
        name="stick_breaking",
    )(proj, proj, proj, tri)


def _t5_bucket(dist):
    n = jnp.maximum(dist, 0)
    nf = jnp.maximum(n, 1).astype(F32)
    large = MAX_EXACT + (jnp.log(nf / MAX_EXACT) / math.log(MAX_DISTANCE / MAX_EXACT)
                         * (N_BUCKETS - MAX_EXACT)).astype(jnp.int32)
    large = jnp.minimum(large, N_BUCKETS - 1)
    return jnp.where(n < MAX_EXACT, n, large)


def _band_bias(rel_bias, blk):
    qi = jnp.arange(blk)[:, None]
    kj = jnp.arange(2 * blk)[None, :]
    dist = qi + blk - kj
    onehot = jax.nn.one_hot(_t5_bucket(dist), N_BUCKETS, dtype=F32)
    return jnp.einsum("qkn,nh->hqk", onehot, rel_bias, precision=lax.Precision.HIGHEST), dist


def _swa_bias(rel_bias_a):
    band, dist = _band_bias(rel_bias_a, BLOCK)
    in_window = (dist >= 0) & (dist < WINDOW)
    general = jnp.where(in_window[None], band, NEG_INF)
    first = jnp.where((in_window & (jnp.arange(2 * BLOCK)[None, :] >= BLOCK))[None], band, NEG_INF)
    both = jnp.stack([first, general])
    return both.reshape(2, A_HEADS // 2, 2, BLOCK, 2 * BLOCK).transpose(0, 1, 3, 2, 4).reshape(
        2, A_HEADS // 2, BLOCK, 4 * BLOCK)


def _diff_bias(rel_bias_b):
    band, dist = _band_bias(rel_bias_b, TILE)
    band = (band - rel_bias_b[N_BUCKETS - 1][:, None, None]) * LOG2E
    diag = jnp.where((dist[:, TILE:] >= 0)[None], band[:, :, TILE:], NEG_INF)
    return jnp.stack([diag, band[:, :, :TILE]], axis=1).swapaxes(-1, -2)


def _even_in_weight(w):
    aq, ak, av, bq, bk, bv = jnp.split(w, [512, 640, 768, 1280, 1792], axis=1)
    dup = lambda t: jnp.concatenate(
        [t[:, h * HEAD_DIM:(h + 1) * HEAD_DIM] for h in range(A_KV_HEADS) for _ in range(2)], axis=1)
    return jnp.concatenate([aq * SCALE, dup(ak), dup(av), bq * (SCALE * LOG2E), bk, bv], axis=1).astype(BF16)


def _suffix_matrix():
    j = jnp.arange(BLOCK)[:, None]
    s = jnp.arange(BLOCK)[None, :]
    half = jnp.concatenate([(j >= s).astype(BF16), jnp.ones((BLOCK, BLOCK), BF16)], axis=1)
    return jnp.concatenate([half, half], axis=0)


def kernel(x, rel_bias, norm_mix, norm_ffn, norm_final, w_in_even, w_out_even, sinks, lam_q1, lam_k1, lam_q2,
           lam_k2, diff_norm, w_in_odd, w_out_odd, ffn_up, ffn_conv, ffn_conv_b, ffn_down):
    bsz, seq, d = x.shape
    depth = norm_mix.shape[0]
    dff = ffn_down.shape[1]
    cw = 256
    n_pairs_c = d // LANES
    n = bsz * seq

    swa_bias = _swa_bias(rel_bias[:, :A_HEADS])
    diff_bias = _diff_bias(rel_bias[:, A_HEADS:])
    tri = _suffix_matrix()

    xf = x.reshape(n, d)
    for layer in range(depth):
        if layer % 2 == 0:
            e = layer // 2
            proj = _norm_proj(xf, norm_mix[layer], _even_in_weight(w_in_even[e]))
            proj = proj.reshape(bsz, seq, proj.shape[1])
            oa = _swa(proj, sinks[e], swa_bias, bsz=bsz, seq=seq)
            lambda_init = 0.8 - 0.6 * math.exp(-0.3 * layer)
            pad = lambda v: jnp.pad(v, (0, LANES - HEAD_DIM))
            lam_rows = jnp.stack([pad(lam_q1[e]), pad(lam_k1[e]), pad(lam_q2[e]), pad(lam_k2[e])]
                                 + [jnp.zeros((LANES,), F32)] * (SUBLANES - 4))
            ob = _diff(proj, diff_bias, lam_rows, diff_norm[e], bsz=bsz, seq=seq, lambda_init=lambda_init)
            w_out = w_out_even[e].astype(BF16)
            half = oa.shape[-1]
            mixed, w_outs = [oa.reshape(n, half), ob.reshape(n, half)], [w_out[:half], w_out[half:]]
        else:
            o = layer // 2
            w_in = jnp.concatenate([w_in_odd[o][:, :d] * (SCALE * LOG2E), w_in_odd[o][:, d:]],
                                   axis=1).astype(BF16)
            proj = _norm_proj(xf, norm_mix[layer], w_in).reshape(bsz, seq, 3 * d)
            oc = _stick(proj, tri, bsz=bsz, seq=seq, n_pairs_total=n_pairs_c)
            mixed, w_outs = [oc.reshape(n, d)], [w_out_odd[o].astype(BF16)]
        xf = _mix_ffn(xf, mixed, w_outs, norm_ffn[layer], ffn_up[layer].astype(BF16), ffn_conv[layer],
                      ffn_conv_b[layer].reshape(1, 2 * dff), ffn_down[layer].astype(BF16), norm_final,
                      seq=seq, cw=cw, final_norm=(layer == depth - 1))
    return xf.reshape(bsz, seq, d)
```

```python
import functools
import math

import jax
import jax.numpy as jnp
from jax import lax
from jax.experimental import pallas as pl
from jax.experimental.pallas import tpu as pltpu

LANES = 128
SUBLANES = 8
VMEM_LIMIT_BYTES = 56 * 1024 * 1024

HEAD_DIM = 64
BLOCK = 128
TILE = 256
STICK_TILE = 128
STICK_MERGE = 3
SWA_UNROLL = 2
V_ROWS = LANES + 2 * SUBLANES
WINDOW = 128
A_HEADS, A_KV_HEADS = 8, 2
B_HEADS = 4
N_BUCKETS, MAX_EXACT, MAX_DISTANCE = 32, 16, 128
CONV_W = 3
EPS = 1e-6
SCALE = HEAD_DIM ** -0.5

LOG2E = math.log2(math.e)
EXP2_MAX = 126.0
R_STOP = 152.0

F32 = jnp.float32
BF16 = jnp.bfloat16
NEG_INF = float("-inf")


def _cparams(n_axes):
    return pltpu.CompilerParams(
        dimension_semantics=("arbitrary",) * n_axes, vmem_limit_bytes=VMEM_LIMIT_BYTES)


def _rms(x, g):
    ms = jnp.mean(x * x, axis=-1, keepdims=True)
    return x * lax.rsqrt(ms + EPS) * g


def _lane_halves(rows):
    lane = lax.broadcasted_iota(jnp.int32, (rows, LANES), 1)
    return lane < HEAD_DIM


def _block_diag_rows(t):
    lo = _lane_halves(t.shape[0])
    zero = jnp.zeros_like(t)
    return jnp.concatenate([jnp.where(lo, t, zero), jnp.where(lo, zero, t)], axis=0)


def _dot_nt(a, b):
    return lax.dot_general(a, b, (((1,), (1,)), ((), ())), preferred_element_type=F32)


def _norm_proj_kernel(x_ref, g_ref, w_ref, o_ref, *, col_chunk):
    h = _rms(x_ref[...], g_ref[...]).astype(BF16)
    for n0 in range(0, o_ref.shape[1], col_chunk):
        o_ref[:, n0:n0 + col_chunk] = jnp.dot(
            h, w_ref[:, n0:n0 + col_chunk], preferred_element_type=F32).astype(o_ref.dtype)


def _norm_proj(x, g, w, *, tm=512, col_chunk=512):
    n, d = x.shape
    nout = w.shape[1]
    assert n % tm == 0 and nout % col_chunk == 0, (n, nout)
    return pl.pallas_call(
        functools.partial(_norm_proj_kernel, col_chunk=col_chunk),
        grid=(n // tm,),
        in_specs=[pl.BlockSpec((tm, d), lambda i: (i, 0)),
                  pl.BlockSpec((1, d), lambda i: (0, 0)),
                  pl.BlockSpec((d, nout), lambda i: (0, 0))],
        out_specs=pl.BlockSpec((tm, nout), lambda i: (i, 0)),
        out_shape=jax.ShapeDtypeStruct((n, nout), BF16),
        compiler_params=_cparams(1),
        name="norm_proj",
    )(x, g.reshape(1, d), w)


def _mix_ffn_kernel(*refs, n_mix, tiles_per_seq, n_chunks, cw, final_norm):
    x_ref = refs[0]
    o_refs, wo_refs = refs[1:1 + n_mix], refs[1 + n_mix:1 + 2 * n_mix]
    g_ref, wup_ref, wc_ref, bc_ref, wdn_ref, gf_ref, out_ref, tail_ref, act_ref = refs[1 + 2 * n_mix:]
    tm = x_ref.shape[0]
    dff = wdn_ref.shape[0]

    @pl.when((pl.program_id(0) % tiles_per_seq) == 0)
    def _():
        tail_ref[...] = jnp.zeros(tail_ref.shape, F32)

    x = x_ref[...]
    for o_ref, wo_ref in zip(o_refs, wo_refs):
        x = x + jnp.dot(o_ref[...], wo_ref[...], preferred_element_type=F32)
    h = _rms(x, g_ref[...]).astype(BF16)
    for c in range(n_chunks):
        conv = []
        for part in range(2):
            cols = slice(part * dff + c * cw, part * dff + (c + 1) * cw)
            u = jnp.dot(h, wup_ref[:, cols], preferred_element_type=F32)
            ext = jnp.concatenate([tail_ref[2 * c + part], u], axis=0)
            tail_ref[2 * c + part] = u[tm - SUBLANES:, :]
            u1 = pltpu.roll(ext, 1, axis=0)[SUBLANES:]
            u2 = pltpu.roll(ext, 2, axis=0)[SUBLANES:]
            wc = wc_ref[:, cols]
            conv.append((u2 * wc[0:1] + u1 * wc[1:2] + u * wc[2:3]) + bc_ref[:, cols])
        gate, val = conv
        act_ref[:, c * cw:(c + 1) * cw] = (gate * jax.nn.sigmoid(gate) * val).astype(BF16)
    y = x + jnp.dot(act_ref[...], wdn_ref[...], preferred_element_type=F32)
    if final_norm:
        y = _rms(y, gf_ref[...])
    out_ref[...] = y


def _mix_ffn(x, outs, w_outs, g, wup, wc, bc, wdn, g_final, *, seq, cw, tm=512, final_norm=False):
    n, d = x.shape
    dff = wdn.shape[0]
    n_chunks = dff // cw
    assert seq % tm == 0 and n % seq == 0 and dff % cw == 0 and wc.shape[0] == CONV_W, (n, seq, dff)
    const = lambda i: (0, 0)
    rows = lambda i: (i, 0)
    resident = lambda a: pl.BlockSpec(a.shape, const, pipeline_mode=pl.Buffered(1))
    return pl.pallas_call(
        functools.partial(_mix_ffn_kernel, n_mix=len(outs), tiles_per_seq=seq // tm, n_chunks=n_chunks,
                          cw=cw, final_norm=final_norm),
        grid=(n // tm,),
        in_specs=([pl.BlockSpec((tm, d), rows)]
                  + [pl.BlockSpec((tm, o.shape[1]), rows) for o in outs]
                  + [resident(w) for w in w_outs]
                  + [pl.BlockSpec((1, d), const), resident(wup), pl.BlockSpec(wc.shape, const),
                     pl.BlockSpec(bc.shape, const), resident(wdn), pl.BlockSpec((1, d), const)]),
        out_specs=pl.BlockSpec((tm, d), rows),
        out_shape=jax.ShapeDtypeStruct((n, d), F32),
        scratch_shapes=[pltpu.VMEM((2 * n_chunks, SUBLANES, cw), F32),
                        pltpu.VMEM((tm, dff), BF16)],
        compiler_params=_cparams(1),
        name="mix_ffn",
    )(x, *outs, *w_outs, g.reshape(1, d), wup, wc, bc, wdn, g_final.reshape(1, d))


def _swa_kernel(sink_ref, q_ref, k_ref, v_ref, bias_ref, o_ref, *, n_pairs, pairs_per_kv):
    nb = q_ref.shape[0] // BLOCK

    def band(ref, j, kv_cols):
        cur = pl.multiple_of(j * BLOCK, BLOCK)
        prev = pl.multiple_of(jnp.maximum(j - 1, 0) * BLOCK, BLOCK)
        return jnp.concatenate([ref[pl.ds(prev, BLOCK), kv_cols], ref[pl.ds(cur, BLOCK), kv_cols]], axis=0)

    def body(i, carry):
        chains = [(i * SWA_UNROLL + u, c) for u in range(SWA_UNROLL) for c in range(n_pairs)]
        kv = lambda c: slice((c // pairs_per_kv) * LANES, (c // pairs_per_kv + 1) * LANES)
        scores = []
        for j, c in chains:
            q2 = q_ref[pl.ds(pl.multiple_of(j * BLOCK, BLOCK), BLOCK), c * LANES:(c + 1) * LANES]
            scores.append(_dot_nt(q2, _block_diag_rows(band(k_ref, j, kv(c)))) + bias_ref[jnp.minimum(j, 1), c])
        weights = []
        for (j, c), s in zip(chains, scores):
            w_halves = []
            for i2 in range(2):
                half = s[:, i2 * 2 * BLOCK:(i2 + 1) * 2 * BLOCK]
                sink = sink_ref[2 * c + i2]
                m = jnp.maximum(jnp.max(half, axis=-1, keepdims=True), sink)
                p = jnp.exp(half - m)
                w_halves.append(p / (jnp.sum(p, axis=-1, keepdims=True) + jnp.exp(sink - m)))
            weights.append(jnp.concatenate(w_halves, axis=1).astype(BF16))
        for (j, c), w in zip(chains, weights):
            o = jnp.dot(w, _block_diag_rows(band(v_ref, j, kv(c))), preferred_element_type=F32)
            o_ref[pl.ds(pl.multiple_of(j * BLOCK, BLOCK), BLOCK), c * LANES:(c + 1) * LANES] = (
                o.astype(o_ref.dtype))
        return carry

    lax.fori_loop(0, nb // SWA_UNROLL, body, 0)


def _swa(proj, sinks, bias, *, bsz, seq):
    n_pairs = A_HEADS // 2
    pairs_per_kv = n_pairs // A_KV_HEADS
    q_w, kv_w = n_pairs * LANES, A_KV_HEADS * LANES
    assert seq % (BLOCK * SWA_UNROLL) == 0 and WINDOW == BLOCK, seq
    return pl.pallas_call(
        functools.partial(_swa_kernel, n_pairs=n_pairs, pairs_per_kv=pairs_per_kv),
        grid=(bsz,),
        in_specs=[pl.BlockSpec(memory_space=pltpu.SMEM),
                  pl.BlockSpec((None, seq, q_w), lambda b: (b, 0, 0)),
                  pl.BlockSpec((None, seq, kv_w), lambda b: (b, 0, q_w // kv_w)),
                  pl.BlockSpec((None, seq, kv_w), lambda b: (b, 0, q_w // kv_w + 1)),
                  pl.BlockSpec(bias.shape, lambda b: (0, 0, 0, 0))],
        out_specs=pl.BlockSpec((None, seq, q_w), lambda b: (b, 0, 0)),
        out_shape=jax.ShapeDtypeStruct((bsz, seq, q_w), BF16),
        compiler_params=_cparams(1),
        name="swa_sink",
    )(sinks, proj, proj, proj, bias)


def _diff_kernel(q_ref, k_ref, v_ref, bias_ref, lam_ref, gain_ref, o_ref, vt_ref, m_ref, acc_ref, s_ref, *,
                 lambda_init, n_heads):
    nt = q_ref.shape[0] // TILE
    lp = lam_ref[...]
    lam = (jnp.exp(jnp.sum(lp[0:1] * lp[1:2], axis=-1, keepdims=True))
           - jnp.exp(jnp.sum(lp[2:3] * lp[3:4], axis=-1, keepdims=True)) + lambda_init)

    def transpose_v(kt, carry):
        k0 = pl.multiple_of(kt * TILE, TILE)
        for h in range(n_heads):
            vt = v_ref[pl.ds(k0, TILE), h * LANES:(h + 1) * LANES].astype(F32).T
            vt_ref[h, kt] = jnp.concatenate([vt.astype(BF16), jnp.ones((V_ROWS - LANES, TILE), BF16)], axis=0)
        return carry

    lax.fori_loop(0, nt, transpose_v, 0)

    def q_tile(qi, carry):
        q0 = pl.multiple_of(qi * TILE, TILE)
        m_ref[...] = jnp.full(m_ref.shape, NEG_INF, F32)
        acc_ref[...] = jnp.zeros(acc_ref.shape, F32)

        def raw_scores(h, kt):
            cols = slice(h * LANES, (h + 1) * LANES)
            kbd = _block_diag_rows(k_ref[pl.ds(pl.multiple_of(kt * TILE, TILE), TILE), cols])
            return _dot_nt(kbd, q_ref[pl.ds(q0, TILE), cols])

        def update(get_scores, tiles):
            scores = {}
            for h in range(n_heads):
                for t, (_, bias_idx) in enumerate(tiles):
                    st = get_scores(h, t)
                    halves = [st[:TILE], st[TILE:]]
                    if bias_idx is not None:
                        halves = [x + bias_ref[h, bias_idx] for x in halves]
                    scores[h, t] = halves
            probs, alphas = [], []
            for h in range(n_heads):
                m_old = m_ref[h]
                m_blk = jnp.concatenate(
                    [functools.reduce(jnp.maximum, [jnp.max(scores[h, t][i], axis=0, keepdims=True)
                                                    for t in range(len(tiles))]) for i in range(2)], axis=1)
                m_new = jnp.maximum(m_old, m_blk)
                alphas.append(jnp.exp2(m_old - m_new)[0:1])
                m_ref[h] = m_new
                probs.append(jnp.concatenate(
                    [jnp.concatenate([jnp.exp2(scores[h, t][i] - m_new[0:1, i * TILE:(i + 1) * TILE]
                                               ).astype(BF16) for i in range(2)], axis=1)
                     for t in range(len(tiles))], axis=0))
            for h in range(n_heads):
                vt = jnp.concatenate([vt_ref[h, kt] for kt, _ in tiles], axis=1)
                pv = jnp.dot(vt, probs[h], preferred_element_type=F32)
                acc_ref[h] = acc_ref[h] * alphas[h] + pv

        def step(tiles):
            update(lambda h, t: raw_scores(h, tiles[t][0]), tiles)

        def fill(slot, kt):
            for h in range(n_heads):
                for t in range(2):
                    s_ref[slot, h, t] = raw_scores(h, kt + t)

        n_far = jnp.maximum(qi - 1, 0)
        first = n_far % 2
        n_pairs = n_far // 2

        @pl.when(first == 1)
        def _():
            step([(0, None)])

        @pl.when(qi == 0)
        def _():
            step([(0, 0)])

        @pl.when(qi >= 1)
        def _():
            from_slot = lambda slot: (lambda h, t: s_ref[slot, h, t])
            last = [(qi - 1, 1), (qi, 0)]
            fill(0, first)

            def two_pairs(j, c2):
                kt = first + 4 * j
                fill(1, kt + 2)
                update(from_slot(0), [(kt, None), (kt + 1, None)])
                fill(0, kt + 4)
                update(from_slot(1), [(kt + 2, None), (kt + 3, None)])
                return c2

            lax.fori_loop(0, n_pairs // 2, two_pairs, 0)

            @pl.when(n_pairs % 2 == 1)
            def _():
                kt = qi - 3
                fill(1, kt + 2)
                update(from_slot(0), [(kt, None), (kt + 1, None)])
                update(from_slot(1), last)

            @pl.when(n_pairs % 2 == 0)
            def _():
                update(from_slot(0), last)

        for h in range(n_heads):
            acc = acc_ref[h]
            o1 = acc[:LANES, :TILE] / acc[LANES:LANES + 1, :TILE]
            o2 = acc[:LANES, TILE:] / acc[LANES:LANES + 1, TILE:]
            o = _rms((o1 - lam * o2).T, gain_ref[...]) * (1.0 - lambda_init)
            o_ref[pl.ds(q0, TILE), h * LANES:(h + 1) * LANES] = o.astype(o_ref.dtype)
        return carry

    lax.fori_loop(0, nt, q_tile, 0)


def _diff(proj, bias, lam_rows, gain, *, bsz, seq, lambda_init, n_heads=4):
    assert seq % TILE == 0 and B_HEADS % n_heads == 0, seq
    q_col0 = A_HEADS // 2 + 2 * A_KV_HEADS
    groups = B_HEADS // n_heads
    w = n_heads * LANES
    qb, kb, vb = (c // n_heads for c in (q_col0, q_col0 + B_HEADS, q_col0 + 2 * B_HEADS))
    return pl.pallas_call(
        functools.partial(_diff_kernel, lambda_init=lambda_init, n_heads=n_heads),
        grid=(bsz, groups),
        in_specs=[pl.BlockSpec((None, seq, w), lambda b, g: (b, 0, qb + g)),
                  pl.BlockSpec((None, seq, w), lambda b, g: (b, 0, kb + g)),
                  pl.BlockSpec((None, seq, w), lambda b, g: (b, 0, vb + g), pipeline_mode=pl.Buffered(1)),
                  pl.BlockSpec((n_heads, 2, TILE, TILE), lambda b, g: (g, 0, 0, 0),
                               pipeline_mode=pl.Buffered(1)),
                  pl.BlockSpec((SUBLANES, LANES), lambda b, g: (0, 0)),
                  pl.BlockSpec((1, LANES), lambda b, g: (0, 0))],
        out_specs=pl.BlockSpec((None, seq, w), lambda b, g: (b, 0, g)),
        out_shape=jax.ShapeDtypeStruct((bsz, seq, B_HEADS * LANES), BF16),
        scratch_shapes=[pltpu.VMEM((n_heads, seq // TILE, V_ROWS, TILE), BF16),
                        pltpu.VMEM((n_heads, SUBLANES, 2 * TILE), F32),
                        pltpu.VMEM((n_heads, V_ROWS, 2 * TILE), F32),
                        pltpu.VMEM((2, n_heads, 2, 2 * TILE, TILE), F32)],
        compiler_params=_cparams(2),
        name="diff_attn",
    )(proj, proj, proj, bias, lam_rows, gain.reshape(1, LANES))


def _stick_kernel(q_ref, k_ref, v_ref, tri_ref, o_ref, r_ref, acc_ref, *, n_pairs):
    TILE = STICK_TILE
    nt = q_ref.shape[0] // TILE
    nsub = TILE // BLOCK
    row = lax.broadcasted_iota(jnp.int32, (TILE, 2 * TILE), 0)
    col = lax.broadcasted_iota(jnp.int32, (TILE, 2 * TILE), 1) & (TILE - 1)
    strict = col < row

    def q_tile(qi, carry):
        q0 = pl.multiple_of(qi * TILE, TILE)
        r_ref[...] = jnp.zeros(r_ref.shape, F32)
        acc_ref[...] = jnp.zeros(acc_ref.shape, F32)

        def step(tiles):
            k0s = [pl.multiple_of(kt * TILE, TILE) for kt, _ in tiles]
            work = [(p, t) for p in range(n_pairs) for t in range(len(tiles))]
            tri = tri_ref[...]
            z, hi, lo, res = {}, {}, {}, {}
            for p, t in work:
                cols = slice(p * LANES, (p + 1) * LANES)
                z[p, t] = _dot_nt(q_ref[pl.ds(q0, TILE), cols],
                                  _block_diag_rows(k_ref[pl.ds(k0s[t], TILE), cols]))
            for p, t in work:
                sp = jnp.maximum(z[p, t], jnp.log2(1.0 + jnp.exp2(jnp.minimum(z[p, t], EXP2_MAX))))
                if tiles[t][1]:
                    sp = jnp.where(strict, sp, 0.0)
                hi[p, t] = sp.astype(BF16)
                lo[p, t] = (sp - hi[p, t].astype(F32)).astype(BF16)
            for t in range(len(tiles)):
                blocks = [(p, head, sub) for p in range(n_pairs) for head in range(2) for sub in range(nsub)]
                lhs = jnp.concatenate(
                    [jnp.concatenate([hi[p, t][:, head * TILE + sub * BLOCK:head * TILE + (sub + 1) * BLOCK],
                                      lo[p, t][:, head * TILE + sub * BLOCK:head * TILE + (sub + 1) * BLOCK]],
                                     axis=1) for p, head, sub in blocks], axis=0)
                sums = jnp.dot(lhs, tri, preferred_element_type=F32)
                for i, (p, head, sub) in enumerate(blocks):
                    res[p, t, head, sub] = sums[i * TILE:(i + 1) * TILE]
            weights, r_all = [], []
            for p in range(n_pairs):
                r = r_ref[p]
                r_heads = [r[:, :LANES], r[:, LANES:]]
                a_tiles = []
                for t in range(len(tiles)):
                    pieces = [None] * (2 * nsub)
                    for head in range(2):
                        for sub in reversed(range(nsub)):
                            pieces[head * nsub + sub] = res[p, t, head, sub][:, :BLOCK] + r_heads[head]
                            r_heads[head] = r_heads[head] + res[p, t, head, sub][:, BLOCK:]
                    a = jnp.exp2(z[p, t] - jnp.concatenate(pieces, axis=1))
                    if tiles[t][1]:
                        a = jnp.where(strict, a, 0.0)
                    a_tiles.append(a.astype(BF16))
                r_ref[p] = jnp.concatenate(r_heads, axis=1)
                r_all += r_heads
                weights.append(jnp.concatenate(a_tiles, axis=1))
            for p in range(n_pairs):
                cols = slice(p * LANES, (p + 1) * LANES)
                vbd = jnp.concatenate([_block_diag_rows(v_ref[pl.ds(k0, TILE), cols]) for k0 in k0s], axis=0)
                acc_ref[p] += jnp.dot(weights[p], vbd, preferred_element_type=F32)
            return jnp.min(functools.reduce(jnp.minimum, r_all))

        def back(state):
            kt, _ = state
            return kt - 1, step([(kt, False)])

        for n in range(1, STICK_MERGE):
            @pl.when(qi == n - 1)
            def _(n=n):
                step([(n - 1 - i, i == 0) for i in range(n)])

        @pl.when(qi >= STICK_MERGE - 1)
        def _():
            r_low = step([(qi - i, i == 0) for i in range(STICK_MERGE)])
            lax.while_loop(lambda state: (state[0] >= 0) & (state[1] < R_STOP), back, (qi - STICK_MERGE, r_low))
        for p in range(n_pairs):
            o_ref[pl.ds(q0, TILE), p * LANES:(p + 1) * LANES] = acc_ref[p].astype(o_ref.dtype)
        return carry

    lax.fori_loop(0, nt, q_tile, 0)


def _stick(proj, tri, *, bsz, seq, n_pairs_total, n_pairs=4):
    assert seq % STICK_TILE == 0 and n_pairs_total % n_pairs == 0, (seq, n_pairs_total)
    groups = n_pairs_total // n_pairs
    w = n_pairs * LANES
    return pl.pallas_call(
        functools.partial(_stick_kernel, n_pairs=n_pairs),
        grid=(bsz, groups),
        in_specs=[pl.BlockSpec((None, seq, w), lambda b, g: (b, 0, g)),
                  pl.BlockSpec((None, seq, w), lambda b, g: (b, 0, groups + g)),
                  pl.BlockSpec((None, seq, w), lambda b, g: (b, 0, 2 * groups + g)),
                  pl.BlockSpec((2 * BLOCK, 2 * BLOCK), lambda b, g: (0, 0))],
        out_specs=pl.BlockSpec((None, seq, w), lambda b, g: (b, 0, g)),
        out_shape=jax.ShapeDtypeStruct((bsz, seq, n_pairs_total * LANES), BF16),
        scratch_shapes=[pltpu.VMEM((n_pairs, STICK_TILE, 2 * LANES), F32),
                        pltpu.VMEM((n_pairs, STICK_TILE, LANES), F32)],
        compiler_params=_cparams(2),
        name="stick_breaking",
    )(proj, proj, proj, tri)


def _t5_bucket(dist):
    n = jnp.maximum(dist, 0)
    nf = jnp.maximum(n, 1).astype(F32)
    large = MAX_EXACT + (jnp.log(nf / MAX_EXACT) / math.log(MAX_DISTANCE / MAX_EXACT)
                         * (N_BUCKETS - MAX_EXACT)).astype(jnp.int32)
    large = jnp.minimum(large, N_BUCKETS - 1)
    return jnp.where(n < MAX_EXACT, n, large)


def _band_bias(rel_bias, blk):
    qi = jnp.arange(blk)[:, None]
    kj = jnp.arange(2 * blk)[None, :]
    dist = qi + blk - kj
    onehot = jax.nn.one_hot(_t5_bucket(dist), N_BUCKETS, dtype=F32)
    return jnp.einsum("qkn,nh->hqk", onehot, rel_bias, precision=lax.Precision.HIGHEST), dist


def _swa_bias(rel_bias_a):
    band, dist = _band_bias(rel_bias_a, BLOCK)
    in_window = (dist >= 0) & (dist < WINDOW)
    general = jnp.where(in_window[None], band, NEG_INF)
    first = jnp.where((in_window & (jnp.arange(2 * BLOCK)[None, :] >= BLOCK))[None], band, NEG_INF)
    both = jnp.stack([first, general])
    return both.reshape(2, A_HEADS // 2, 2, BLOCK, 2 * BLOCK).transpose(0, 1, 3, 2, 4).reshape(
        2, A_HEADS // 2, BLOCK, 4 * BLOCK)


def _diff_bias(rel_bias_b):
    band, dist = _band_bias(rel_bias_b, TILE)
    band = (band - rel_bias_b[N_BUCKETS - 1][:, None, None]) * LOG2E
    diag = jnp.where((dist[:, TILE:] >= 0)[None], band[:, :, TILE:], NEG_INF)
    return jnp.stack([diag, band[:, :, :TILE]], axis=1).swapaxes(-1, -2)


def _even_in_weight(w):
    a_q, a_kv, b_qk = A_HEADS * HEAD_DIM, A_KV_HEADS * HEAD_DIM, B_HEADS * 2 * HEAD_DIM
    splits = [a_q, a_q + a_kv, a_q + 2 * a_kv, a_q + 2 * a_kv + b_qk, a_q + 2 * a_kv + 2 * b_qk]
    aq, ak, av, bq, bk, bv = jnp.split(w, splits, axis=1)
    dup = lambda t: jnp.concatenate(
        [t[:, h * HEAD_DIM:(h + 1) * HEAD_DIM] for h in range(A_KV_HEADS) for _ in range(2)], axis=1)
    return jnp.concatenate([aq * SCALE, dup(ak), dup(av), bq * (SCALE * LOG2E), bk, bv], axis=1).astype(BF16)


def _suffix_matrix():
    j = jnp.arange(BLOCK)[:, None]
    s = jnp.arange(BLOCK)[None, :]
    half = jnp.concatenate([(j >= s).astype(BF16), jnp.ones((BLOCK, BLOCK), BF16)], axis=1)
    return jnp.concatenate([half, half], axis=0)


def kernel(x, rel_bias, norm_mix, norm_ffn, norm_final, w_in_even, w_out_even, sinks, lam_q1, lam_k1, lam_q2,
           lam_k2, diff_norm, w_in_odd, w_out_odd, ffn_up, ffn_conv, ffn_conv_b, ffn_down):
    bsz, seq, d = x.shape
    depth = norm_mix.shape[0]
    dff = ffn_down.shape[1]
    cw = 256
    n_pairs_c = d // LANES
    n = bsz * seq

    swa_bias = _swa_bias(rel_bias[:, :A_HEADS])
    diff_bias = _diff_bias(rel_bias[:, A_HEADS:])
    tri = _suffix_matrix()

    xf = x.reshape(n, d)
    for layer in range(depth):
        if layer % 2 == 0:
            e = layer // 2
            proj = _norm_proj(xf, norm_mix[layer], _even_in_weight(w_in_even[e]))
            proj = proj.reshape(bsz, seq, proj.shape[1])
            oa = _swa(proj, sinks[e], swa_bias, bsz=bsz, seq=seq)
            lambda_init = 0.8 - 0.6 * math.exp(-0.3 * layer)
            pad = lambda v: jnp.pad(v, (0, LANES - HEAD_DIM))
            lam_rows = jnp.stack([pad(lam_q1[e]), pad(lam_k1[e]), pad(lam_q2[e]), pad(lam_k2[e])]
                                 + [jnp.zeros((LANES,), F32)] * (SUBLANES - 4))
            ob = _diff(proj, diff_bias, lam_rows, diff_norm[e], bsz=bsz, seq=seq, lambda_init=lambda_init)
            w_out = w_out_even[e].astype(BF16)
            half = oa.shape[-1]
            mixed, w_outs = [oa.reshape(n, half), ob.reshape(n, half)], [w_out[:half], w_out[half:]]
        else:
            o = layer // 2
            w_in = jnp.concatenate([w_in_odd[o][:, :d] * (SCALE * LOG2E), w_in_odd[o][:, d:]],
                                   axis=1).astype(BF16)
            proj = _norm_proj(xf, norm_mix[layer], w_in).reshape(bsz, seq, 3 * d)
            oc = _stick(proj, tri, bsz=bsz, seq=seq, n_pairs_total=n_pairs_c)
            mixed, w_outs = [oc.reshape(n, d)], [w_out_odd[o].astype(BF16)]
        xf = _mix_ffn(xf, mixed, w_outs, norm_ffn[layer], ffn_up[layer].astype(BF16), ffn_conv[layer],
                      ffn_conv_b[layer].reshape(1, 2 * dff), ffn_down[layer].astype(BF16), norm_final,
                      seq=seq, cw=cw, final_norm=(layer == depth - 1))
    return xf.reshape(bsz, seq, d)
```

```python
import functools
import math

import jax
import jax.numpy as jnp
from jax import lax
from jax.experimental import pallas as pl
from jax.experimental.pallas import tpu as pltpu

LANES = 128
SUBLANES = 8
VMEM_LIMIT_BYTES = 56 * 1024 * 1024

HEAD_DIM = 64
BLOCK = 128
TILE = 256
STICK_TILE = 128
STICK_MERGE = 3
SWA_UNROLL = 2
V_ROWS = LANES + 2 * SUBLANES
WINDOW = 128
A_HEADS, A_KV_HEADS = 8, 2
B_HEADS = 4
N_BUCKETS, MAX_EXACT, MAX_DISTANCE = 32, 16, 128
CONV_W = 3
EPS = 1e-6
SCALE = HEAD_DIM ** -0.5

LOG2E = math.log2(math.e)
EXP2_MAX = 126.0
R_STOP = 152.0

F32 = jnp.float32
BF16 = jnp.bfloat16
NEG_INF = float("-inf")


def _cparams(n_axes):
    return pltpu.CompilerParams(
        dimension_semantics=("arbitrary",) * n_axes, vmem_limit_bytes=VMEM_LIMIT_BYTES)


def _rms(x, g):
    ms = jnp.mean(x * x, axis=-1, keepdims=True)
    return x * lax.rsqrt(ms + EPS) * g


def _lane_halves(rows):
    lane = lax.broadcasted_iota(jnp.int32, (rows, LANES), 1)
    return lane < HEAD_DIM


def _block_diag_rows(t):
    lo = _lane_halves(t.shape[0])
    zero = jnp.zeros_like(t)
    return jnp.concatenate([jnp.where(lo, t, zero), jnp.where(lo, zero, t)], axis=0)


def _dot_nt(a, b):
    return lax.dot_general(a, b, (((1,), (1,)), ((), ())), preferred_element_type=F32)


def _norm_proj_kernel(x_ref, g_ref, w_ref, o_ref, *, col_chunk):
    h = _rms(x_ref[...], g_ref[...]).astype(BF16)
    for n0 in range(0, o_ref.shape[1], col_chunk):
        o_ref[:, n0:n0 + col_chunk] = jnp.dot(
            h, w_ref[:, n0:n0 + col_chunk], preferred_element_type=F32).astype(o_ref.dtype)


def _norm_proj(x, g, w, *, tm=1024, col_chunk=512):
    n, d = x.shape
    nout = w.shape[1]
    assert n % tm == 0 and nout % col_chunk == 0, (n, nout)
    return pl.pallas_call(
        functools.partial(_norm_proj_kernel, col_chunk=col_chunk),
        grid=(n // tm,),
        in_specs=[pl.BlockSpec((tm, d), lambda i: (i, 0)),
                  pl.BlockSpec((1, d), lambda i: (0, 0)),
                  pl.BlockSpec((d, nout), lambda i: (0, 0), pipeline_mode=pl.Buffered(1))],
        out_specs=pl.BlockSpec((tm, nout), lambda i: (i, 0)),
        out_shape=jax.ShapeDtypeStruct((n, nout), BF16),
        compiler_params=_cparams(1),
        name="norm_proj",
    )(x, g.reshape(1, d), w)


def _mix_ffn_kernel(*refs, n_mix, tiles_per_seq, n_chunks, cw, final_norm):
    x_ref = refs[0]
    o_refs, wo_refs = refs[1:1 + n_mix], refs[1 + n_mix:1 + 2 * n_mix]
    g_ref, wup_ref, wc_ref, bc_ref, wdn_ref, gf_ref, out_ref, tail_ref, act_ref = refs[1 + 2 * n_mix:]
    tm = x_ref.shape[0]
    dff = wdn_ref.shape[0]

    @pl.when((pl.program_id(0) % tiles_per_seq) == 0)
    def _():
        tail_ref[...] = jnp.zeros(tail_ref.shape, F32)

    x = x_ref[...]
    for o_ref, wo_ref in zip(o_refs, wo_refs):
        x = x + jnp.dot(o_ref[...], wo_ref[...], preferred_element_type=F32)
    h = _rms(x, g_ref[...]).astype(BF16)
    for c in range(n_chunks):
        conv = []
        for part in range(2):
            cols = slice(part * dff + c * cw, part * dff + (c + 1) * cw)
            u = jnp.dot(h, wup_ref[:, cols], preferred_element_type=F32)
            ext = jnp.concatenate([tail_ref[2 * c + part], u], axis=0)
            tail_ref[2 * c + part] = u[tm - SUBLANES:, :]
            u1 = pltpu.roll(ext, 1, axis=0)[SUBLANES:]
            u2 = pltpu.roll(ext, 2, axis=0)[SUBLANES:]
            wc = wc_ref[:, cols]
            conv.append((u2 * wc[0:1] + u1 * wc[1:2] + u * wc[2:3]) + bc_ref[:, cols])
        gate, val = conv
        act_ref[:, c * cw:(c + 1) * cw] = (gate * jax.nn.sigmoid(gate) * val).astype(BF16)
    y = x + jnp.dot(act_ref[...], wdn_ref[...], preferred_element_type=F32)
    if final_norm:
        y = _rms(y, gf_ref[...])
    out_ref[...] = y


def _mix_ffn(x, outs, w_outs, g, wup, wc, bc, wdn, g_final, *, seq, cw, tm=1024, final_norm=False):
    n, d = x.shape
    dff = wdn.shape[0]
    n_chunks = dff // cw
    assert seq % tm == 0 and n % seq == 0 and dff % cw == 0 and wc.shape[0] == CONV_W, (n, seq, dff)
    const = lambda i: (0, 0)
    rows = lambda i: (i, 0)
    resident = lambda a: pl.BlockSpec(a.shape, const, pipeline_mode=pl.Buffered(1))
    return pl.pallas_call(
        functools.partial(_mix_ffn_kernel, n_mix=len(outs), tiles_per_seq=seq // tm, n_chunks=n_chunks,
                          cw=cw, final_norm=final_norm),
        grid=(n // tm,),
        in_specs=([pl.BlockSpec((tm, d), rows)]
                  + [pl.BlockSpec((tm, o.shape[1]), rows) for o in outs]
                  + [resident(w) for w in w_outs]
                  + [pl.BlockSpec((1, d), const), resident(wup), pl.BlockSpec(wc.shape, const),
                     pl.BlockSpec(bc.shape, const), resident(wdn), pl.BlockSpec((1, d), const)]),
        out_specs=pl.BlockSpec((tm, d), rows),
        out_shape=jax.ShapeDtypeStruct((n, d), F32),
        scratch_shapes=[pltpu.VMEM((2 * n_chunks, SUBLANES, cw), F32),
                        pltpu.VMEM((tm, dff), BF16)],
        compiler_params=_cparams(1),
        name="mix_ffn",
    )(x, *outs, *w_outs, g.reshape(1, d), wup, wc, bc, wdn, g_final.reshape(1, d))


def _swa_kernel(sink_ref, q_ref, k_ref, v_ref, bias_ref, o_ref, *, n_pairs, pairs_per_kv):
    nb = q_ref.shape[0] // BLOCK

    def band(ref, j, kv_cols):
        cur = pl.multiple_of(j * BLOCK, BLOCK)
        prev = pl.multiple_of(jnp.maximum(j - 1, 0) * BLOCK, BLOCK)
        return jnp.concatenate([ref[pl.ds(prev, BLOCK), kv_cols], ref[pl.ds(cur, BLOCK), kv_cols]], axis=0)

    def body(i, carry):
        chains = [(i * SWA_UNROLL + u, c) for u in range(SWA_UNROLL) for c in range(n_pairs)]
        kv = lambda c: slice((c // pairs_per_kv) * LANES, (c // pairs_per_kv + 1) * LANES)
        scores = []
        for j, c in chains:
            q2 = q_ref[pl.ds(pl.multiple_of(j * BLOCK, BLOCK), BLOCK), c * LANES:(c + 1) * LANES]
            scores.append(_dot_nt(q2, _block_diag_rows(band(k_ref, j, kv(c)))) + bias_ref[jnp.minimum(j, 1), c])
        weights = []
        for (j, c), s in zip(chains, scores):
            w_halves = []
            for i2 in range(2):
                half = s[:, i2 * 2 * BLOCK:(i2 + 1) * 2 * BLOCK]
                sink = sink_ref[2 * c + i2]
                m = jnp.maximum(jnp.max(half, axis=-1, keepdims=True), sink)
                p = jnp.exp(half - m)
                w_halves.append(p / (jnp.sum(p, axis=-1, keepdims=True) + jnp.exp(sink - m)))
            weights.append(jnp.concatenate(w_halves, axis=1).astype(BF16))
        for (j, c), w in zip(chains, weights):
            o = jnp.dot(w, _block_diag_rows(band(v_ref, j, kv(c))), preferred_element_type=F32)
            o_ref[pl.ds(pl.multiple_of(j * BLOCK, BLOCK), BLOCK), c * LANES:(c + 1) * LANES] = (
                o.astype(o_ref.dtype))
        return carry

    lax.fori_loop(0, nb // SWA_UNROLL, body, 0)


def _swa(proj, sinks, bias, *, bsz, seq):
    n_pairs = A_HEADS // 2
    pairs_per_kv = n_pairs // A_KV_HEADS
    q_w, kv_w = n_pairs * LANES, A_KV_HEADS * LANES
    assert seq % (BLOCK * SWA_UNROLL) == 0 and WINDOW == BLOCK, seq
    return pl.pallas_call(
        functools.partial(_swa_kernel, n_pairs=n_pairs, pairs_per_kv=pairs_per_kv),
        grid=(bsz,),
        in_specs=[pl.BlockSpec(memory_space=pltpu.SMEM),
                  pl.BlockSpec((None, seq, q_w), lambda b: (b, 0, 0)),
                  pl.BlockSpec((None, seq, kv_w), lambda b: (b, 0, q_w // kv_w)),
                  pl.BlockSpec((None, seq, kv_w), lambda b: (b, 0, q_w // kv_w + 1)),
                  pl.BlockSpec(bias.shape, lambda b: (0, 0, 0, 0))],
        out_specs=pl.BlockSpec((None, seq, q_w), lambda b: (b, 0, 0)),
        out_shape=jax.ShapeDtypeStruct((bsz, seq, q_w), BF16),
        compiler_params=_cparams(1),
        name="swa_sink",
    )(sinks, proj, proj, proj, bias)


def _diff_kernel(q_ref, k_ref, v_ref, bias_ref, lam_ref, gain_ref, o_ref, vt_ref, m_ref, acc_ref, s_ref, *,
                 lambda_init, n_heads):
    nt = q_ref.shape[0] // TILE
    lp = lam_ref[...]
    lam = (jnp.exp(jnp.sum(lp[0:1] * lp[1:2], axis=-1, keepdims=True))
           - jnp.exp(jnp.sum(lp[2:3] * lp[3:4], axis=-1, keepdims=True)) + lambda_init)

    def transpose_v(kt, carry):
        k0 = pl.multiple_of(kt * TILE, TILE)
        for h in range(n_heads):
            vt = v_ref[pl.ds(k0, TILE), h * LANES:(h + 1) * LANES].astype(F32).T
            vt_ref[h, kt] = jnp.concatenate([vt.astype(BF16), jnp.ones((V_ROWS - LANES, TILE), BF16)], axis=0)
        return carry

    lax.fori_loop(0, nt, transpose_v, 0)

    def q_tile(qi, carry):
        q0 = pl.multiple_of(qi * TILE, TILE)
        m_ref[...] = jnp.full(m_ref.shape, NEG_INF, F32)
        acc_ref[...] = jnp.zeros(acc_ref.shape, F32)

        def raw_scores(h, kt):
            cols = slice(h * LANES, (h + 1) * LANES)
            kbd = _block_diag_rows(k_ref[pl.ds(pl.multiple_of(kt * TILE, TILE), TILE), cols])
            return _dot_nt(kbd, q_ref[pl.ds(q0, TILE), cols])

        def update(get_scores, tiles):
            scores = {}
            for h in range(n_heads):
                for t, (_, bias_idx) in enumerate(tiles):
                    st = get_scores(h, t)
                    halves = [st[:TILE], st[TILE:]]
                    if bias_idx is not None:
                        halves = [x + bias_ref[h, bias_idx] for x in halves]
                    scores[h, t] = halves
            probs, alphas = [], []
            for h in range(n_heads):
                m_old = m_ref[h]
                m_blk = jnp.concatenate(
                    [functools.reduce(jnp.maximum, [jnp.max(scores[h, t][i], axis=0, keepdims=True)
                                                    for t in range(len(tiles))]) for i in range(2)], axis=1)
                m_new = jnp.maximum(m_old, m_blk)
                alphas.append(jnp.exp2(m_old - m_new)[0:1])
                m_ref[h] = m_new
                probs.append(jnp.concatenate(
                    [jnp.concatenate([jnp.exp2(scores[h, t][i] - m_new[0:1, i * TILE:(i + 1) * TILE]
                                               ).astype(BF16) for i in range(2)], axis=1)
                     for t in range(len(tiles))], axis=0))
            for h in range(n_heads):
                vt = jnp.concatenate([vt_ref[h, kt] for kt, _ in tiles], axis=1)
                pv = jnp.dot(vt, probs[h], preferred_element_type=F32)
                acc_ref[h] = acc_ref[h] * alphas[h] + pv

        def step(tiles):
            update(lambda h, t: raw_scores(h, tiles[t][0]), tiles)

        def fill(slot, kt):
            for h in range(n_heads):
                for t in range(2):
                    s_ref[slot, h, t] = raw_scores(h, kt + t)

        n_far = jnp.maximum(qi - 1, 0)
        first = n_far % 2
        n_pairs = n_far // 2

        @pl.when(first == 1)
        def _():
            step([(0, None)])

        @pl.when(qi == 0)
        def _():
            step([(0, 0)])

        @pl.when(qi >= 1)
        def _():
            from_slot = lambda slot: (lambda h, t: s_ref[slot, h, t])
            last = [(qi - 1, 1), (qi, 0)]
            fill(0, first)

            def two_pairs(j, c2):
                kt = first + 4 * j
                fill(1, kt + 2)
                update(from_slot(0), [(kt, None), (kt + 1, None)])
                fill(0, kt + 4)
                update(from_slot(1), [(kt + 2, None), (kt + 3, None)])
                return c2

            lax.fori_loop(0, n_pairs // 2, two_pairs, 0)

            @pl.when(n_pairs % 2 == 1)
            def _():
                kt = qi - 3
                fill(1, kt + 2)
                update(from_slot(0), [(kt, None), (kt + 1, None)])
                update(from_slot(1), last)

            @pl.when(n_pairs % 2 == 0)
            def _():
                update(from_slot(0), last)

        for h in range(n_heads):
            acc = acc_ref[h]
            o1 = acc[:LANES, :TILE] / acc[LANES:LANES + 1, :TILE]
            o2 = acc[:LANES, TILE:] / acc[LANES:LANES + 1, TILE:]
            o = _rms((o1 - lam * o2).T, gain_ref[...]) * (1.0 - lambda_init)
            o_ref[pl.ds(q0, TILE), h * LANES:(h + 1) * LANES] = o.astype(o_ref.dtype)
        return carry

    lax.fori_loop(0, nt, q_tile, 0)


def _diff(proj, bias, lam_rows, gain, *, bsz, seq, lambda_init, n_heads=4):
    assert seq % TILE == 0 and B_HEADS % n_heads == 0, seq
    q_col0 = A_HEADS // 2 + 2 * A_KV_HEADS
    groups = B_HEADS // n_heads
    w = n_heads * LANES
    qb, kb, vb = (c // n_heads for c in (q_col0, q_col0 + B_HEADS, q_col0 + 2 * B_HEADS))
    return pl.pallas_call(
        functools.partial(_diff_kernel, lambda_init=lambda_init, n_heads=n_heads),
        grid=(bsz, groups),
        in_specs=[pl.BlockSpec((None, seq, w), lambda b, g: (b, 0, qb + g)),
                  pl.BlockSpec((None, seq, w), lambda b, g: (b, 0, kb + g)),
                  pl.BlockSpec((None, seq, w), lambda b, g: (b, 0, vb + g), pipeline_mode=pl.Buffered(1)),
                  pl.BlockSpec((n_heads, 2, TILE, TILE), lambda b, g: (g, 0, 0, 0),
                               pipeline_mode=pl.Buffered(1)),
                  pl.BlockSpec((SUBLANES, LANES), lambda b, g: (0, 0)),
                  pl.BlockSpec((1, LANES), lambda b, g: (0, 0))],
        out_specs=pl.BlockSpec((None, seq, w), lambda b, g: (b, 0, g)),
        out_shape=jax.ShapeDtypeStruct((bsz, seq, B_HEADS * LANES), BF16),
        scratch_shapes=[pltpu.VMEM((n_heads, seq // TILE, V_ROWS, TILE), BF16),
                        pltpu.VMEM((n_heads, SUBLANES, 2 * TILE), F32),
                        pltpu.VMEM((n_heads, V_ROWS, 2 * TILE), F32),
                        pltpu.VMEM((2, n_heads, 2, 2 * TILE, TILE), F32)],
        compiler_params=_cparams(2),
        name="diff_attn",
    )(proj, proj, proj, bias, lam_rows, gain.reshape(1, LANES))


def _stick_kernel(q_ref, k_ref, v_ref, tri_ref, o_ref, r_ref, acc_ref, *, n_pairs):
    TILE = STICK_TILE
    nt = q_ref.shape[0] // TILE
    nsub = TILE // BLOCK
    row = lax.broadcasted_iota(jnp.int32, (TILE, 2 * TILE), 0)
    col = lax.broadcasted_iota(jnp.int32, (TILE, 2 * TILE), 1) & (TILE - 1)
    diag_mask = jnp.where(col < row, 0.0, NEG_INF)

    def q_tile(qi, carry):
        q0 = pl.multiple_of(qi * TILE, TILE)
        r_ref[...] = jnp.zeros(r_ref.shape, F32)
        acc_ref[...] = jnp.zeros(acc_ref.shape, F32)

        def step(tiles):
            k0s = [pl.multiple_of(kt * TILE, TILE) for kt, _ in tiles]
            work = [(p, t) for p in range(n_pairs) for t in range(len(tiles))]
            tri = tri_ref[...]
            z, hi, lo, res = {}, {}, {}, {}
            for p, t in work:
                cols = slice(p * LANES, (p + 1) * LANES)
                z[p, t] = _dot_nt(q_ref[pl.ds(q0, TILE), cols],
                                  _block_diag_rows(k_ref[pl.ds(k0s[t], TILE), cols]))
                if tiles[t][1]:
                    z[p, t] = z[p, t] + diag_mask
            for p, t in work:
                sp = jnp.maximum(z[p, t], jnp.log2(1.0 + jnp.exp2(jnp.minimum(z[p, t], EXP2_MAX))))
                hi[p, t] = sp.astype(BF16)
                lo[p, t] = (sp - hi[p, t].astype(F32)).astype(BF16)
            for t in range(len(tiles)):
                blocks = [(p, head, sub) for p in range(n_pairs) for head in range(2) for sub in range(nsub)]
                lhs = jnp.concatenate(
                    [jnp.concatenate([hi[p, t][:, head * TILE + sub * BLOCK:head * TILE + (sub + 1) * BLOCK],
                                      lo[p, t][:, head * TILE + sub * BLOCK:head * TILE + (sub + 1) * BLOCK]],
                                     axis=1) for p, head, sub in blocks], axis=0)
                sums = jnp.dot(lhs, tri, preferred_element_type=F32)
                for i, (p, head, sub) in enumerate(blocks):
                    res[p, t, head, sub] = sums[i * TILE:(i + 1) * TILE]
            weights, r_all = [], []
            for p in range(n_pairs):
                r = r_ref[p]
                r_heads = [r[:, :LANES], r[:, LANES:]]
                a_tiles = []
                for t in range(len(tiles)):
                    pieces = [None] * (2 * nsub)
                    for head in range(2):
                        for sub in reversed(range(nsub)):
                            pieces[head * nsub + sub] = res[p, t, head, sub][:, :BLOCK] + r_heads[head]
                            r_heads[head] = r_heads[head] + res[p, t, head, sub][:, BLOCK:]
                    a = jnp.exp2(z[p, t] - jnp.concatenate(pieces, axis=1))
                    a_tiles.append(a.astype(BF16))
                r_ref[p] = jnp.concatenate(r_heads, axis=1)
                r_all += r_heads
                weights.append(jnp.concatenate(a_tiles, axis=1))
            for p in range(n_pairs):
                cols = slice(p * LANES, (p + 1) * LANES)
                vbd = jnp.concatenate([_block_diag_rows(v_ref[pl.ds(k0, TILE), cols]) for k0 in k0s], axis=0)
                acc_ref[p] += jnp.dot(weights[p], vbd, preferred_element_type=F32)
            return jnp.min(functools.reduce(jnp.minimum, r_all))

        def back(state):
            kt, _ = state
            return kt - 1, step([(kt, False)])

        for n in range(1, STICK_MERGE):
            @pl.when(qi == n - 1)
            def _(n=n):
                step([(n - 1 - i, i == 0) for i in range(n)])

        @pl.when(qi >= STICK_MERGE - 1)
        def _():
            r_low = step([(qi - i, i == 0) for i in range(STICK_MERGE)])
            lax.while_loop(lambda state: (state[0] >= 0) & (state[1] < R_STOP), back, (qi - STICK_MERGE, r_low))
        for p in range(n_pairs):
            o_ref[pl.ds(q0, TILE), p * LANES:(p + 1) * LANES] = acc_ref[p].astype(o_ref.dtype)
        return carry

    lax.fori_loop(0, nt, q_tile, 0)


def _stick(proj, tri, *, bsz, seq, n_pairs_total, n_pairs=4):
    assert seq % STICK_TILE == 0 and n_pairs_total % n_pairs == 0, (seq, n_pairs_total)
    groups = n_pairs_total // n_pairs
    w = n_pairs * LANES
    return pl.pallas_call(
        functools.partial(_stick_kernel, n_pairs=n_pairs),
        grid=(bsz, groups),
        in_specs=[pl.BlockSpec((None, seq, w), lambda b, g: (b, 0, g)),
                  pl.BlockSpec((None, seq, w), lambda b, g: (b, 0, groups + g)),
                  pl.BlockSpec((None, seq, w), lambda b, g: (b, 0, 2 * groups + g)),
                  pl.BlockSpec((2 * BLOCK, 2 * BLOCK), lambda b, g: (0, 0))],
        out_specs=pl.BlockSpec((None, seq, w), lambda b, g: (b, 0, g)),
        out_shape=jax.ShapeDtypeStruct((bsz, seq, n_pairs_total * LANES), BF16),
        scratch_shapes=[pltpu.VMEM((n_pairs, STICK_TILE, 2 * LANES), F32),
                        pltpu.VMEM((n_pairs, STICK_TILE, LANES), F32)],
        compiler_params=_cparams(2),
        name="stick_breaking",
    )(proj, proj, proj, tri)


def _t5_bucket(dist):
    n = jnp.maximum(dist, 0)
    nf = jnp.maximum(n, 1).astype(F32)
    large = MAX_EXACT + (jnp.log(nf / MAX_EXACT) / math.log(MAX_DISTANCE / MAX_EXACT)
                         * (N_BUCKETS - MAX_EXACT)).astype(jnp.int32)
    large = jnp.minimum(large, N_BUCKETS - 1)
    return jnp.where(n < MAX_EXACT, n, large)


def _band_bias(rel_bias, blk):
    qi = jnp.arange(blk)[:, None]
    kj = jnp.arange(2 * blk)[None, :]
    dist = qi + blk - kj
    onehot = jax.nn.one_hot(_t5_bucket(dist), N_BUCKETS, dtype=F32)
    return jnp.einsum("qkn,nh->hqk", onehot, rel_bias, precision=lax.Precision.HIGHEST), dist


def _swa_bias(rel_bias_a):
    band, dist = _band_bias(rel_bias_a, BLOCK)
    in_window = (dist >= 0) & (dist < WINDOW)
    general = jnp.where(in_window[None], band, NEG_INF)
    first = jnp.where((in_window & (jnp.arange(2 * BLOCK)[None, :] >= BLOCK))[None], band, NEG_INF)
    both = jnp.stack([first, general])
    return both.reshape(2, A_HEADS // 2, 2, BLOCK, 2 * BLOCK).transpose(0, 1, 3, 2, 4).reshape(
        2, A_HEADS // 2, BLOCK, 4 * BLOCK)


def _diff_bias(rel_bias_b):
    band, dist = _band_bias(rel_bias_b, TILE)
    band = (band - rel_bias_b[N_BUCKETS - 1][:, None, None]) * LOG2E
    diag = jnp.where((dist[:, TILE:] >= 0)[None], band[:, :, TILE:], NEG_INF)
    return jnp.stack([diag, band[:, :, :TILE]], axis=1).swapaxes(-1, -2)


def _even_in_weight(w):
    a_q, a_kv, b_qk = A_HEADS * HEAD_DIM, A_KV_HEADS * HEAD_DIM, B_HEADS * 2 * HEAD_DIM
    splits = [a_q, a_q + a_kv, a_q + 2 * a_kv, a_q + 2 * a_kv + b_qk, a_q + 2 * a_kv + 2 * b_qk]
    aq, ak, av, bq, bk, bv = jnp.split(w, splits, axis=1)
    dup = lambda t: jnp.concatenate(
        [t[:, h * HEAD_DIM:(h + 1) * HEAD_DIM] for h in range(A_KV_HEADS) for _ in range(2)], axis=1)
    return jnp.concatenate([aq * SCALE, dup(ak), dup(av), bq * (SCALE * LOG2E), bk, bv], axis=1).astype(BF16)


def _suffix_matrix():
    j = jnp.arange(BLOCK)[:, None]
    s = jnp.arange(BLOCK)[None, :]
    half = jnp.concatenate([(j >= s).astype(BF16), jnp.ones((BLOCK, BLOCK), BF16)], axis=1)
    return jnp.concatenate([half, half], axis=0)


def kernel(x, rel_bias, norm_mix, norm_ffn, norm_final, w_in_even, w_out_even, sinks, lam_q1, lam_k1, lam_q2,
           lam_k2, diff_norm, w_in_odd, w_out_odd, ffn_up, ffn_conv, ffn_conv_b, ffn_down):
    bsz, seq, d = x.shape
    depth = norm_mix.shape[0]
    dff = ffn_down.shape[1]
    cw = 256
    n_pairs_c = d // LANES
    n = bsz * seq

    swa_bias = _swa_bias(rel_bias[:, :A_HEADS])
    diff_bias = _diff_bias(rel_bias[:, A_HEADS:])
    tri = _suffix_matrix()

    xf = x.reshape(n, d)
    for layer in range(depth):
        if layer % 2 == 0:
            e = layer // 2
            proj = _norm_proj(xf, norm_mix[layer], _even_in_weight(w_in_even[e]))
            proj = proj.reshape(bsz, seq, proj.shape[1])
            oa = _swa(proj, sinks[e], swa_bias, bsz=bsz, seq=seq)
            lambda_init = 0.8 - 0.6 * math.exp(-0.3 * layer)
            pad = lambda v: jnp.pad(v, (0, LANES - HEAD_DIM))
            lam_rows = jnp.stack([pad(lam_q1[e]), pad(lam_k1[e]), pad(lam_q2[e]), pad(lam_k2[e])]
                                 + [jnp.zeros((LANES,), F32)] * (SUBLANES - 4))
            ob = _diff(proj, diff_bias, lam_rows, diff_norm[e], bsz=bsz, seq=seq, lambda_init=lambda_init)
            w_out = w_out_even[e].astype(BF16)
            half = oa.shape[-1]
            mixed, w_outs = [oa.reshape(n, half), ob.reshape(n, half)], [w_out[:half], w_out[half:]]
        else:
            o = layer // 2
            w_in = jnp.concatenate([w_in_odd[o][:, :d] * (SCALE * LOG2E), w_in_odd[o][:, d:]],
                                   axis=1).astype(BF16)
            proj = _norm_proj(xf, norm_mix[layer], w_in).reshape(bsz, seq, 3 * d)
            oc = _stick(proj, tri, bsz=bsz, seq=seq, n_pairs_total=n_pairs_c)
            mixed, w_outs = [oc.reshape(n, d)], [w_out_odd[o].astype(BF16)]
        xf = _mix_ffn(xf, mixed, w_outs, norm_ffn[layer], ffn_up[layer].astype(BF16), ffn_conv[layer],
                      ffn_conv_b[layer].reshape(1, 2 * dff), ffn_down[layer].astype(BF16), norm_final,
                      seq=seq, cw=cw, final_norm=(layer == depth - 1))
    return xf.reshape(bsz, seq, d)
```

```python
import functools
import math

import jax
import jax.numpy as jnp
from jax import lax
from jax.experimental import pallas as pl
from jax.experimental.pallas import tpu as pltpu

LANES = 128
SUBLANES = 8
VMEM_LIMIT_BYTES = 56 * 1024 * 1024

HEAD_DIM = 64
BLOCK = 128
TILE = 256
STICK_TILE = 128
STICK_MERGE = 3
SWA_UNROLL = 2
V_ROWS = LANES + 2 * SUBLANES
WINDOW = 128
A_HEADS, A_KV_HEADS = 8, 2
B_HEADS = 4
N_BUCKETS, MAX_EXACT, MAX_DISTANCE = 32, 16, 128
CONV_W = 3
EPS = 1e-6
SCALE = HEAD_DIM ** -0.5

LOG2E = math.log2(math.e)
EXP2_MAX = 126.0
R_STOP = 152.0

F32 = jnp.float32
BF16 = jnp.bfloat16
NEG_INF = float("-inf")


def _cparams(n_axes):
    return pltpu.CompilerParams(
        dimension_semantics=("arbitrary",) * n_axes, vmem_limit_bytes=VMEM_LIMIT_BYTES)


def _rms(x, g):
    ms = jnp.mean(x * x, axis=-1, keepdims=True)
    return x * lax.rsqrt(ms + EPS) * g


def _lane_halves(rows):
    lane = lax.broadcasted_iota(jnp.int32, (rows, LANES), 1)
    return lane < HEAD_DIM


def _block_diag_rows(t):
    lo = _lane_halves(t.shape[0])
    zero = jnp.zeros_like(t)
    return jnp.concatenate([jnp.where(lo, t, zero), jnp.where(lo, zero, t)], axis=0)


def _dot_nt(a, b):
    return lax.dot_general(a, b, (((1,), (1,)), ((), ())), preferred_element_type=F32)


def _norm_proj_kernel(x_ref, g_ref, w_ref, o_ref, *, col_chunk):
    h = _rms(x_ref[...], g_ref[...]).astype(BF16)
    for n0 in range(0, o_ref.shape[1], col_chunk):
        o_ref[:, n0:n0 + col_chunk] = jnp.dot(
            h, w_ref[:, n0:n0 + col_chunk], preferred_element_type=F32).astype(o_ref.dtype)


def _norm_proj(x, g, w, *, tm=1024, col_chunk=512):
    n, d = x.shape
    nout = w.shape[1]
    assert n % tm == 0 and nout % col_chunk == 0, (n, nout)
    return pl.pallas_call(
        functools.partial(_norm_proj_kernel, col_chunk=col_chunk),
        grid=(n // tm,),
        in_specs=[pl.BlockSpec((tm, d), lambda i: (i, 0)),
                  pl.BlockSpec((1, d), lambda i: (0, 0)),
                  pl.BlockSpec((d, nout), lambda i: (0, 0), pipeline_mode=pl.Buffered(1))],
        out_specs=pl.BlockSpec((tm, nout), lambda i: (i, 0)),
        out_shape=jax.ShapeDtypeStruct((n, nout), BF16),
        compiler_params=_cparams(1),
        name="norm_proj",
    )(x, g.reshape(1, d), w)


def _mix_ffn_kernel(*refs, n_mix, tiles_per_seq, n_chunks, cw, final_norm):
    x_ref = refs[0]
    o_refs, wo_refs = refs[1:1 + n_mix], refs[1 + n_mix:1 + 2 * n_mix]
    g_ref, wup_ref, wc_ref, bc_ref, wdn_ref, gf_ref, out_ref, tail_ref, act_ref = refs[1 + 2 * n_mix:]
    tm = x_ref.shape[0]
    dff = wdn_ref.shape[0]

    @pl.when((pl.program_id(0) % tiles_per_seq) == 0)
    def _():
        tail_ref[...] = jnp.zeros(tail_ref.shape, F32)

    x = x_ref[...]
    for o_ref, wo_ref in zip(o_refs, wo_refs):
        x = x + jnp.dot(o_ref[...], wo_ref[...], preferred_element_type=F32)
    h = _rms(x, g_ref[...]).astype(BF16)
    for c in range(n_chunks):
        conv = []
        for part in range(2):
            cols = slice(part * dff + c * cw, part * dff + (c + 1) * cw)
            u = jnp.dot(h, wup_ref[:, cols], preferred_element_type=F32)
            ext = jnp.concatenate([tail_ref[2 * c + part], u], axis=0)
            tail_ref[2 * c + part] = u[tm - SUBLANES:, :]
            u1 = pltpu.roll(ext, 1, axis=0)[SUBLANES:]
            u2 = pltpu.roll(ext, 2, axis=0)[SUBLANES:]
            wc = wc_ref[:, cols]
            conv.append((u2 * wc[0:1] + u1 * wc[1:2] + u * wc[2:3]) + bc_ref[:, cols])
        gate, val = conv
        act_ref[:, c * cw:(c + 1) * cw] = (gate * jax.nn.sigmoid(gate) * val).astype(BF16)
    y = x + jnp.dot(act_ref[...], wdn_ref[...], preferred_element_type=F32)
    if final_norm:
        y = _rms(y, gf_ref[...])
    out_ref[...] = y


def _mix_ffn(x, outs, w_outs, g, wup, wc, bc, wdn, g_final, *, seq, cw, tm=1024, final_norm=False):
    n, d = x.shape
    dff = wdn.shape[0]
    n_chunks = dff // cw
    assert seq % tm == 0 and n % seq == 0 and dff % cw == 0 and wc.shape[0] == CONV_W, (n, seq, dff)
    const = lambda i: (0, 0)
    rows = lambda i: (i, 0)
    resident = lambda a: pl.BlockSpec(a.shape, const, pipeline_mode=pl.Buffered(1))
    return pl.pallas_call(
        functools.partial(_mix_ffn_kernel, n_mix=len(outs), tiles_per_seq=seq // tm, n_chunks=n_chunks,
                          cw=cw, final_norm=final_norm),
        grid=(n // tm,),
        in_specs=([pl.BlockSpec((tm, d), rows)]
                  + [pl.BlockSpec((tm, o.shape[1]), rows) for o in outs]
                  + [resident(w) for w in w_outs]
                  + [pl.BlockSpec((1, d), const), resident(wup), pl.BlockSpec(wc.shape, const),
                     pl.BlockSpec(bc.shape, const), resident(wdn), pl.BlockSpec((1, d), const)]),
        out_specs=pl.BlockSpec((tm, d), rows),
        out_shape=jax.ShapeDtypeStruct((n, d), F32),
        scratch_shapes=[pltpu.VMEM((2 * n_chunks, SUBLANES, cw), F32),
                        pltpu.VMEM((tm, dff), BF16)],
        compiler_params=_cparams(1),
        name="mix_ffn",
    )(x, *outs, *w_outs, g.reshape(1, d), wup, wc, bc, wdn, g_final.reshape(1, d))


def _swa_kernel(sink_ref, q_ref, k_ref, v_ref, bias_ref, o_ref, *, n_pairs, pairs_per_kv):
    nb = q_ref.shape[0] // BLOCK

    def band(ref, j, kv_cols):
        cur = pl.multiple_of(j * BLOCK, BLOCK)
        prev = pl.multiple_of(jnp.maximum(j - 1, 0) * BLOCK, BLOCK)
        return jnp.concatenate([ref[pl.ds(prev, BLOCK), kv_cols], ref[pl.ds(cur, BLOCK), kv_cols]], axis=0)

    def body(i, carry):
        chains = [(i * SWA_UNROLL + u, c) for u in range(SWA_UNROLL) for c in range(n_pairs)]
        kv = lambda c: slice((c // pairs_per_kv) * LANES, (c // pairs_per_kv + 1) * LANES)
        scores = []
        for j, c in chains:
            q2 = q_ref[pl.ds(pl.multiple_of(j * BLOCK, BLOCK), BLOCK), c * LANES:(c + 1) * LANES]
            scores.append(_dot_nt(q2, _block_diag_rows(band(k_ref, j, kv(c)))) + bias_ref[jnp.minimum(j, 1), c])
        weights = []
        for (j, c), s in zip(chains, scores):
            w_halves = []
            for i2 in range(2):
                half = s[:, i2 * 2 * BLOCK:(i2 + 1) * 2 * BLOCK]
                sink = sink_ref[2 * c + i2]
                m = jnp.maximum(jnp.max(half, axis=-1, keepdims=True), sink)
                p = jnp.exp(half - m)
                w_halves.append(p / (jnp.sum(p, axis=-1, keepdims=True) + jnp.exp(sink - m)))
            weights.append(jnp.concatenate(w_halves, axis=1).astype(BF16))
        for (j, c), w in zip(chains, weights):
            o = jnp.dot(w, _block_diag_rows(band(v_ref, j, kv(c))), preferred_element_type=F32)
            o_ref[pl.ds(pl.multiple_of(j * BLOCK, BLOCK), BLOCK), c * LANES:(c + 1) * LANES] = (
                o.astype(o_ref.dtype))
        return carry

    lax.fori_loop(0, nb // SWA_UNROLL, body, 0)


def _swa(proj, sinks, bias, *, bsz, seq):
    n_pairs = A_HEADS // 2
    pairs_per_kv = n_pairs // A_KV_HEADS
    q_w, kv_w = n_pairs * LANES, A_KV_HEADS * LANES
    assert seq % (BLOCK * SWA_UNROLL) == 0 and WINDOW == BLOCK, seq
    return pl.pallas_call(
        functools.partial(_swa_kernel, n_pairs=n_pairs, pairs_per_kv=pairs_per_kv),
        grid=(bsz,),
        in_specs=[pl.BlockSpec(memory_space=pltpu.SMEM),
                  pl.BlockSpec((None, seq, q_w), lambda b: (b, 0, 0)),
                  pl.BlockSpec((None, seq, kv_w), lambda b: (b, 0, q_w // kv_w)),
                  pl.BlockSpec((None, seq, kv_w), lambda b: (b, 0, q_w // kv_w + 1)),
                  pl.BlockSpec(bias.shape, lambda b: (0, 0, 0, 0))],
        out_specs=pl.BlockSpec((None, seq, q_w), lambda b: (b, 0, 0)),
        out_shape=jax.ShapeDtypeStruct((bsz, seq, q_w), BF16),
        compiler_params=_cparams(1),
        name="swa_sink",
    )(sinks, proj, proj, proj, bias)


def _diff_kernel(q_ref, k_ref, v_ref, bias_ref, lam_ref, gain_ref, o_ref, vt_ref, m_ref, acc_ref, s_ref, *,
                 lambda_init, n_heads):
    nt = q_ref.shape[0] // TILE
    lp = lam_ref[...]
    lam = (jnp.exp(jnp.sum(lp[0:1] * lp[1:2], axis=-1, keepdims=True))
           - jnp.exp(jnp.sum(lp[2:3] * lp[3:4], axis=-1, keepdims=True)) + lambda_init)

    def transpose_v(kt, carry):
        k0 = pl.multiple_of(kt * TILE, TILE)
        for h in range(n_heads):
            vt = v_ref[pl.ds(k0, TILE), h * LANES:(h + 1) * LANES].astype(F32).T
            vt_ref[h, kt] = jnp.concatenate([vt.astype(BF16), jnp.ones((V_ROWS - LANES, TILE), BF16)], axis=0)
        return carry

    lax.fori_loop(0, nt, transpose_v, 0)

    def q_tile(qi, carry):
        q0 = pl.multiple_of(qi * TILE, TILE)
        m_ref[...] = jnp.full(m_ref.shape, NEG_INF, F32)
        acc_ref[...] = jnp.zeros(acc_ref.shape, F32)

        def raw_scores(h, kt, q_start=q0):
            cols = slice(h * LANES, (h + 1) * LANES)
            kbd = _block_diag_rows(k_ref[pl.ds(pl.multiple_of(kt * TILE, TILE), TILE), cols])
            return _dot_nt(kbd, q_ref[pl.ds(q_start, TILE), cols])

        def update(get_scores, tiles):
            scores = {}
            for h in range(n_heads):
                for t, (_, bias_idx) in enumerate(tiles):
                    st = get_scores(h, t)
                    halves = [st[:TILE], st[TILE:]]
                    if bias_idx is not None:
                        halves = [x + bias_ref[h, bias_idx] for x in halves]
                    scores[h, t] = halves
            probs, alphas = [], []
            for h in range(n_heads):
                m_old = m_ref[h]
                m_blk = jnp.concatenate(
                    [functools.reduce(jnp.maximum, [jnp.max(scores[h, t][i], axis=0, keepdims=True)
                                                    for t in range(len(tiles))]) for i in range(2)], axis=1)
                m_new = jnp.maximum(m_old, m_blk)
                alphas.append(jnp.exp2(m_old - m_new)[0:1])
                m_ref[h] = m_new
                probs.append(jnp.concatenate(
                    [jnp.concatenate([jnp.exp2(scores[h, t][i] - m_new[0:1, i * TILE:(i + 1) * TILE]
                                               ).astype(BF16) for i in range(2)], axis=1)
                     for t in range(len(tiles))], axis=0))
            for h in range(n_heads):
                vt = jnp.concatenate([vt_ref[h, kt] for kt, _ in tiles], axis=1)
                pv = jnp.dot(vt, probs[h], preferred_element_type=F32)
                acc_ref[h] = acc_ref[h] * alphas[h] + pv

        def step(tiles):
            update(lambda h, t: raw_scores(h, tiles[t][0]), tiles)

        def fill(slot, kt, q_start=q0):
            for h in range(n_heads):
                for t in range(2):
                    s_ref[slot, h, t] = raw_scores(h, kt + t, q_start)

        n_far = jnp.maximum(qi - 1, 0)
        first = n_far % 2
        n_pairs = n_far // 2

        @pl.when(first == 1)
        def _():
            step([(0, None)])

        @pl.when(qi == 0)
        def _():
            step([(0, 0)])

        @pl.when(qi >= 1)
        def _():
            from_slot = lambda slot: (lambda h, t: s_ref[slot, h, t])
            last = [(qi - 1, 1), (qi, 0)]

            def two_pairs(j, c2):
                kt = first + 4 * j
                fill(1, kt + 2)
                update(from_slot(0), [(kt, None), (kt + 1, None)])
                fill(0, kt + 4)
                update(from_slot(1), [(kt + 2, None), (kt + 3, None)])
                return c2

            lax.fori_loop(0, n_pairs // 2, two_pairs, 0)

            @pl.when(n_pairs % 2 == 1)
            def _():
                kt = qi - 3
                fill(1, kt + 2)
                update(from_slot(0), [(kt, None), (kt + 1, None)])
                update(from_slot(1), last)

            @pl.when(n_pairs % 2 == 0)
            def _():
                update(from_slot(0), last)

        nxt = jnp.minimum(qi + 1, nt - 1)
        fill(0, jnp.maximum(nxt - 1, 0) % 2, pl.multiple_of(nxt * TILE, TILE))

        for h in range(n_heads):
            acc = acc_ref[h]
            o1 = acc[:LANES, :TILE] / acc[LANES:LANES + 1, :TILE]
            o2 = acc[:LANES, TILE:] / acc[LANES:LANES + 1, TILE:]
            o = _rms((o1 - lam * o2).T, gain_ref[...]) * (1.0 - lambda_init)
            o_ref[pl.ds(q0, TILE), h * LANES:(h + 1) * LANES] = o.astype(o_ref.dtype)
        return carry

    lax.fori_loop(0, nt, q_tile, 0)


def _diff(proj, bias, lam_rows, gain, *, bsz, seq, lambda_init, n_heads=4):
    assert seq % TILE == 0 and seq // TILE >= 3 and B_HEADS % n_heads == 0, seq
    q_col0 = A_HEADS // 2 + 2 * A_KV_HEADS
    groups = B_HEADS // n_heads
    w = n_heads * LANES
    qb, kb, vb = (c // n_heads for c in (q_col0, q_col0 + B_HEADS, q_col0 + 2 * B_HEADS))
    return pl.pallas_call(
        functools.partial(_diff_kernel, lambda_init=lambda_init, n_heads=n_heads),
        grid=(bsz, groups),
        in_specs=[pl.BlockSpec((None, seq, w), lambda b, g: (b, 0, qb + g)),
                  pl.BlockSpec((None, seq, w), lambda b, g: (b, 0, kb + g)),
                  pl.BlockSpec((None, seq, w), lambda b, g: (b, 0, vb + g), pipeline_mode=pl.Buffered(1)),
                  pl.BlockSpec((n_heads, 2, TILE, TILE), lambda b, g: (g, 0, 0, 0),
                               pipeline_mode=pl.Buffered(1)),
                  pl.BlockSpec((SUBLANES, LANES), lambda b, g: (0, 0)),
                  pl.BlockSpec((1, LANES), lambda b, g: (0, 0))],
        out_specs=pl.BlockSpec((None, seq, w), lambda b, g: (b, 0, g)),
        out_shape=jax.ShapeDtypeStruct((bsz, seq, B_HEADS * LANES), BF16),
        scratch_shapes=[pltpu.VMEM((n_heads, seq // TILE, V_ROWS, TILE), BF16),
                        pltpu.VMEM((n_heads, SUBLANES, 2 * TILE), F32),
                        pltpu.VMEM((n_heads, V_ROWS, 2 * TILE), F32),
                        pltpu.VMEM((2, n_heads, 2, 2 * TILE, TILE), F32)],
        compiler_params=_cparams(2),
        name="diff_attn",
    )(proj, proj, proj, bias, lam_rows, gain.reshape(1, LANES))


def _stick_kernel(q_ref, k_ref, v_ref, tri_ref, o_ref, r_ref, acc_ref, *, n_pairs):
    TILE = STICK_TILE
    nt = q_ref.shape[0] // TILE
    nsub = TILE // BLOCK
    row = lax.broadcasted_iota(jnp.int32, (TILE, 2 * TILE), 0)
    col = lax.broadcasted_iota(jnp.int32, (TILE, 2 * TILE), 1) & (TILE - 1)
    diag_mask = jnp.where(col < row, 0.0, NEG_INF)

    def q_tile(qi, carry):
        q0 = pl.multiple_of(qi * TILE, TILE)
        r_ref[...] = jnp.zeros(r_ref.shape, F32)
        acc_ref[...] = jnp.zeros(acc_ref.shape, F32)

        def step(tiles):
            k0s = [pl.multiple_of(kt * TILE, TILE) for kt, _ in tiles]
            work = [(p, t) for p in range(n_pairs) for t in range(len(tiles))]
            tri = tri_ref[...]
            z, hi, lo, res = {}, {}, {}, {}
            for p, t in work:
                cols = slice(p * LANES, (p + 1) * LANES)
                z[p, t] = _dot_nt(q_ref[pl.ds(q0, TILE), cols],
                                  _block_diag_rows(k_ref[pl.ds(k0s[t], TILE), cols]))
                if tiles[t][1]:
                    z[p, t] = z[p, t] + diag_mask
            for p, t in work:
                sp = jnp.maximum(z[p, t], jnp.log2(1.0 + jnp.exp2(jnp.minimum(z[p, t], EXP2_MAX))))
                hi[p, t] = sp.astype(BF16)
                lo[p, t] = (sp - hi[p, t].astype(F32)).astype(BF16)
            for t in range(len(tiles)):
                blocks = [(p, head, sub) for p in range(n_pairs) for head in range(2) for sub in range(nsub)]
                lhs = jnp.concatenate(
                    [jnp.concatenate([hi[p, t][:, head * TILE + sub * BLOCK:head * TILE + (sub + 1) * BLOCK],
                                      lo[p, t][:, head * TILE + sub * BLOCK:head * TILE + (sub + 1) * BLOCK]],
                                     axis=1) for p, head, sub in blocks], axis=0)
                sums = jnp.dot(lhs, tri, preferred_element_type=F32)
                for i, (p, head, sub) in enumerate(blocks):
                    res[p, t, head, sub] = sums[i * TILE:(i + 1) * TILE]
            weights, r_all = [], []
            for p in range(n_pairs):
                r = r_ref[p]
                r_heads = [r[:, :LANES], r[:, LANES:]]
                a_tiles = []
                for t in range(len(tiles)):
                    pieces = [None] * (2 * nsub)
                    for head in range(2):
                        for sub in reversed(range(nsub)):
                            pieces[head * nsub + sub] = res[p, t, head, sub][:, :BLOCK] + r_heads[head]
                            r_heads[head] = r_heads[head] + res[p, t, head, sub][:, BLOCK:]
                    a = jnp.exp2(z[p, t] - jnp.concatenate(pieces, axis=1))
                    a_tiles.append(a.astype(BF16))
                r_ref[p] = jnp.concatenate(r_heads, axis=1)
                r_all += r_heads
                weights.append(jnp.concatenate(a_tiles, axis=1))
            for p in range(n_pairs):
                cols = slice(p * LANES, (p + 1) * LANES)
                vbd = jnp.concatenate([_block_diag_rows(v_ref[pl.ds(k0, TILE), cols]) for k0 in k0s], axis=0)
                acc_ref[p] += jnp.dot(weights[p], vbd, preferred_element_type=F32)
            return jnp.min(functools.reduce(jnp.minimum, r_all))

        def back(state):
            kt, _ = state
            return kt - 1, step([(kt, False)])

        for n in range(1, STICK_MERGE):
            @pl.when(qi == n - 1)
            def _(n=n):
                step([(n - 1 - i, i == 0) for i in range(n)])

        @pl.when(qi >= STICK_MERGE - 1)
        def _():
            r_low = step([(qi - i, i == 0) for i in range(STICK_MERGE)])
            lax.while_loop(lambda state: (state[0] >= 0) & (state[1] < R_STOP), back, (qi - STICK_MERGE, r_low))
        for p in range(n_pairs):
            o_ref[pl.ds(q0, TILE), p * LANES:(p + 1) * LANES] = acc_ref[p].astype(o_ref.dtype)
        return carry

    lax.fori_loop(0, nt, q_tile, 0)


def _stick(proj, tri, *, bsz, seq, n_pairs_total, n_pairs=4):
    assert seq % STICK_TILE == 0 and n_pairs_total % n_pairs == 0, (seq, n_pairs_total)
    groups = n_pairs_total // n_pairs
    w = n_pairs * LANES
    return pl.pallas_call(
        functools.partial(_stick_kernel, n_pairs=n_pairs),
        grid=(bsz, groups),
        in_specs=[pl.BlockSpec((None, seq, w), lambda b, g: (b, 0, g)),
                  pl.BlockSpec((None, seq, w), lambda b, g: (b, 0, groups + g)),
                  pl.BlockSpec((None, seq, w), lambda b, g: (b, 0, 2 * groups + g)),
                  pl.BlockSpec((2 * BLOCK, 2 * BLOCK), lambda b, g: (0, 0))],
        out_specs=pl.BlockSpec((None, seq, w), lambda b, g: (b, 0, g)),
        out_shape=jax.ShapeDtypeStruct((bsz, seq, n_pairs_total * LANES), BF16),
        scratch_shapes=[pltpu.VMEM((n_pairs, STICK_TILE, 2 * LANES), F32),
                        pltpu.VMEM((n_pairs, STICK_TILE, LANES), F32)],
        compiler_params=_cparams(2),
        name="stick_breaking",
    )(proj, proj, proj, tri)


def _t5_bucket(dist):
    n = jnp.maximum(dist, 0)
    nf = jnp.maximum(n, 1).astype(F32)
    large = MAX_EXACT + (jnp.log(nf / MAX_EXACT) / math.log(MAX_DISTANCE / MAX_EXACT)
                         * (N_BUCKETS - MAX_EXACT)).astype(jnp.int32)
    large = jnp.minimum(large, N_BUCKETS - 1)
    return jnp.where(n < MAX_EXACT, n, large)


def _band_bias(rel_bias, blk):
    qi = jnp.arange(blk)[:, None]
    kj = jnp.arange(2 * blk)[None, :]
    dist = qi + blk - kj
    onehot = jax.nn.one_hot(_t5_bucket(dist), N_BUCKETS, dtype=F32)
    return jnp.einsum("qkn,nh->hqk", onehot, rel_bias, precision=lax.Precision.HIGHEST), dist


def _swa_bias(rel_bias_a):
    band, dist = _band_bias(rel_bias_a, BLOCK)
    in_window = (dist >= 0) & (dist < WINDOW)
    general = jnp.where(in_window[None], band, NEG_INF)
    first = jnp.where((in_window & (jnp.arange(2 * BLOCK)[None, :] >= BLOCK))[None], band, NEG_INF)
    both = jnp.stack([first, general])
    return both.reshape(2, A_HEADS // 2, 2, BLOCK, 2 * BLOCK).transpose(0, 1, 3, 2, 4).reshape(
        2, A_HEADS // 2, BLOCK, 4 * BLOCK)


def _diff_bias(rel_bias_b):
    band, dist = _band_bias(rel_bias_b, TILE)
    band = (band - rel_bias_b[N_BUCKETS - 1][:, None, None]) * LOG2E
    diag = jnp.where((dist[:, TILE:] >= 0)[None], band[:, :, TILE:], NEG_INF)
    return jnp.stack([diag, band[:, :, :TILE]], axis=1).swapaxes(-1, -2)


def _even_in_weight(w):
    a_q, a_kv, b_qk = A_HEADS * HEAD_DIM, A_KV_HEADS * HEAD_DIM, B_HEADS * 2 * HEAD_DIM
    splits = [a_q, a_q + a_kv, a_q + 2 * a_kv, a_q + 2 * a_kv + b_qk, a_q + 2 * a_kv + 2 * b_qk]
    aq, ak, av, bq, bk, bv = jnp.split(w, splits, axis=1)
    dup = lambda t: jnp.concatenate(
        [t[:, h * HEAD_DIM:(h + 1) * HEAD_DIM] for h in range(A_KV_HEADS) for _ in range(2)], axis=1)
    return jnp.concatenate([aq * SCALE, dup(ak), dup(av), bq * (SCALE * LOG2E), bk, bv], axis=1).astype(BF16)


def _suffix_matrix():
    j = jnp.arange(BLOCK)[:, None]
    s = jnp.arange(BLOCK)[None, :]
    half = jnp.concatenate([(j >= s).astype(BF16), jnp.ones((BLOCK, BLOCK), BF16)], axis=1)
    return jnp.concatenate([half, half], axis=0)


def kernel(x, rel_bias, norm_mix, norm_ffn, norm_final, w_in_even, w_out_even, sinks, lam_q1, lam_k1, lam_q2,
           lam_k2, diff_norm, w_in_odd, w_out_odd, ffn_up, ffn_conv, ffn_conv_b, ffn_down):
    bsz, seq, d = x.shape
    depth = norm_mix.shape[0]
    dff = ffn_down.shape[1]
    cw = 256
    n_pairs_c = d // LANES
    n = bsz * seq

    swa_bias = _swa_bias(rel_bias[:, :A_HEADS])
    diff_bias = _diff_bias(rel_bias[:, A_HEADS:])
    tri = _suffix_matrix()

    xf = x.reshape(n, d)
    for layer in range(depth):
        if layer % 2 == 0:
            e = layer // 2
            proj = _norm_proj(xf, norm_mix[layer], _even_in_weight(w_in_even[e]))
            proj = proj.reshape(bsz, seq, proj.shape[1])
            oa = _swa(proj, sinks[e], swa_bias, bsz=bsz, seq=seq)
            lambda_init = 0.8 - 0.6 * math.exp(-0.3 * layer)
            pad = lambda v: jnp.pad(v, (0, LANES - HEAD_DIM))
            lam_rows = jnp.stack([pad(lam_q1[e]), pad(lam_k1[e]), pad(lam_q2[e]), pad(lam_k2[e])]
                                 + [jnp.zeros((LANES,), F32)] * (SUBLANES - 4))
            ob = _diff(proj, diff_bias, lam_rows, diff_norm[e], bsz=bsz, seq=seq, lambda_init=lambda_init)
            w_out = w_out_even[e].astype(BF16)
            half = oa.shape[-1]
            mixed, w_outs = [oa.reshape(n, half), ob.reshape(n, half)], [w_out[:half], w_out[half:]]
        else:
            o = layer // 2
            w_in = jnp.concatenate([w_in_odd[o][:, :d] * (SCALE * LOG2E), w_in_odd[o][:, d:]],
                                   axis=1).astype(BF16)
            proj = _norm_proj(xf, norm_mix[layer], w_in).reshape(bsz, seq, 3 * d)
            oc = _stick(proj, tri, bsz=bsz, seq=seq, n_pairs_total=n_pairs_c)
            mixed, w_outs = [oc.reshape(n, d)], [w_out_odd[o].astype(BF16)]
        xf = _mix_ffn(xf, mixed, w_outs, norm_ffn[layer], ffn_up[layer].astype(BF16), ffn_conv[layer],
                      ffn_conv_b[layer].reshape(1, 2 * dff), ffn_down[layer].astype(BF16), norm_final,
                      seq=seq, cw=cw, final_norm=(layer == depth - 1))
    return xf.reshape(bsz, seq, d)
```

```python
import functools
import math

import jax
import jax.numpy as jnp
from jax import lax
from jax.experimental import pallas as pl
from jax.experimental.pallas import tpu as pltpu

LANES = 128
SUBLANES = 8
VMEM_LIMIT_BYTES = 56 * 1024 * 1024

HEAD_DIM = 64
BLOCK = 128
TILE = 256
STICK_TILE = 128
STICK_MERGE = 3
SWA_UNROLL = 2
V_ROWS = LANES + 2 * SUBLANES
WINDOW = 128
A_HEADS, A_KV_HEADS = 8, 2
B_HEADS = 4
N_BUCKETS, MAX_EXACT, MAX_DISTANCE = 32, 16, 128
CONV_W = 3
EPS = 1e-6
SCALE = HEAD_DIM ** -0.5

LOG2E = math.log2(math.e)
EXP2_MAX = 126.0
R_STOP = 152.0

F32 = jnp.float32
BF16 = jnp.bfloat16
NEG_INF = float("-inf")


def _cparams(n_axes):
    return pltpu.CompilerParams(
        dimension_semantics=("arbitrary",) * n_axes, vmem_limit_bytes=VMEM_LIMIT_BYTES)


def _rms(x, g):
    ms = jnp.mean(x * x, axis=-1, keepdims=True)
    return x * lax.rsqrt(ms + EPS) * g


def _lane_halves(rows):
    lane = lax.broadcasted_iota(jnp.int32, (rows, LANES), 1)
    return lane < HEAD_DIM


def _block_diag_rows(t):
    lo = _lane_halves(t.shape[0])
    zero = jnp.zeros_like(t)
    return jnp.concatenate([jnp.where(lo, t, zero), jnp.where(lo, zero, t)], axis=0)


def _dot_nt(a, b):
    return lax.dot_general(a, b, (((1,), (1,)), ((), ())), preferred_element_type=F32)


def _norm_proj_kernel(x_ref, g_ref, w_ref, o_ref, *, col_chunk):
    h = _rms(x_ref[...], g_ref[...]).astype(BF16)
    for n0 in range(0, o_ref.shape[1], col_chunk):
        o_ref[:, n0:n0 + col_chunk] = jnp.dot(
            h, w_ref[:, n0:n0 + col_chunk], preferred_element_type=F32).astype(o_ref.dtype)


def _norm_proj(x, g, w, *, tm=1024, col_chunk=512):
    n, d = x.shape
    nout = w.shape[1]
    assert n % tm == 0 and nout % col_chunk == 0, (n, nout)
    return pl.pallas_call(
        functools.partial(_norm_proj_kernel, col_chunk=col_chunk),
        grid=(n // tm,),
        in_specs=[pl.BlockSpec((tm, d), lambda i: (i, 0)),
                  pl.BlockSpec((1, d), lambda i: (0, 0)),
                  pl.BlockSpec((d, nout), lambda i: (0, 0), pipeline_mode=pl.Buffered(1))],
        out_specs=pl.BlockSpec((tm, nout), lambda i: (i, 0)),
        out_shape=jax.ShapeDtypeStruct((n, nout), BF16),
        compiler_params=_cparams(1),
        name="norm_proj",
    )(x, g.reshape(1, d), w)


def _mix_ffn_kernel(*refs, n_mix, tiles_per_seq, n_chunks, cw, final_norm):
    x_ref = refs[0]
    o_refs, wo_refs = refs[1:1 + n_mix], refs[1 + n_mix:1 + 2 * n_mix]
    g_ref, wup_ref, wc_ref, bc_ref, wdn_ref, gf_ref, out_ref, tail_ref, act_ref = refs[1 + 2 * n_mix:]
    tm = x_ref.shape[0]
    dff = wdn_ref.shape[0]

    @pl.when((pl.program_id(0) % tiles_per_seq) == 0)
    def _():
        tail_ref[...] = jnp.zeros(tail_ref.shape, F32)

    x = x_ref[...]
    for o_ref, wo_ref in zip(o_refs, wo_refs):
        x = x + jnp.dot(o_ref[...], wo_ref[...], preferred_element_type=F32)
    h = _rms(x, g_ref[...]).astype(BF16)
    for c in range(n_chunks):
        conv = []
        for part in range(2):
            cols = slice(part * dff + c * cw, part * dff + (c + 1) * cw)
            u = jnp.dot(h, wup_ref[:, cols], preferred_element_type=F32)
            ext = jnp.concatenate([tail_ref[2 * c + part], u], axis=0)
            tail_ref[2 * c + part] = u[tm - SUBLANES:, :]
            u1 = pltpu.roll(ext, 1, axis=0)[SUBLANES:]
            u2 = pltpu.roll(ext, 2, axis=0)[SUBLANES:]
            wc = wc_ref[:, cols]
            conv.append((u2 * wc[0:1] + u1 * wc[1:2] + u * wc[2:3]) + bc_ref[:, cols])
        gate, val = conv
        act_ref[:, c * cw:(c + 1) * cw] = (gate * jax.nn.sigmoid(gate) * val).astype(BF16)
    y = x + jnp.dot(act_ref[...], wdn_ref[...], preferred_element_type=F32)
    if final_norm:
        y = _rms(y, gf_ref[...])
    out_ref[...] = y


def _mix_ffn(x, outs, w_outs, g, wup, wc, bc, wdn, g_final, *, seq, cw, tm=1024, final_norm=False):
    n, d = x.shape
    dff = wdn.shape[0]
    n_chunks = dff // cw
    assert seq % tm == 0 and n % seq == 0 and dff % cw == 0 and wc.shape[0] == CONV_W, (n, seq, dff)
    const = lambda i: (0, 0)
    rows = lambda i: (i, 0)
    resident = lambda a: pl.BlockSpec(a.shape, const, pipeline_mode=pl.Buffered(1))
    return pl.pallas_call(
        functools.partial(_mix_ffn_kernel, n_mix=len(outs), tiles_per_seq=seq // tm, n_chunks=n_chunks,
                          cw=cw, final_norm=final_norm),
        grid=(n // tm,),
        in_specs=([pl.BlockSpec((tm, d), rows)]
                  + [pl.BlockSpec((tm, o.shape[1]), rows) for o in outs]
                  + [resident(w) for w in w_outs]
                  + [pl.BlockSpec((1, d), const), resident(wup), pl.BlockSpec(wc.shape, const),
                     pl.BlockSpec(bc.shape, const), resident(wdn), pl.BlockSpec((1, d), const)]),
        out_specs=pl.BlockSpec((tm, d), rows),
        out_shape=jax.ShapeDtypeStruct((n, d), F32),
        scratch_shapes=[pltpu.VMEM((2 * n_chunks, SUBLANES, cw), F32),
                        pltpu.VMEM((tm, dff), BF16)],
        compiler_params=_cparams(1),
        name="mix_ffn",
    )(x, *outs, *w_outs, g.reshape(1, d), wup, wc, bc, wdn, g_final.reshape(1, d))


def _swa_kernel(sink_ref, q_ref, k_ref, v_ref, bias_ref, o_ref, *, n_pairs, pairs_per_kv):
    nb = q_ref.shape[0] // BLOCK

    def band(ref, j, kv_cols):
        cur = pl.multiple_of(j * BLOCK, BLOCK)
        prev = pl.multiple_of(jnp.maximum(j - 1, 0) * BLOCK, BLOCK)
        return jnp.concatenate([ref[pl.ds(prev, BLOCK), kv_cols], ref[pl.ds(cur, BLOCK), kv_cols]], axis=0)

    def body(i, carry):
        chains = [(i * SWA_UNROLL + u, c) for u in range(SWA_UNROLL) for c in range(n_pairs)]
        kv = lambda c: slice((c // pairs_per_kv) * LANES, (c // pairs_per_kv + 1) * LANES)
        scores = []
        for j, c in chains:
            q2 = q_ref[pl.ds(pl.multiple_of(j * BLOCK, BLOCK), BLOCK), c * LANES:(c + 1) * LANES]
            scores.append(_dot_nt(q2, _block_diag_rows(band(k_ref, j, kv(c)))) + bias_ref[jnp.minimum(j, 1), c])
        weights = []
        for (j, c), s in zip(chains, scores):
            w_halves = []
            for i2 in range(2):
                half = s[:, i2 * 2 * BLOCK:(i2 + 1) * 2 * BLOCK]
                sink = sink_ref[2 * c + i2]
                m = jnp.maximum(jnp.max(half, axis=-1, keepdims=True), sink)
                p = jnp.exp(half - m)
                w_halves.append(p / (jnp.sum(p, axis=-1, keepdims=True) + jnp.exp(sink - m)))
            weights.append(jnp.concatenate(w_halves, axis=1).astype(BF16))
        for (j, c), w in zip(chains, weights):
            o = jnp.dot(w, _block_diag_rows(band(v_ref, j, kv(c))), preferred_element_type=F32)
            o_ref[pl.ds(pl.multiple_of(j * BLOCK, BLOCK), BLOCK), c * LANES:(c + 1) * LANES] = (
                o.astype(o_ref.dtype))
        return carry

    lax.fori_loop(0, nb // SWA_UNROLL, body, 0)


def _swa(proj, sinks, bias, *, bsz, seq):
    n_pairs = A_HEADS // 2
    pairs_per_kv = n_pairs // A_KV_HEADS
    q_w, kv_w = n_pairs * LANES, A_KV_HEADS * LANES
    assert seq % (BLOCK * SWA_UNROLL) == 0 and WINDOW == BLOCK, seq
    return pl.pallas_call(
        functools.partial(_swa_kernel, n_pairs=n_pairs, pairs_per_kv=pairs_per_kv),
        grid=(bsz,),
        in_specs=[pl.BlockSpec(memory_space=pltpu.SMEM),
                  pl.BlockSpec((None, seq, q_w), lambda b: (b, 0, 0)),
                  pl.BlockSpec((None, seq, kv_w), lambda b: (b, 0, q_w // kv_w)),
                  pl.BlockSpec((None, seq, kv_w), lambda b: (b, 0, q_w // kv_w + 1)),
                  pl.BlockSpec(bias.shape, lambda b: (0, 0, 0, 0))],
        out_specs=pl.BlockSpec((None, seq, q_w), lambda b: (b, 0, 0)),
        out_shape=jax.ShapeDtypeStruct((bsz, seq, q_w), BF16),
        compiler_params=_cparams(1),
        name="swa_sink",
    )(sinks, proj, proj, proj, bias)


def _diff_kernel(q_ref, k_ref, v_ref, bias_ref, lam_ref, gain_ref, o_ref, vt_ref, m_ref, acc_ref, s_ref, *,
                 lambda_init, n_heads):
    nt = q_ref.shape[0] // TILE
    lp = lam_ref[...]
    lam = (jnp.exp(jnp.sum(lp[0:1] * lp[1:2], axis=-1, keepdims=True))
           - jnp.exp(jnp.sum(lp[2:3] * lp[3:4], axis=-1, keepdims=True)) + lambda_init)

    def transpose_v(kt, carry):
        k0 = pl.multiple_of(kt * TILE, TILE)
        for h in range(n_heads):
            vt = v_ref[pl.ds(k0, TILE), h * LANES:(h + 1) * LANES].astype(F32).T
            vt_ref[h, kt] = jnp.concatenate([vt.astype(BF16), jnp.ones((V_ROWS - LANES, TILE), BF16)], axis=0)
        return carry

    lax.fori_loop(0, nt, transpose_v, 0)

    def q_tile(qi, carry):
        q0 = pl.multiple_of(qi * TILE, TILE)
        m_ref[...] = jnp.full(m_ref.shape, NEG_INF, F32)
        acc_ref[...] = jnp.zeros(acc_ref.shape, F32)

        def raw_scores(h, kt, q_start=q0):
            cols = slice(h * LANES, (h + 1) * LANES)
            kbd = _block_diag_rows(k_ref[pl.ds(pl.multiple_of(kt * TILE, TILE), TILE), cols])
            return _dot_nt(kbd, q_ref[pl.ds(q_start, TILE), cols])

        def update(get_scores, tiles):
            scores = {}
            for h in range(n_heads):
                for t, (_, bias_idx) in enumerate(tiles):
                    st = get_scores(h, t)
                    halves = [st[:TILE], st[TILE:]]
                    if bias_idx is not None:
                        halves = [x + bias_ref[h, bias_idx] for x in halves]
                    scores[h, t] = halves
            probs, alphas = [], []
            for h in range(n_heads):
                m_old = m_ref[h]
                m_blk = jnp.concatenate(
                    [functools.reduce(jnp.maximum, [jnp.max(scores[h, t][i], axis=0, keepdims=True)
                                                    for t in range(len(tiles))]) for i in range(2)], axis=1)
                m_new = jnp.maximum(m_old, m_blk)
                alphas.append(jnp.exp2(m_old - m_new)[0:1])
                m_ref[h] = m_new
                probs.append(jnp.concatenate(
                    [jnp.concatenate([jnp.exp2(scores[h, t][i] - m_new[0:1, i * TILE:(i + 1) * TILE]
                                               ).astype(BF16) for i in range(2)], axis=1)
                     for t in range(len(tiles))], axis=0))
            for h in range(n_heads):
                vt = jnp.concatenate([vt_ref[h, kt] for kt, _ in tiles], axis=1)
                pv = jnp.dot(vt, probs[h], preferred_element_type=F32)
                acc_ref[h] = acc_ref[h] * alphas[h] + pv

        def step(tiles):
            update(lambda h, t: raw_scores(h, tiles[t][0]), tiles)

        def fill(slot, kt, q_start=q0):
            for h in range(n_heads):
                for t in range(2):
                    s_ref[slot, h, t] = raw_scores(h, kt + t, q_start)

        n_far = jnp.maximum(qi - 1, 0)
        first = n_far % 2
        n_pairs = n_far // 2

        @pl.when(first == 1)
        def _():
            step([(0, None)])

        @pl.when(qi == 0)
        def _():
            step([(0, 0)])

        @pl.when(qi >= 1)
        def _():
            from_slot = lambda slot: (lambda h, t: s_ref[slot, h, t])
            last = [(qi - 1, 1), (qi, 0)]

            def two_pairs(j, c2):
                kt = first + 4 * j
                fill(1, kt + 2)
                update(from_slot(0), [(kt, None), (kt + 1, None)])
                fill(0, kt + 4)
                update(from_slot(1), [(kt + 2, None), (kt + 3, None)])
                return c2

            lax.fori_loop(0, n_pairs // 2, two_pairs, 0)

            @pl.when(n_pairs % 2 == 1)
            def _():
                kt = qi - 3
                fill(1, kt + 2)
                update(from_slot(0), [(kt, None), (kt + 1, None)])
                update(from_slot(1), last)

            @pl.when(n_pairs % 2 == 0)
            def _():
                update(from_slot(0), last)

        nxt = jnp.minimum(qi + 1, nt - 1)
        fill(0, jnp.maximum(nxt - 1, 0) % 2, pl.multiple_of(nxt * TILE, TILE))

        for h in range(n_heads):
            acc = acc_ref[h]
            o1 = acc[:LANES, :TILE] / acc[LANES:LANES + 1, :TILE]
            o2 = acc[:LANES, TILE:] / acc[LANES:LANES + 1, TILE:]
            o = _rms((o1 - lam * o2).T, gain_ref[...]) * (1.0 - lambda_init)
            o_ref[pl.ds(q0, TILE), h * LANES:(h + 1) * LANES] = o.astype(o_ref.dtype)
        return carry

    lax.fori_loop(0, nt, q_tile, 0)


def _diff(proj, bias, lam_rows, gain, *, bsz, seq, lambda_init, n_heads=4):
    assert seq % TILE == 0 and seq // TILE >= 3 and B_HEADS % n_heads == 0, seq
    q_col0 = A_HEADS // 2 + 2 * A_KV_HEADS
    groups = B_HEADS // n_heads
    w = n_heads * LANES
    qb, kb, vb = (c // n_heads for c in (q_col0, q_col0 + B_HEADS, q_col0 + 2 * B_HEADS))
    return pl.pallas_call(
        functools.partial(_diff_kernel, lambda_init=lambda_init, n_heads=n_heads),
        grid=(bsz, groups),
        in_specs=[pl.BlockSpec((None, seq, w), lambda b, g: (b, 0, qb + g)),
                  pl.BlockSpec((None, seq, w), lambda b, g: (b, 0, kb + g)),
                  pl.BlockSpec((None, seq, w), lambda b, g: (b, 0, vb + g), pipeline_mode=pl.Buffered(1)),
                  pl.BlockSpec((n_heads, 2, TILE, TILE), lambda b, g: (g, 0, 0, 0),
                               pipeline_mode=pl.Buffered(1)),
                  pl.BlockSpec((SUBLANES, LANES), lambda b, g: (0, 0)),
                  pl.BlockSpec((1, LANES), lambda b, g: (0, 0))],
        out_specs=pl.BlockSpec((None, seq, w), lambda b, g: (b, 0, g)),
        out_shape=jax.ShapeDtypeStruct((bsz, seq, B_HEADS * LANES), BF16),
        scratch_shapes=[pltpu.VMEM((n_heads, seq // TILE, V_ROWS, TILE), BF16),
                        pltpu.VMEM((n_heads, SUBLANES, 2 * TILE), F32),
                        pltpu.VMEM((n_heads, V_ROWS, 2 * TILE), F32),
                        pltpu.VMEM((2, n_heads, 2, 2 * TILE, TILE), F32)],
        compiler_params=_cparams(2),
        name="diff_attn",
    )(proj, proj, proj, bias, lam_rows, gain.reshape(1, LANES))


def _stick_kernel(q_ref, k_ref, v_ref, tri_ref, o_ref, r_ref, acc_ref, z_ref, *, n_pairs):
    TILE = STICK_TILE
    nt = q_ref.shape[0] // TILE
    nsub = TILE // BLOCK
    row = lax.broadcasted_iota(jnp.int32, (TILE, 2 * TILE), 0)
    col = lax.broadcasted_iota(jnp.int32, (TILE, 2 * TILE), 1) & (TILE - 1)
    diag_mask = jnp.where(col < row, 0.0, NEG_INF)

    def q_tile(qi, carry):
        q0 = pl.multiple_of(qi * TILE, TILE)
        r_ref[...] = jnp.zeros(r_ref.shape, F32)
        acc_ref[...] = jnp.zeros(acc_ref.shape, F32)

        def scores(p, kt, q_start):
            cols = slice(p * LANES, (p + 1) * LANES)
            kbd = _block_diag_rows(k_ref[pl.ds(pl.multiple_of(kt * TILE, TILE), TILE), cols])
            return _dot_nt(q_ref[pl.ds(q_start, TILE), cols], kbd)

        def step(tiles, prefetched=False, prefetch_tile=None):
            k0s = [pl.multiple_of(kt * TILE, TILE) for kt, _ in tiles]
            work = [(p, t) for p in range(n_pairs) for t in range(len(tiles))]
            tri = tri_ref[...]
            z, hi, lo, res = {}, {}, {}, {}
            for p, t in work:
                z[p, t] = z_ref[p, t] if prefetched else scores(p, tiles[t][0], q0)
                if tiles[t][1]:
                    z[p, t] = z[p, t] + diag_mask
            for p, t in work:
                sp = jnp.maximum(z[p, t], jnp.log2(1.0 + jnp.exp2(jnp.minimum(z[p, t], EXP2_MAX))))
                hi[p, t] = sp.astype(BF16)
                lo[p, t] = (sp - hi[p, t].astype(F32)).astype(BF16)
            for t in range(len(tiles)):
                blocks = [(p, head, sub) for p in range(n_pairs) for head in range(2) for sub in range(nsub)]
                lhs = jnp.concatenate(
                    [jnp.concatenate([hi[p, t][:, head * TILE + sub * BLOCK:head * TILE + (sub + 1) * BLOCK],
                                      lo[p, t][:, head * TILE + sub * BLOCK:head * TILE + (sub + 1) * BLOCK]],
                                     axis=1) for p, head, sub in blocks], axis=0)
                sums = jnp.dot(lhs, tri, preferred_element_type=F32)
                for i, (p, head, sub) in enumerate(blocks):
                    res[p, t, head, sub] = sums[i * TILE:(i + 1) * TILE]
            if prefetch_tile is not None:
                q_next = pl.multiple_of(prefetch_tile * TILE, TILE)
                for p in range(n_pairs):
                    for t in range(STICK_MERGE):
                        z_ref[p, t] = scores(p, prefetch_tile - t, q_next)
            weights, r_all = [], []
            for p in range(n_pairs):
                r = r_ref[p]
                r_heads = [r[:, :LANES], r[:, LANES:]]
                a_tiles = []
                for t in range(len(tiles)):
                    pieces = [None] * (2 * nsub)
                    for head in range(2):
                        for sub in reversed(range(nsub)):
                            pieces[head * nsub + sub] = res[p, t, head, sub][:, :BLOCK] + r_heads[head]
                            r_heads[head] = r_heads[head] + res[p, t, head, sub][:, BLOCK:]
                    a = jnp.exp2(z[p, t] - jnp.concatenate(pieces, axis=1))
                    a_tiles.append(a.astype(BF16))
                r_ref[p] = jnp.concatenate(r_heads, axis=1)
                r_all += r_heads
                weights.append(jnp.concatenate(a_tiles, axis=1))
            for p in range(n_pairs):
                cols = slice(p * LANES, (p + 1) * LANES)
                vbd = jnp.concatenate([_block_diag_rows(v_ref[pl.ds(k0, TILE), cols]) for k0 in k0s], axis=0)
                acc_ref[p] += jnp.dot(weights[p], vbd, preferred_element_type=F32)
            return jnp.min(functools.reduce(jnp.minimum, r_all))

        def back(state):
            kt, _ = state
            return kt - 1, step([(kt, False)])

        for n in range(1, STICK_MERGE):
            @pl.when(qi == n - 1)
            def _(n=n):
                step([(n - 1 - i, i == 0) for i in range(n)],
                     prefetch_tile=qi + 1 if n == STICK_MERGE - 1 else None)

        @pl.when(qi >= STICK_MERGE - 1)
        def _():
            r_low = step([(qi - i, i == 0) for i in range(STICK_MERGE)], prefetched=True,
                         prefetch_tile=jnp.minimum(qi + 1, nt - 1))
            lax.while_loop(lambda state: (state[0] >= 0) & (state[1] < R_STOP), back, (qi - STICK_MERGE, r_low))
        for p in range(n_pairs):
            o_ref[pl.ds(q0, TILE), p * LANES:(p + 1) * LANES] = acc_ref[p].astype(o_ref.dtype)
        return carry

    lax.fori_loop(0, nt, q_tile, 0)


def _stick(proj, tri, *, bsz, seq, n_pairs_total, n_pairs=4):
    assert seq % STICK_TILE == 0 and seq // STICK_TILE >= STICK_MERGE and n_pairs_total % n_pairs == 0, (
        seq, n_pairs_total)
    groups = n_pairs_total // n_pairs
    w = n_pairs * LANES
    return pl.pallas_call(
        functools.partial(_stick_kernel, n_pairs=n_pairs),
        grid=(bsz, groups),
        in_specs=[pl.BlockSpec((None, seq, w), lambda b, g: (b, 0, g)),
                  pl.BlockSpec((None, seq, w), lambda b, g: (b, 0, groups + g)),
                  pl.BlockSpec((None, seq, w), lambda b, g: (b, 0, 2 * groups + g)),
                  pl.BlockSpec((2 * BLOCK, 2 * BLOCK), lambda b, g: (0, 0))],
        out_specs=pl.BlockSpec((None, seq, w), lambda b, g: (b, 0, g)),
        out_shape=jax.ShapeDtypeStruct((bsz, seq, n_pairs_total * LANES), BF16),
        scratch_shapes=[pltpu.VMEM((n_pairs, STICK_TILE, 2 * LANES), F32),
                        pltpu.VMEM((n_pairs, STICK_TILE, LANES), F32),
                        pltpu.VMEM((n_pairs, STICK_MERGE, STICK_TILE, 2 * STICK_TILE), F32)],
        compiler_params=_cparams(2),
        name="stick_breaking",
    )(proj, proj, proj, tri)


def _t5_bucket(dist):
    n = jnp.maximum(dist, 0)
    nf = jnp.maximum(n, 1).astype(F32)
    large = MAX_EXACT + (jnp.log(nf / MAX_EXACT) / math.log(MAX_DISTANCE / MAX_EXACT)
                         * (N_BUCKETS - MAX_EXACT)).astype(jnp.int32)
    large = jnp.minimum(large, N_BUCKETS - 1)
    return jnp.where(n < MAX_EXACT, n, large)


def _band_bias(rel_bias, blk):
    qi = jnp.arange(blk)[:, None]
    kj = jnp.arange(2 * blk)[None, :]
    dist = qi + blk - kj
    onehot = jax.nn.one_hot(_t5_bucket(dist), N_BUCKETS, dtype=F32)
    return jnp.einsum("qkn,nh->hqk", onehot, rel_bias, precision=lax.Precision.HIGHEST), dist


def _swa_bias(rel_bias_a):
    band, dist = _band_bias(rel_bias_a, BLOCK)
    in_window = (dist >= 0) & (dist < WINDOW)
    general = jnp.where(in_window[None], band, NEG_INF)
    first = jnp.where((in_window & (jnp.arange(2 * BLOCK)[None, :] >= BLOCK))[None], band, NEG_INF)
    both = jnp.stack([first, general])
    return both.reshape(2, A_HEADS // 2, 2, BLOCK, 2 * BLOCK).transpose(0, 1, 3, 2, 4).reshape(
        2, A_HEADS // 2, BLOCK, 4 * BLOCK)


def _diff_bias(rel_bias_b):
    band, dist = _band_bias(rel_bias_b, TILE)
    band = (band - rel_bias_b[N_BUCKETS - 1][:, None, None]) * LOG2E
    diag = jnp.where((dist[:, TILE:] >= 0)[None], band[:, :, TILE:], NEG_INF)
    return jnp.stack([diag, band[:, :, :TILE]], axis=1).swapaxes(-1, -2)


def _even_in_weight(w):
    a_q, a_kv, b_qk = A_HEADS * HEAD_DIM, A_KV_HEADS * HEAD_DIM, B_HEADS * 2 * HEAD_DIM
    splits = [a_q, a_q + a_kv, a_q + 2 * a_kv, a_q + 2 * a_kv + b_qk, a_q + 2 * a_kv + 2 * b_qk]
    aq, ak, av, bq, bk, bv = jnp.split(w, splits, axis=1)
    dup = lambda t: jnp.concatenate(
        [t[:, h * HEAD_DIM:(h + 1) * HEAD_DIM] for h in range(A_KV_HEADS) for _ in range(2)], axis=1)
    return jnp.concatenate([aq * SCALE, dup(ak), dup(av), bq * (SCALE * LOG2E), bk, bv], axis=1).astype(BF16)


def _suffix_matrix():
    j = jnp.arange(BLOCK)[:, None]
    s = jnp.arange(BLOCK)[None, :]
    half = jnp.concatenate([(j >= s).astype(BF16), jnp.ones((BLOCK, BLOCK), BF16)], axis=1)
    return jnp.concatenate([half, half], axis=0)


def kernel(x, rel_bias, norm_mix, norm_ffn, norm_final, w_in_even, w_out_even, sinks, lam_q1, lam_k1, lam_q2,
           lam_k2, diff_norm, w_in_odd, w_out_odd, ffn_up, ffn_conv, ffn_conv_b, ffn_down):
    bsz, seq, d = x.shape
    depth = norm_mix.shape[0]
    dff = ffn_down.shape[1]
    cw = 256
    n_pairs_c = d // LANES
    n = bsz * seq

    swa_bias = _swa_bias(rel_bias[:, :A_HEADS])
    diff_bias = _diff_bias(rel_bias[:, A_HEADS:])
    tri = _suffix_matrix()

    xf = x.reshape(n, d)
    for layer in range(depth):
        if layer % 2 == 0:
            e = layer // 2
            proj = _norm_proj(xf, norm_mix[layer], _even_in_weight(w_in_even[e]))
            proj = proj.reshape(bsz, seq, proj.shape[1])
            oa = _swa(proj, sinks[e], swa_bias, bsz=bsz, seq=seq)
            lambda_init = 0.8 - 0.6 * math.exp(-0.3 * layer)
            pad = lambda v: jnp.pad(v, (0, LANES - HEAD_DIM))
            lam_rows = jnp.stack([pad(lam_q1[e]), pad(lam_k1[e]), pad(lam_q2[e]), pad(lam_k2[e])]
                                 + [jnp.zeros((LANES,), F32)] * (SUBLANES - 4))
            ob = _diff(proj, diff_bias, lam_rows, diff_norm[e], bsz=bsz, seq=seq, lambda_init=lambda_init)
            w_out = w_out_even[e].astype(BF16)
            half = oa.shape[-1]
            mixed, w_outs = [oa.reshape(n, half), ob.reshape(n, half)], [w_out[:half], w_out[half:]]
        else:
            o = layer // 2
            w_in = jnp.concatenate([w_in_odd[o][:, :d] * (SCALE * LOG2E), w_in_odd[o][:, d:]],
                                   axis=1).astype(BF16)
            proj = _norm_proj(xf, norm_mix[layer], w_in).reshape(bsz, seq, 3 * d)
            oc = _stick(proj, tri, bsz=bsz, seq=seq, n_pairs_total=n_pairs_c)
            mixed, w_outs = [oc.reshape(n, d)], [w_out_odd[o].astype(BF16)]
        xf = _mix_ffn(xf, mixed, w_outs, norm_ffn[layer], ffn_up[layer].astype(BF16), ffn_conv[layer],
                      ffn_conv_b[layer].reshape(1, 2 * dff), ffn_down[layer].astype(BF16), norm_final,
                      seq=seq, cw=cw, final_norm=(layer == depth - 1))
    return xf.reshape(bsz, seq, d)
```

```python
import functools
import math

import jax
import jax.numpy as jnp
from jax import lax
from jax.experimental import pallas as pl
from jax.experimental.pallas import tpu as pltpu

LANES = 128
SUBLANES = 8
VMEM_LIMIT_BYTES = 56 * 1024 * 1024

HEAD_DIM = 64
BLOCK = 128
TILE = 256
STICK_TILE = 128
STICK_MERGE = 3
SWA_UNROLL = 2
V_ROWS = LANES + 2 * SUBLANES
WINDOW = 128
A_HEADS, A_KV_HEADS = 8, 2
B_HEADS = 4
N_BUCKETS, MAX_EXACT, MAX_DISTANCE = 32, 16, 128
CONV_W = 3
EPS = 1e-6
SCALE = HEAD_DIM ** -0.5

LOG2E = math.log2(math.e)
EXP2_MAX = 126.0
R_STOP = 152.0

F32 = jnp.float32
BF16 = jnp.bfloat16
NEG_INF = float("-inf")


def _cparams(n_axes):
    return pltpu.CompilerParams(
        dimension_semantics=("arbitrary",) * n_axes, vmem_limit_bytes=VMEM_LIMIT_BYTES)


def _rms(x, g):
    ms = jnp.mean(x * x, axis=-1, keepdims=True)
    return x * lax.rsqrt(ms + EPS) * g


def _lane_halves(rows):
    lane = lax.broadcasted_iota(jnp.int32, (rows, LANES), 1)
    return lane < HEAD_DIM


def _block_diag_rows(t):
    lo = _lane_halves(t.shape[0])
    zero = jnp.zeros_like(t)
    return jnp.concatenate([jnp.where(lo, t, zero), jnp.where(lo, zero, t)], axis=0)


def _dot_nt(a, b):
    return lax.dot_general(a, b, (((1,), (1,)), ((), ())), preferred_element_type=F32)


def _norm_proj_kernel(x_ref, g_ref, w_ref, o_ref, *, col_chunk):
    h = _rms(x_ref[...], g_ref[...]).astype(BF16)
    for n0 in range(0, o_ref.shape[1], col_chunk):
        o_ref[:, n0:n0 + col_chunk] = jnp.dot(
            h, w_ref[:, n0:n0 + col_chunk], preferred_element_type=F32).astype(o_ref.dtype)


def _norm_proj(x, g, w, *, tm=1024, col_chunk=512):
    n, d = x.shape
    nout = w.shape[1]
    assert n % tm == 0 and nout % col_chunk == 0, (n, nout)
    return pl.pallas_call(
        functools.partial(_norm_proj_kernel, col_chunk=col_chunk),
        grid=(n // tm,),
        in_specs=[pl.BlockSpec((tm, d), lambda i: (i, 0)),
                  pl.BlockSpec((1, d), lambda i: (0, 0)),
                  pl.BlockSpec((d, nout), lambda i: (0, 0), pipeline_mode=pl.Buffered(1))],
        out_specs=pl.BlockSpec((tm, nout), lambda i: (i, 0)),
        out_shape=jax.ShapeDtypeStruct((n, nout), BF16),
        compiler_params=_cparams(1),
        name="norm_proj",
    )(x, g.reshape(1, d), w)


def _mix_ffn_kernel(*refs, n_mix, tiles_per_seq, n_chunks, cw, final_norm):
    x_ref = refs[0]
    o_refs, wo_refs = refs[1:1 + n_mix], refs[1 + n_mix:1 + 2 * n_mix]
    g_ref, wup_ref, wc_ref, bc_ref, wdn_ref, gf_ref, out_ref, tail_ref, act_ref = refs[1 + 2 * n_mix:]
    tm = x_ref.shape[0]
    dff = wdn_ref.shape[0]

    @pl.when((pl.program_id(0) % tiles_per_seq) == 0)
    def _():
        tail_ref[...] = jnp.zeros(tail_ref.shape, F32)

    x = x_ref[...]
    for o_ref, wo_ref in zip(o_refs, wo_refs):
        x = x + jnp.dot(o_ref[...], wo_ref[...], preferred_element_type=F32)
    h = _rms(x, g_ref[...]).astype(BF16)
    for c in range(n_chunks):
        conv = []
        for part in range(2):
            cols = slice(part * dff + c * cw, part * dff + (c + 1) * cw)
            u = jnp.dot(h, wup_ref[:, cols], preferred_element_type=F32)
            ext = jnp.concatenate([tail_ref[2 * c + part], u], axis=0)
            tail_ref[2 * c + part] = u[tm - SUBLANES:, :]
            u1 = pltpu.roll(ext, 1, axis=0)[SUBLANES:]
            u2 = pltpu.roll(ext, 2, axis=0)[SUBLANES:]
            wc = wc_ref[:, cols]
            conv.append((u2 * wc[0:1] + u1 * wc[1:2] + u * wc[2:3]) + bc_ref[:, cols])
        gate, val = conv
        act_ref[:, c * cw:(c + 1) * cw] = (gate * jax.nn.sigmoid(gate) * val).astype(BF16)
    y = x + jnp.dot(act_ref[...], wdn_ref[...], preferred_element_type=F32)
    if final_norm:
        y = _rms(y, gf_ref[...])
    out_ref[...] = y


def _mix_ffn(x, outs, w_outs, g, wup, wc, bc, wdn, g_final, *, seq, cw, tm=1024, final_norm=False):
    n, d = x.shape
    dff = wdn.shape[0]
    n_chunks = dff // cw
    assert seq % tm == 0 and n % seq == 0 and dff % cw == 0 and wc.shape[0] == CONV_W, (n, seq, dff)
    const = lambda i: (0, 0)
    rows = lambda i: (i, 0)
    resident = lambda a: pl.BlockSpec(a.shape, const, pipeline_mode=pl.Buffered(1))
    return pl.pallas_call(
        functools.partial(_mix_ffn_kernel, n_mix=len(outs), tiles_per_seq=seq // tm, n_chunks=n_chunks,
                          cw=cw, final_norm=final_norm),
        grid=(n // tm,),
        in_specs=([pl.BlockSpec((tm, d), rows)]
                  + [pl.BlockSpec((tm, o.shape[1]), rows) for o in outs]
                  + [resident(w) for w in w_outs]
                  + [pl.BlockSpec((1, d), const), resident(wup), pl.BlockSpec(wc.shape, const),
                     pl.BlockSpec(bc.shape, const), resident(wdn), pl.BlockSpec((1, d), const)]),
        out_specs=pl.BlockSpec((tm, d), rows),
        out_shape=jax.ShapeDtypeStruct((n, d), F32),
        scratch_shapes=[pltpu.VMEM((2 * n_chunks, SUBLANES, cw), F32),
                        pltpu.VMEM((tm, dff), BF16)],
        compiler_params=_cparams(1),
        name="mix_ffn",
    )(x, *outs, *w_outs, g.reshape(1, d), wup, wc, bc, wdn, g_final.reshape(1, d))


def _swa_kernel(sink_ref, q_ref, k_ref, v_ref, bias_ref, o_ref, *, n_pairs, pairs_per_kv):
    nb = q_ref.shape[0] // BLOCK

    def band(ref, j, kv_cols):
        cur = pl.multiple_of(j * BLOCK, BLOCK)
        prev = pl.multiple_of(jnp.maximum(j - 1, 0) * BLOCK, BLOCK)
        return jnp.concatenate([ref[pl.ds(prev, BLOCK), kv_cols], ref[pl.ds(cur, BLOCK), kv_cols]], axis=0)

    def body(i, carry):
        chains = [(i * SWA_UNROLL + u, c) for u in range(SWA_UNROLL) for c in range(n_pairs)]
        kv = lambda c: slice((c // pairs_per_kv) * LANES, (c // pairs_per_kv + 1) * LANES)
        scores = []
        for j, c in chains:
            q2 = q_ref[pl.ds(pl.multiple_of(j * BLOCK, BLOCK), BLOCK), c * LANES:(c + 1) * LANES]
            scores.append(_dot_nt(q2, _block_diag_rows(band(k_ref, j, kv(c)))) + bias_ref[jnp.minimum(j, 1), c])
        weights = []
        for (j, c), s in zip(chains, scores):
            w_halves = []
            for i2 in range(2):
                half = s[:, i2 * 2 * BLOCK:(i2 + 1) * 2 * BLOCK]
                sink = sink_ref[2 * c + i2]
                m = jnp.maximum(jnp.max(half, axis=-1, keepdims=True), sink)
                p = jnp.exp(half - m)
                w_halves.append(p / (jnp.sum(p, axis=-1, keepdims=True) + jnp.exp(sink - m)))
            weights.append(jnp.concatenate(w_halves, axis=1).astype(BF16))
        for (j, c), w in zip(chains, weights):
            o = jnp.dot(w, _block_diag_rows(band(v_ref, j, kv(c))), preferred_element_type=F32)
            o_ref[pl.ds(pl.multiple_of(j * BLOCK, BLOCK), BLOCK), c * LANES:(c + 1) * LANES] = (
                o.astype(o_ref.dtype))
        return carry

    lax.fori_loop(0, nb // SWA_UNROLL, body, 0)


def _swa(proj, sinks, bias, *, bsz, seq):
    n_pairs = A_HEADS // 2
    pairs_per_kv = n_pairs // A_KV_HEADS
    q_w, kv_w = n_pairs * LANES, A_KV_HEADS * LANES
    assert seq % (BLOCK * SWA_UNROLL) == 0 and WINDOW == BLOCK, seq
    return pl.pallas_call(
        functools.partial(_swa_kernel, n_pairs=n_pairs, pairs_per_kv=pairs_per_kv),
        grid=(bsz,),
        in_specs=[pl.BlockSpec(memory_space=pltpu.SMEM),
                  pl.BlockSpec((None, seq, q_w), lambda b: (b, 0, 0)),
                  pl.BlockSpec((None, seq, kv_w), lambda b: (b, 0, q_w // kv_w)),
                  pl.BlockSpec((None, seq, kv_w), lambda b: (b, 0, q_w // kv_w + 1)),
                  pl.BlockSpec(bias.shape, lambda b: (0, 0, 0, 0))],
        out_specs=pl.BlockSpec((None, seq, q_w), lambda b: (b, 0, 0)),
        out_shape=jax.ShapeDtypeStruct((bsz, seq, q_w), BF16),
        compiler_params=_cparams(1),
        name="swa_sink",
    )(sinks, proj, proj, proj, bias)


def _diff_kernel(q_ref, k_ref, v_ref, bias_ref, lam_ref, gain_ref, o_ref, vt_ref, m_ref, acc_ref, s_ref, *,
                 lambda_init, n_heads):
    nt = q_ref.shape[0] // TILE
    lp = lam_ref[...]
    lam = (jnp.exp(jnp.sum(lp[0:1] * lp[1:2], axis=-1, keepdims=True))
           - jnp.exp(jnp.sum(lp[2:3] * lp[3:4], axis=-1, keepdims=True)) + lambda_init)

    def transpose_v(kt, carry):
        k0 = pl.multiple_of(kt * TILE, TILE)
        for h in range(n_heads):
            vt = v_ref[pl.ds(k0, TILE), h * LANES:(h + 1) * LANES].astype(F32).T
            vt_ref[h, kt] = jnp.concatenate([vt.astype(BF16), jnp.ones((V_ROWS - LANES, TILE), BF16)], axis=0)
        return carry

    lax.fori_loop(0, nt, transpose_v, 0)

    def q_tile(qi, carry):
        q0 = pl.multiple_of(qi * TILE, TILE)
        m_ref[...] = jnp.full(m_ref.shape, NEG_INF, F32)
        acc_ref[...] = jnp.zeros(acc_ref.shape, F32)

        def raw_scores(h, kt, q_start=q0):
            cols = slice(h * LANES, (h + 1) * LANES)
            kbd = _block_diag_rows(k_ref[pl.ds(pl.multiple_of(kt * TILE, TILE), TILE), cols])
            return _dot_nt(kbd, q_ref[pl.ds(q_start, TILE), cols])

        def update(get_scores, tiles):
            scores = {}
            for h in range(n_heads):
                for t, (_, bias_idx) in enumerate(tiles):
                    st = get_scores(h, t)
                    halves = [st[:TILE], st[TILE:]]
                    if bias_idx is not None:
                        halves = [x + bias_ref[h, bias_idx] for x in halves]
                    scores[h, t] = halves
            probs, alphas = [], []
            for h in range(n_heads):
                m_old = m_ref[h]
                m_blk = jnp.concatenate(
                    [functools.reduce(jnp.maximum, [jnp.max(scores[h, t][i], axis=0, keepdims=True)
                                                    for t in range(len(tiles))]) for i in range(2)], axis=1)
                m_new = jnp.maximum(m_old, m_blk)
                alphas.append(jnp.exp2(m_old - m_new)[0:1])
                m_ref[h] = m_new
                probs.append(jnp.concatenate(
                    [jnp.concatenate([jnp.exp2(scores[h, t][i] - m_new[0:1, i * TILE:(i + 1) * TILE]
                                               ).astype(BF16) for i in range(2)], axis=1)
                     for t in range(len(tiles))], axis=0))
            for h in range(n_heads):
                vt = jnp.concatenate([vt_ref[h, kt] for kt, _ in tiles], axis=1)
                pv = jnp.dot(vt, probs[h], preferred_element_type=F32)
                acc_ref[h] = acc_ref[h] * alphas[h] + pv

        def step(tiles):
            update(lambda h, t: raw_scores(h, tiles[t][0]), tiles)

        def fill(slot, kt, q_start=q0):
            for h in range(n_heads):
                cols = slice(h * LANES, (h + 1) * LANES)
                kbd = jnp.concatenate(
                    [_block_diag_rows(k_ref[pl.ds(pl.multiple_of((kt + t) * TILE, TILE), TILE), cols])
                     for t in range(2)], axis=0)
                st = _dot_nt(kbd, q_ref[pl.ds(q_start, TILE), cols])
                for t in range(2):
                    s_ref[slot, h, t] = st[t * 2 * TILE:(t + 1) * 2 * TILE]

        n_far = jnp.maximum(qi - 1, 0)
        first = n_far % 2
        n_pairs = n_far // 2

        @pl.when(first == 1)
        def _():
            step([(0, None)])

        @pl.when(qi == 0)
        def _():
            step([(0, 0)])

        @pl.when(qi >= 1)
        def _():
            from_slot = lambda slot: (lambda h, t: s_ref[slot, h, t])
            last = [(qi - 1, 1), (qi, 0)]

            def two_pairs(j, c2):
                kt = first + 4 * j
                fill(1, kt + 2)
                update(from_slot(0), [(kt, None), (kt + 1, None)])
                fill(0, kt + 4)
                update(from_slot(1), [(kt + 2, None), (kt + 3, None)])
                return c2

            lax.fori_loop(0, n_pairs // 2, two_pairs, 0)

            @pl.when(n_pairs % 2 == 1)
            def _():
                kt = qi - 3
                fill(1, kt + 2)
                update(from_slot(0), [(kt, None), (kt + 1, None)])
                update(from_slot(1), last)

            @pl.when(n_pairs % 2 == 0)
            def _():
                update(from_slot(0), last)

        nxt = jnp.minimum(qi + 1, nt - 1)
        fill(0, jnp.maximum(nxt - 1, 0) % 2, pl.multiple_of(nxt * TILE, TILE))

        for h in range(n_heads):
            acc = acc_ref[h]
            o1 = acc[:LANES, :TILE] / acc[LANES:LANES + 1, :TILE]
            o2 = acc[:LANES, TILE:] / acc[LANES:LANES + 1, TILE:]
            o = _rms((o1 - lam * o2).T, gain_ref[...]) * (1.0 - lambda_init)
            o_ref[pl.ds(q0, TILE), h * LANES:(h + 1) * LANES] = o.astype(o_ref.dtype)
        return carry

    lax.fori_loop(0, nt, q_tile, 0)


def _diff(proj, bias, lam_rows, gain, *, bsz, seq, lambda_init, n_heads=4):
    assert seq % TILE == 0 and seq // TILE >= 3 and B_HEADS % n_heads == 0, seq
    q_col0 = A_HEADS // 2 + 2 * A_KV_HEADS
    groups = B_HEADS // n_heads
    w = n_heads * LANES
    qb, kb, vb = (c // n_heads for c in (q_col0, q_col0 + B_HEADS, q_col0 + 2 * B_HEADS))
    return pl.pallas_call(
        functools.partial(_diff_kernel, lambda_init=lambda_init, n_heads=n_heads),
        grid=(bsz, groups),
        in_specs=[pl.BlockSpec((None, seq, w), lambda b, g: (b, 0, qb + g)),
                  pl.BlockSpec((None, seq, w), lambda b, g: (b, 0, kb + g)),
                  pl.BlockSpec((None, seq, w), lambda b, g: (b, 0, vb + g), pipeline_mode=pl.Buffered(1)),
                  pl.BlockSpec((n_heads, 2, TILE, TILE), lambda b, g: (g, 0, 0, 0),
                               pipeline_mode=pl.Buffered(1)),
                  pl.BlockSpec((SUBLANES, LANES), lambda b, g: (0, 0)),
                  pl.BlockSpec((1, LANES), lambda b, g: (0, 0))],
        out_specs=pl.BlockSpec((None, seq, w), lambda b, g: (b, 0, g)),
        out_shape=jax.ShapeDtypeStruct((bsz, seq, B_HEADS * LANES), BF16),
        scratch_shapes=[pltpu.VMEM((n_heads, seq // TILE, V_ROWS, TILE), BF16),
                        pltpu.VMEM((n_heads, SUBLANES, 2 * TILE), F32),
                        pltpu.VMEM((n_heads, V_ROWS, 2 * TILE), F32),
                        pltpu.VMEM((2, n_heads, 2, 2 * TILE, TILE), F32)],
        compiler_params=_cparams(2),
        name="diff_attn",
    )(proj, proj, proj, bias, lam_rows, gain.reshape(1, LANES))


def _stick_kernel(q_ref, k_ref, v_ref, tri_ref, o_ref, r_ref, acc_ref, z_ref, *, n_pairs):
    TILE = STICK_TILE
    nt = q_ref.shape[0] // TILE
    nsub = TILE // BLOCK
    row = lax.broadcasted_iota(jnp.int32, (TILE, 2 * TILE), 0)
    col = lax.broadcasted_iota(jnp.int32, (TILE, 2 * TILE), 1) & (TILE - 1)
    diag_mask = jnp.where(col < row, 0.0, NEG_INF)

    def q_tile(qi, carry):
        q0 = pl.multiple_of(qi * TILE, TILE)
        r_ref[...] = jnp.zeros(r_ref.shape, F32)
        acc_ref[...] = jnp.zeros(acc_ref.shape, F32)

        def scores(p, kt, q_start):
            cols = slice(p * LANES, (p + 1) * LANES)
            kbd = _block_diag_rows(k_ref[pl.ds(pl.multiple_of(kt * TILE, TILE), TILE), cols])
            return _dot_nt(q_ref[pl.ds(q_start, TILE), cols], kbd)

        def step(tiles, prefetched=False, prefetch_tile=None):
            k0s = [pl.multiple_of(kt * TILE, TILE) for kt, _ in tiles]
            work = [(p, t) for p in range(n_pairs) for t in range(len(tiles))]
            tri = tri_ref[...]
            z, hi, lo, res = {}, {}, {}, {}
            for p, t in work:
                z[p, t] = z_ref[p, t] if prefetched else scores(p, tiles[t][0], q0)
                if tiles[t][1]:
                    z[p, t] = z[p, t] + diag_mask
            for p, t in work:
                sp = jnp.maximum(z[p, t], jnp.log2(1.0 + jnp.exp2(jnp.minimum(z[p, t], EXP2_MAX))))
                hi[p, t] = sp.astype(BF16)
                lo[p, t] = (sp - hi[p, t].astype(F32)).astype(BF16)
            for t in range(len(tiles)):
                blocks = [(p, head, sub) for p in range(n_pairs) for head in range(2) for sub in range(nsub)]
                lhs = jnp.concatenate(
                    [jnp.concatenate([hi[p, t][:, head * TILE + sub * BLOCK:head * TILE + (sub + 1) * BLOCK],
                                      lo[p, t][:, head * TILE + sub * BLOCK:head * TILE + (sub + 1) * BLOCK]],
                                     axis=1) for p, head, sub in blocks], axis=0)
                sums = jnp.dot(lhs, tri, preferred_element_type=F32)
                for i, (p, head, sub) in enumerate(blocks):
                    res[p, t, head, sub] = sums[i * TILE:(i + 1) * TILE]
            if prefetch_tile is not None:
                q_next = pl.multiple_of(prefetch_tile * TILE, TILE)
                for p in range(n_pairs):
                    for t in range(STICK_MERGE):
                        z_ref[p, t] = scores(p, prefetch_tile - t, q_next)
            weights, r_all = [], []
            for p in range(n_pairs):
                r = r_ref[p]
                r_heads = [r[:, :LANES], r[:, LANES:]]
                a_tiles = []
                for t in range(len(tiles)):
                    pieces = [None] * (2 * nsub)
                    for head in range(2):
                        for sub in reversed(range(nsub)):
                            pieces[head * nsub + sub] = res[p, t, head, sub][:, :BLOCK] + r_heads[head]
                            r_heads[head] = r_heads[head] + res[p, t, head, sub][:, BLOCK:]
                    a = jnp.exp2(z[p, t] - jnp.concatenate(pieces, axis=1))
                    a_tiles.append(a.astype(BF16))
                r_ref[p] = jnp.concatenate(r_heads, axis=1)
                r_all += r_heads
                weights.append(jnp.concatenate(a_tiles, axis=1))
            for p in range(n_pairs):
                cols = slice(p * LANES, (p + 1) * LANES)
                vbd = jnp.concatenate([_block_diag_rows(v_ref[pl.ds(k0, TILE), cols]) for k0 in k0s], axis=0)
                acc_ref[p] += jnp.dot(weights[p], vbd, preferred_element_type=F32)
            return jnp.min(functools.reduce(jnp.minimum, r_all))

        def back(state):
            kt, _ = state
            return kt - 1, step([(kt, False)])

        for n in range(1, STICK_MERGE):
            @pl.when(qi == n - 1)
            def _(n=n):
                step([(n - 1 - i, i == 0) for i in range(n)],
                     prefetch_tile=qi + 1 if n == STICK_MERGE - 1 else None)

        @pl.when(qi >= STICK_MERGE - 1)
        def _():
            r_low = step([(qi - i, i == 0) for i in range(STICK_MERGE)], prefetched=True,
                         prefetch_tile=jnp.minimum(qi + 1, nt - 1))
            lax.while_loop(lambda state: (state[0] >= 0) & (state[1] < R_STOP), back, (qi - STICK_MERGE, r_low))
        for p in range(n_pairs):
            o_ref[pl.ds(q0, TILE), p * LANES:(p + 1) * LANES] = acc_ref[p].astype(o_ref.dtype)
        return carry

    lax.fori_loop(0, nt, q_tile, 0)


def _stick(proj, tri, *, bsz, seq, n_pairs_total, n_pairs=4):
    assert seq % STICK_TILE == 0 and seq // STICK_TILE >= STICK_MERGE and n_pairs_total % n_pairs == 0, (
        seq, n_pairs_total)
    groups = n_pairs_total // n_pairs
    w = n_pairs * LANES
    return pl.pallas_call(
        functools.partial(_stick_kernel, n_pairs=n_pairs),
        grid=(bsz, groups),
        in_specs=[pl.BlockSpec((None, seq, w), lambda b, g: (b, 0, g)),
                  pl.BlockSpec((None, seq, w), lambda b, g: (b, 0, groups + g)),
                  pl.BlockSpec((None, seq, w), lambda b, g: (b, 0, 2 * groups + g)),
                  pl.BlockSpec((2 * BLOCK, 2 * BLOCK), lambda b, g: (0, 0))],
        out_specs=pl.BlockSpec((None, seq, w), lambda b, g: (b, 0, g)),
        out_shape=jax.ShapeDtypeStruct((bsz, seq, n_pairs_total * LANES), BF16),
        scratch_shapes=[pltpu.VMEM((n_pairs, STICK_TILE, 2 * LANES), F32),
                        pltpu.VMEM((n_pairs, STICK_TILE, LANES), F32),
                        pltpu.VMEM((n_pairs, STICK_MERGE, STICK_TILE, 2 * STICK_TILE), F32)],
        compiler_params=_cparams(2),
        name="stick_breaking",
    )(proj, proj, proj, tri)


def _t5_bucket(dist):
    n = jnp.maximum(dist, 0)
    nf = jnp.maximum(n, 1).astype(F32)
    large = MAX_EXACT + (jnp.log(nf / MAX_EXACT) / math.log(MAX_DISTANCE / MAX_EXACT)
                         * (N_BUCKETS - MAX_EXACT)).astype(jnp.int32)
    large = jnp.minimum(large, N_BUCKETS - 1)
    return jnp.where(n < MAX_EXACT, n, large)


def _band_bias(rel_bias, blk):
    qi = jnp.arange(blk)[:, None]
    kj = jnp.arange(2 * blk)[None, :]
    dist = qi + blk - kj
    onehot = jax.nn.one_hot(_t5_bucket(dist), N_BUCKETS, dtype=F32)
    return jnp.einsum("qkn,nh->hqk", onehot, rel_bias, precision=lax.Precision.HIGHEST), dist


def _swa_bias(rel_bias_a):
    band, dist = _band_bias(rel_bias_a, BLOCK)
    in_window = (dist >= 0) & (dist < WINDOW)
    general = jnp.where(in_window[None], band, NEG_INF)
    first = jnp.where((in_window & (jnp.arange(2 * BLOCK)[None, :] >= BLOCK))[None], band, NEG_INF)
    both = jnp.stack([first, general])
    return both.reshape(2, A_HEADS // 2, 2, BLOCK, 2 * BLOCK).transpose(0, 1, 3, 2, 4).reshape(
        2, A_HEADS // 2, BLOCK, 4 * BLOCK)


def _diff_bias(rel_bias_b):
    band, dist = _band_bias(rel_bias_b, TILE)
    band = (band - rel_bias_b[N_BUCKETS - 1][:, None, None]) * LOG2E
    diag = jnp.where((dist[:, TILE:] >= 0)[None], band[:, :, TILE:], NEG_INF)
    return jnp.stack([diag, band[:, :, :TILE]], axis=1).swapaxes(-1, -2)


def _even_in_weight(w):
    a_q, a_kv, b_qk = A_HEADS * HEAD_DIM, A_KV_HEADS * HEAD_DIM, B_HEADS * 2 * HEAD_DIM
    splits = [a_q, a_q + a_kv, a_q + 2 * a_kv, a_q + 2 * a_kv + b_qk, a_q + 2 * a_kv + 2 * b_qk]
    aq, ak, av, bq, bk, bv = jnp.split(w, splits, axis=1)
    dup = lambda t: jnp.concatenate(
        [t[:, h * HEAD_DIM:(h + 1) * HEAD_DIM] for h in range(A_KV_HEADS) for _ in range(2)], axis=1)
    return jnp.concatenate([aq * SCALE, dup(ak), dup(av), bq * (SCALE * LOG2E), bk, bv], axis=1).astype(BF16)


def _suffix_matrix():
    j = jnp.arange(BLOCK)[:, None]
    s = jnp.arange(BLOCK)[None, :]
    half = jnp.concatenate([(j >= s).astype(BF16), jnp.ones((BLOCK, BLOCK), BF16)], axis=1)
    return jnp.concatenate([half, half], axis=0)


def kernel(x, rel_bias, norm_mix, norm_ffn, norm_final, w_in_even, w_out_even, sinks, lam_q1, lam_k1, lam_q2,
           lam_k2, diff_norm, w_in_odd, w_out_odd, ffn_up, ffn_conv, ffn_conv_b, ffn_down):
    bsz, seq, d = x.shape
    depth = norm_mix.shape[0]
    dff = ffn_down.shape[1]
    cw = 256
    n_pairs_c = d // LANES
    n = bsz * seq

    swa_bias = _swa_bias(rel_bias[:, :A_HEADS])
    diff_bias = _diff_bias(rel_bias[:, A_HEADS:])
    tri = _suffix_matrix()

    xf = x.reshape(n, d)
    for layer in range(depth):
        if layer % 2 == 0:
            e = layer // 2
            proj = _norm_proj(xf, norm_mix[layer], _even_in_weight(w_in_even[e]))
            proj = proj.reshape(bsz, seq, proj.shape[1])
            oa = _swa(proj, sinks[e], swa_bias, bsz=bsz, seq=seq)
            lambda_init = 0.8 - 0.6 * math.exp(-0.3 * layer)
            pad = lambda v: jnp.pad(v, (0, LANES - HEAD_DIM))
            lam_rows = jnp.stack([pad(lam_q1[e]), pad(lam_k1[e]), pad(lam_q2[e]), pad(lam_k2[e])]
                                 + [jnp.zeros((LANES,), F32)] * (SUBLANES - 4))
            ob = _diff(proj, diff_bias, lam_rows, diff_norm[e], bsz=bsz, seq=seq, lambda_init=lambda_init)
            w_out = w_out_even[e].astype(BF16)
            half = oa.shape[-1]
            mixed, w_outs = [oa.reshape(n, half), ob.reshape(n, half)], [w_out[:half], w_out[half:]]
        else:
            o = layer // 2
            w_in = jnp.concatenate([w_in_odd[o][:, :d] * (SCALE * LOG2E), w_in_odd[o][:, d:]],
                                   axis=1).astype(BF16)
            proj = _norm_proj(xf, norm_mix[layer], w_in).reshape(bsz, seq, 3 * d)
            oc = _stick(proj, tri, bsz=bsz, seq=seq, n_pairs_total=n_pairs_c)
            mixed, w_outs = [oc.reshape(n, d)], [w_out_odd[o].astype(BF16)]
        xf = _mix_ffn(xf, mixed, w_outs, norm_ffn[layer], ffn_up[layer].astype(BF16), ffn_conv[layer],
                      ffn_conv_b[layer].reshape(1, 2 * dff), ffn_down[layer].astype(BF16), norm_final,
                      seq=seq, cw=cw, final_norm=(layer == depth - 1))
    return xf.reshape(bsz, seq, d)
```

```python
import functools
import math

import jax
import jax.numpy as jnp
from jax import lax
from jax.experimental import pallas as pl
from jax.experimental.pallas import tpu as pltpu

LANES = 128
SUBLANES = 8
VMEM_LIMIT_BYTES = 56 * 1024 * 1024

HEAD_DIM = 64
BLOCK = 128
TILE = 256
STICK_TILE = 128
STICK_MERGE = 3
SWA_UNROLL = 2
V_ROWS = LANES + 2 * SUBLANES
WINDOW = 128
A_HEADS, A_KV_HEADS = 8, 2
B_HEADS = 4
N_BUCKETS, MAX_EXACT, MAX_DISTANCE = 32, 16, 128
CONV_W = 3
EPS = 1e-6
SCALE = HEAD_DIM ** -0.5

LOG2E = math.log2(math.e)
EXP2_MAX = 126.0
R_STOP = 152.0

F32 = jnp.float32
BF16 = jnp.bfloat16
NEG_INF = float("-inf")


def _cparams(n_axes):
    return pltpu.CompilerParams(
        dimension_semantics=("arbitrary",) * n_axes, vmem_limit_bytes=VMEM_LIMIT_BYTES)


def _rms(x, g):
    ms = jnp.mean(x * x, axis=-1, keepdims=True)
    return x * lax.rsqrt(ms + EPS) * g


def _lane_halves(rows):
    lane = lax.broadcasted_iota(jnp.int32, (rows, LANES), 1)
    return lane < HEAD_DIM


def _block_diag_rows(t):
    lo = _lane_halves(t.shape[0])
    zero = jnp.zeros_like(t)
    return jnp.concatenate([jnp.where(lo, t, zero), jnp.where(lo, zero, t)], axis=0)


def _dot_nt(a, b):
    return lax.dot_general(a, b, (((1,), (1,)), ((), ())), preferred_element_type=F32)


def _norm_proj_kernel(x_ref, g_ref, w_ref, o_ref, *, col_chunk):
    h = _rms(x_ref[...], g_ref[...]).astype(BF16)
    for n0 in range(0, o_ref.shape[1], col_chunk):
        o_ref[:, n0:n0 + col_chunk] = jnp.dot(
            h, w_ref[:, n0:n0 + col_chunk], preferred_element_type=F32).astype(o_ref.dtype)


def _norm_proj(x, g, w, *, tm=1024, col_chunk=512):
    n, d = x.shape
    nout = w.shape[1]
    assert n % tm == 0 and nout % col_chunk == 0, (n, nout)
    return pl.pallas_call(
        functools.partial(_norm_proj_kernel, col_chunk=col_chunk),
        grid=(n // tm,),
        in_specs=[pl.BlockSpec((tm, d), lambda i: (i, 0)),
                  pl.BlockSpec((1, d), lambda i: (0, 0)),
                  pl.BlockSpec((d, nout), lambda i: (0, 0), pipeline_mode=pl.Buffered(1))],
        out_specs=pl.BlockSpec((tm, nout), lambda i: (i, 0)),
        out_shape=jax.ShapeDtypeStruct((n, nout), BF16),
        compiler_params=_cparams(1),
        name="norm_proj",
    )(x, g.reshape(1, d), w)


def _mix_ffn_kernel(*refs, n_mix, tiles_per_seq, n_chunks, cw, final_norm):
    x_ref = refs[0]
    o_refs, wo_refs = refs[1:1 + n_mix], refs[1 + n_mix:1 + 2 * n_mix]
    g_ref, wup_ref, wc_ref, bc_ref, wdn_ref, gf_ref, out_ref, tail_ref, act_ref = refs[1 + 2 * n_mix:]
    tm = x_ref.shape[0]
    dff = wdn_ref.shape[0]

    @pl.when((pl.program_id(0) % tiles_per_seq) == 0)
    def _():
        tail_ref[...] = jnp.zeros(tail_ref.shape, F32)

    x = x_ref[...]
    for o_ref, wo_ref in zip(o_refs, wo_refs):
        x = x + jnp.dot(o_ref[...], wo_ref[...], preferred_element_type=F32)
    h = _rms(x, g_ref[...]).astype(BF16)
    for c in range(n_chunks):
        conv = []
        for part in range(2):
            cols = slice(part * dff + c * cw, part * dff + (c + 1) * cw)
            u = jnp.dot(h, wup_ref[:, cols], preferred_element_type=F32)
            ext = jnp.concatenate([tail_ref[2 * c + part], u], axis=0)
            tail_ref[2 * c + part] = u[tm - SUBLANES:, :]
            u1 = pltpu.roll(ext, 1, axis=0)[SUBLANES:]
            u2 = pltpu.roll(ext, 2, axis=0)[SUBLANES:]
            wc = wc_ref[:, cols]
            conv.append((u2 * wc[0:1] + u1 * wc[1:2] + u * wc[2:3]) + bc_ref[:, cols])
        gate, val = conv
        act_ref[:, c * cw:(c + 1) * cw] = (gate * jax.nn.sigmoid(gate) * val).astype(BF16)
    y = x + jnp.dot(act_ref[...], wdn_ref[...], preferred_element_type=F32)
    if final_norm:
        y = _rms(y, gf_ref[...])
    out_ref[...] = y


def _mix_ffn(x, outs, w_outs, g, wup, wc, bc, wdn, g_final, *, seq, cw, tm=1024, final_norm=False):
    n, d = x.shape
    dff = wdn.shape[0]
    n_chunks = dff // cw
    assert seq % tm == 0 and n % seq == 0 and dff % cw == 0 and wc.shape[0] == CONV_W, (n, seq, dff)
    const = lambda i: (0, 0)
    rows = lambda i: (i, 0)
    resident = lambda a: pl.BlockSpec(a.shape, const, pipeline_mode=pl.Buffered(1))
    return pl.pallas_call(
        functools.partial(_mix_ffn_kernel, n_mix=len(outs), tiles_per_seq=seq // tm, n_chunks=n_chunks,
                          cw=cw, final_norm=final_norm),
        grid=(n // tm,),
        in_specs=([pl.BlockSpec((tm, d), rows)]
                  + [pl.BlockSpec((tm, o.shape[1]), rows) for o in outs]
                  + [resident(w) for w in w_outs]
                  + [pl.BlockSpec((1, d), const), resident(wup), pl.BlockSpec(wc.shape, const),
                     pl.BlockSpec(bc.shape, const), resident(wdn), pl.BlockSpec((1, d), const)]),
        out_specs=pl.BlockSpec((tm, d), rows),
        out_shape=jax.ShapeDtypeStruct((n, d), F32),
        scratch_shapes=[pltpu.VMEM((2 * n_chunks, SUBLANES, cw), F32),
                        pltpu.VMEM((tm, dff), BF16)],
        compiler_params=_cparams(1),
        name="mix_ffn",
    )(x, *outs, *w_outs, g.reshape(1, d), wup, wc, bc, wdn, g_final.reshape(1, d))


def _swa_kernel(sink_ref, q_ref, k_ref, v_ref, bias_ref, o_ref, *, n_pairs, pairs_per_kv):
    nb = q_ref.shape[0] // BLOCK

    def band(ref, j, kv_cols):
        cur = pl.multiple_of(j * BLOCK, BLOCK)
        prev = pl.multiple_of(jnp.maximum(j - 1, 0) * BLOCK, BLOCK)
        return jnp.concatenate([ref[pl.ds(prev, BLOCK), kv_cols], ref[pl.ds(cur, BLOCK), kv_cols]], axis=0)

    def body(i, carry):
        chains = [(i * SWA_UNROLL + u, c) for u in range(SWA_UNROLL) for c in range(n_pairs)]
        kv = lambda c: slice((c // pairs_per_kv) * LANES, (c // pairs_per_kv + 1) * LANES)
        scores = []
        for j, c in chains:
            q2 = q_ref[pl.ds(pl.multiple_of(j * BLOCK, BLOCK), BLOCK), c * LANES:(c + 1) * LANES]
            scores.append(_dot_nt(q2, _block_diag_rows(band(k_ref, j, kv(c)))) + bias_ref[jnp.minimum(j, 1), c])
        weights = []
        for (j, c), s in zip(chains, scores):
            w_halves = []
            for i2 in range(2):
                half = s[:, i2 * 2 * BLOCK:(i2 + 1) * 2 * BLOCK]
                sink = sink_ref[2 * c + i2]
                m = jnp.maximum(jnp.max(half, axis=-1, keepdims=True), sink)
                p = jnp.exp(half - m)
                w_halves.append(p / (jnp.sum(p, axis=-1, keepdims=True) + jnp.exp(sink - m)))
            weights.append(jnp.concatenate(w_halves, axis=1).astype(BF16))
        for (j, c), w in zip(chains, weights):
            o = jnp.dot(w, _block_diag_rows(band(v_ref, j, kv(c))), preferred_element_type=F32)
            o_ref[pl.ds(pl.multiple_of(j * BLOCK, BLOCK), BLOCK), c * LANES:(c + 1) * LANES] = (
                o.astype(o_ref.dtype))
        return carry

    lax.fori_loop(0, nb // SWA_UNROLL, body, 0)


def _swa(proj, sinks, bias, *, bsz, seq):
    n_pairs = A_HEADS // 2
    pairs_per_kv = n_pairs // A_KV_HEADS
    q_w, kv_w = n_pairs * LANES, A_KV_HEADS * LANES
    assert seq % (BLOCK * SWA_UNROLL) == 0 and WINDOW == BLOCK, seq
    return pl.pallas_call(
        functools.partial(_swa_kernel, n_pairs=n_pairs, pairs_per_kv=pairs_per_kv),
        grid=(bsz,),
        in_specs=[pl.BlockSpec(memory_space=pltpu.SMEM),
                  pl.BlockSpec((None, seq, q_w), lambda b: (b, 0, 0)),
                  pl.BlockSpec((None, seq, kv_w), lambda b: (b, 0, q_w // kv_w)),
                  pl.BlockSpec((None, seq, kv_w), lambda b: (b, 0, q_w // kv_w + 1)),
                  pl.BlockSpec(bias.shape, lambda b: (0, 0, 0, 0))],
        out_specs=pl.BlockSpec((None, seq, q_w), lambda b: (b, 0, 0)),
        out_shape=jax.ShapeDtypeStruct((bsz, seq, q_w), BF16),
        compiler_params=_cparams(1),
        name="swa_sink",
    )(sinks, proj, proj, proj, bias)


def _diff_kernel(q_ref, k_ref, v_ref, bias_ref, lam_ref, gain_ref, o_ref, vt_ref, m_ref, acc_ref, s_ref, *,
                 lambda_init, n_heads):
    nt = q_ref.shape[0] // TILE
    lp = lam_ref[...]
    lam = (jnp.exp(jnp.sum(lp[0:1] * lp[1:2], axis=-1, keepdims=True))
           - jnp.exp(jnp.sum(lp[2:3] * lp[3:4], axis=-1, keepdims=True)) + lambda_init)

    def transpose_v(kt, carry):
        k0 = pl.multiple_of(kt * TILE, TILE)
        for h in range(n_heads):
            vt = v_ref[pl.ds(k0, TILE), h * LANES:(h + 1) * LANES].astype(F32).T
            vt_ref[h, kt] = jnp.concatenate([vt.astype(BF16), jnp.ones((V_ROWS - LANES, TILE), BF16)], axis=0)
        return carry

    lax.fori_loop(0, nt, transpose_v, 0)

    def q_tile(qi, carry):
        q0 = pl.multiple_of(qi * TILE, TILE)
        m_ref[...] = jnp.full(m_ref.shape, NEG_INF, F32)
        acc_ref[...] = jnp.zeros(acc_ref.shape, F32)

        def raw_scores(h, kt, q_start=q0):
            cols = slice(h * LANES, (h + 1) * LANES)
            kbd = _block_diag_rows(k_ref[pl.ds(pl.multiple_of(kt * TILE, TILE), TILE), cols])
            return _dot_nt(kbd, q_ref[pl.ds(q_start, TILE), cols])

        def update(get_scores, tiles):
            scores = {}
            for h in range(n_heads):
                for t, (_, bias_idx) in enumerate(tiles):
                    st = get_scores(h, t)
                    halves = [st[:TILE], st[TILE:]]
                    if bias_idx is not None:
                        halves = [x + bias_ref[h, bias_idx] for x in halves]
                    scores[h, t] = halves
            probs, alphas = [], []
            for h in range(n_heads):
                m_old = m_ref[h]
                m_blk = jnp.concatenate(
                    [functools.reduce(jnp.maximum, [jnp.max(scores[h, t][i], axis=0, keepdims=True)
                                                    for t in range(len(tiles))]) for i in range(2)], axis=1)
                m_new = jnp.maximum(m_old, m_blk)
                alphas.append(jnp.exp2(m_old - m_new)[0:1])
                m_ref[h] = m_new
                probs.append(jnp.concatenate(
                    [jnp.concatenate([jnp.exp2(scores[h, t][i] - m_new[0:1, i * TILE:(i + 1) * TILE]
                                               ).astype(BF16) for i in range(2)], axis=1)
                     for t in range(len(tiles))], axis=0))
            for h in range(n_heads):
                vt = jnp.concatenate([vt_ref[h, kt] for kt, _ in tiles], axis=1)
                pv = jnp.dot(vt, probs[h], preferred_element_type=F32)
                acc_ref[h] = acc_ref[h] * alphas[h] + pv

        def step(tiles):
            update(lambda h, t: raw_scores(h, tiles[t][0]), tiles)

        def fill(slot, kt, q_start=q0):
            for h in range(n_heads):
                for t in range(2):
                    s_ref[slot, h, t] = raw_scores(h, kt + t, q_start)

        n_far = jnp.maximum(qi - 1, 0)
        first = n_far % 2
        n_pairs = n_far // 2

        @pl.when(first == 1)
        def _():
            step([(0, None)])

        @pl.when(qi == 0)
        def _():
            step([(0, 0)])

        @pl.when(qi >= 1)
        def _():
            from_slot = lambda slot: (lambda h, t: s_ref[slot, h, t])
            last = [(qi - 1, 1), (qi, 0)]

            def two_pairs(j, c2):
                kt = first + 4 * j
                fill(1, kt + 2)
                update(from_slot(0), [(kt, None), (kt + 1, None)])
                fill(0, kt + 4)
                update(from_slot(1), [(kt + 2, None), (kt + 3, None)])
                return c2

            lax.fori_loop(0, n_pairs // 2, two_pairs, 0)

            @pl.when(n_pairs % 2 == 1)
            def _():
                kt = qi - 3
                fill(1, kt + 2)
                update(from_slot(0), [(kt, None), (kt + 1, None)])
                update(from_slot(1), last)

            @pl.when(n_pairs % 2 == 0)
            def _():
                update(from_slot(0), last)

        nxt = jnp.minimum(qi + 1, nt - 1)
        fill(0, jnp.maximum(nxt - 1, 0) % 2, pl.multiple_of(nxt * TILE, TILE))

        for h in range(n_heads):
            acc = acc_ref[h]
            o1 = acc[:LANES, :TILE] / acc[LANES:LANES + 1, :TILE]
            o2 = acc[:LANES, TILE:] / acc[LANES:LANES + 1, TILE:]
            o = _rms((o1 - lam * o2).T, gain_ref[...]) * (1.0 - lambda_init)
            o_ref[pl.ds(q0, TILE), h * LANES:(h + 1) * LANES] = o.astype(o_ref.dtype)
        return carry

    lax.fori_loop(0, nt, q_tile, 0)


def _diff(proj, bias, lam_rows, gain, *, bsz, seq, lambda_init, n_heads=4):
    assert seq % TILE == 0 and seq // TILE >= 3 and B_HEADS % n_heads == 0, seq
    q_col0 = A_HEADS // 2 + 2 * A_KV_HEADS
    groups = B_HEADS // n_heads
    w = n_heads * LANES
    qb, kb, vb = (c // n_heads for c in (q_col0, q_col0 + B_HEADS, q_col0 + 2 * B_HEADS))
    return pl.pallas_call(
        functools.partial(_diff_kernel, lambda_init=lambda_init, n_heads=n_heads),
        grid=(bsz, groups),
        in_specs=[pl.BlockSpec((None, seq, w), lambda b, g: (b, 0, qb + g)),
                  pl.BlockSpec((None, seq, w), lambda b, g: (b, 0, kb + g)),
                  pl.BlockSpec((None, seq, w), lambda b, g: (b, 0, vb + g), pipeline_mode=pl.Buffered(1)),
                  pl.BlockSpec((n_heads, 2, TILE, TILE), lambda b, g: (g, 0, 0, 0),
                               pipeline_mode=pl.Buffered(1)),
                  pl.BlockSpec((SUBLANES, LANES), lambda b, g: (0, 0)),
                  pl.BlockSpec((1, LANES), lambda b, g: (0, 0))],
        out_specs=pl.BlockSpec((None, seq, w), lambda b, g: (b, 0, g)),
        out_shape=jax.ShapeDtypeStruct((bsz, seq, B_HEADS * LANES), BF16),
        scratch_shapes=[pltpu.VMEM((n_heads, seq // TILE, V_ROWS, TILE), BF16),
                        pltpu.VMEM((n_heads, SUBLANES, 2 * TILE), F32),
                        pltpu.VMEM((n_heads, V_ROWS, 2 * TILE), F32),
                        pltpu.VMEM((2, n_heads, 2, 2 * TILE, TILE), F32)],
        compiler_params=_cparams(2),
        name="diff_attn",
    )(proj, proj, proj, bias, lam_rows, gain.reshape(1, LANES))


def _stick_kernel(q_ref, k_ref, v_ref, tri_ref, o_ref, r_ref, acc_ref, z_ref, *, n_pairs):
    TILE = STICK_TILE
    nt = q_ref.shape[0] // TILE
    nsub = TILE // BLOCK
    row = lax.broadcasted_iota(jnp.int32, (TILE, 2 * TILE), 0)
    col = lax.broadcasted_iota(jnp.int32, (TILE, 2 * TILE), 1) & (TILE - 1)
    diag_mask = jnp.where(col < row, 0.0, NEG_INF)

    def q_tile(qi, carry):
        q0 = pl.multiple_of(qi * TILE, TILE)
        r_ref[...] = jnp.zeros(r_ref.shape, F32)
        acc_ref[...] = jnp.zeros(acc_ref.shape, F32)

        def scores(p, kt, q_start):
            cols = slice(p * LANES, (p + 1) * LANES)
            kbd = _block_diag_rows(k_ref[pl.ds(pl.multiple_of(kt * TILE, TILE), TILE), cols])
            return _dot_nt(q_ref[pl.ds(q_start, TILE), cols], kbd)

        def step(tiles, prefetched=False, prefetch_tile=None):
            k0s = [pl.multiple_of(kt * TILE, TILE) for kt, _ in tiles]
            work = [(p, t) for p in range(n_pairs) for t in range(len(tiles))]
            tri = tri_ref[...]
            z, hi, lo, res = {}, {}, {}, {}
            for p, t in work:
                z[p, t] = z_ref[p, t] if prefetched else scores(p, tiles[t][0], q0)
                if tiles[t][1]:
                    z[p, t] = z[p, t] + diag_mask
            for p, t in work:
                sp = jnp.maximum(z[p, t], jnp.log2(1.0 + jnp.exp2(jnp.minimum(z[p, t], EXP2_MAX))))
                hi[p, t] = sp.astype(BF16)
            for t in range(len(tiles)):
                blocks = [(p, head, sub) for p in range(n_pairs) for head in range(2) for sub in range(nsub)]
                lhs = jnp.concatenate(
                    [hi[p, t][:, head * TILE + sub * BLOCK:head * TILE + (sub + 1) * BLOCK]
                     for p, head, sub in blocks], axis=0)
                sums = jnp.dot(lhs, tri, preferred_element_type=F32)
                for i, (p, head, sub) in enumerate(blocks):
                    res[p, t, head, sub] = sums[i * TILE:(i + 1) * TILE]
            if prefetch_tile is not None:
                q_next = pl.multiple_of(prefetch_tile * TILE, TILE)
                for p in range(n_pairs):
                    for t in range(STICK_MERGE):
                        z_ref[p, t] = scores(p, prefetch_tile - t, q_next)
            weights, r_all = [], []
            for p in range(n_pairs):
                r = r_ref[p]
                r_heads = [r[:, :LANES], r[:, LANES:]]
                a_tiles = []
                for t in range(len(tiles)):
                    pieces = [None] * (2 * nsub)
                    for head in range(2):
                        for sub in reversed(range(nsub)):
                            pieces[head * nsub + sub] = res[p, t, head, sub][:, :BLOCK] + r_heads[head]
                            r_heads[head] = r_heads[head] + res[p, t, head, sub][:, BLOCK:]
                    a = jnp.exp2(z[p, t] - jnp.concatenate(pieces, axis=1))
                    a_tiles.append(a.astype(BF16))
                r_ref[p] = jnp.concatenate(r_heads, axis=1)
                r_all += r_heads
                weights.append(jnp.concatenate(a_tiles, axis=1))
            for p in range(n_pairs):
                cols = slice(p * LANES, (p + 1) * LANES)
                vbd = jnp.concatenate([_block_diag_rows(v_ref[pl.ds(k0, TILE), cols]) for k0 in k0s], axis=0)
                acc_ref[p] += jnp.dot(weights[p], vbd, preferred_element_type=F32)
            return jnp.min(functools.reduce(jnp.minimum, r_all))

        def back(state):
            kt, _ = state
            return kt - 1, step([(kt, False)])

        for n in range(1, STICK_MERGE):
            @pl.when(qi == n - 1)
            def _(n=n):
                step([(n - 1 - i, i == 0) for i in range(n)],
                     prefetch_tile=qi + 1 if n == STICK_MERGE - 1 else None)

        @pl.when(qi >= STICK_MERGE - 1)
        def _():
            r_low = step([(qi - i, i == 0) for i in range(STICK_MERGE)], prefetched=True,
                         prefetch_tile=jnp.minimum(qi + 1, nt - 1))
            lax.while_loop(lambda state: (state[0] >= 0) & (state[1] < R_STOP), back, (qi - STICK_MERGE, r_low))
        for p in range(n_pairs):
            o_ref[pl.ds(q0, TILE), p * LANES:(p + 1) * LANES] = acc_ref[p].astype(o_ref.dtype)
        return carry

    lax.fori_loop(0, nt, q_tile, 0)


def _stick(proj, tri, *, bsz, seq, n_pairs_total, n_pairs=4):
    assert seq % STICK_TILE == 0 and seq // STICK_TILE >= STICK_MERGE and n_pairs_total % n_pairs == 0, (
        seq, n_pairs_total)
    groups = n_pairs_total // n_pairs
    w = n_pairs * LANES
    return pl.pallas_call(
        functools.partial(_stick_kernel, n_pairs=n_pairs),
        grid=(bsz, groups),
        in_specs=[pl.BlockSpec((None, seq, w), lambda b, g: (b, 0, g)),
                  pl.BlockSpec((None, seq, w), lambda b, g: (b, 0, groups + g)),
                  pl.BlockSpec((None, seq, w), lambda b, g: (b, 0, 2 * groups + g)),
                  pl.BlockSpec((BLOCK, 2 * BLOCK), lambda b, g: (0, 0))],
        out_specs=pl.BlockSpec((None, seq, w), lambda b, g: (b, 0, g)),
        out_shape=jax.ShapeDtypeStruct((bsz, seq, n_pairs_total * LANES), BF16),
        scratch_shapes=[pltpu.VMEM((n_pairs, STICK_TILE, 2 * LANES), F32),
                        pltpu.VMEM((n_pairs, STICK_TILE, LANES), F32),
                        pltpu.VMEM((n_pairs, STICK_MERGE, STICK_TILE, 2 * STICK_TILE), F32)],
        compiler_params=_cparams(2),
        name="stick_breaking",
    )(proj, proj, proj, tri)


def _t5_bucket(dist):
    n = jnp.maximum(dist, 0)
    nf = jnp.maximum(n, 1).astype(F32)
    large = MAX_EXACT + (jnp.log(nf / MAX_EXACT) / math.log(MAX_DISTANCE / MAX_EXACT)
                         * (N_BUCKETS - MAX_EXACT)).astype(jnp.int32)
    large = jnp.minimum(large, N_BUCKETS - 1)
    return jnp.where(n < MAX_EXACT, n, large)


def _band_bias(rel_bias, blk):
    qi = jnp.arange(blk)[:, None]
    kj = jnp.arange(2 * blk)[None, :]
    dist = qi + blk - kj
    onehot = jax.nn.one_hot(_t5_bucket(dist), N_BUCKETS, dtype=F32)
    return jnp.einsum("qkn,nh->hqk", onehot, rel_bias, precision=lax.Precision.HIGHEST), dist


def _swa_bias(rel_bias_a):
    band, dist = _band_bias(rel_bias_a, BLOCK)
    in_window = (dist >= 0) & (dist < WINDOW)
    general = jnp.where(in_window[None], band, NEG_INF)
    first = jnp.where((in_window & (jnp.arange(2 * BLOCK)[None, :] >= BLOCK))[None], band, NEG_INF)
    both = jnp.stack([first, general])
    return both.reshape(2, A_HEADS // 2, 2, BLOCK, 2 * BLOCK).transpose(0, 1, 3, 2, 4).reshape(
        2, A_HEADS // 2, BLOCK, 4 * BLOCK)


def _diff_bias(rel_bias_b):
    band, dist = _band_bias(rel_bias_b, TILE)
    band = (band - rel_bias_b[N_BUCKETS - 1][:, None, None]) * LOG2E
    diag = jnp.where((dist[:, TILE:] >= 0)[None], band[:, :, TILE:], NEG_INF)
    return jnp.stack([diag, band[:, :, :TILE]], axis=1).swapaxes(-1, -2)


def _even_in_weight(w):
    a_q, a_kv, b_qk = A_HEADS * HEAD_DIM, A_KV_HEADS * HEAD_DIM, B_HEADS * 2 * HEAD_DIM
    splits = [a_q, a_q + a_kv, a_q + 2 * a_kv, a_q + 2 * a_kv + b_qk, a_q + 2 * a_kv + 2 * b_qk]
    aq, ak, av, bq, bk, bv = jnp.split(w, splits, axis=1)
    dup = lambda t: jnp.concatenate(
        [t[:, h * HEAD_DIM:(h + 1) * HEAD_DIM] for h in range(A_KV_HEADS) for _ in range(2)], axis=1)
    return jnp.concatenate([aq * SCALE, dup(ak), dup(av), bq * (SCALE * LOG2E), bk, bv], axis=1).astype(BF16)


def _suffix_matrix():
    j = jnp.arange(BLOCK)[:, None]
    s = jnp.arange(BLOCK)[None, :]
    return jnp.concatenate([(j >= s).astype(BF16), jnp.ones((BLOCK, BLOCK), BF16)], axis=1)


def kernel(x, rel_bias, norm_mix, norm_ffn, norm_final, w_in_even, w_out_even, sinks, lam_q1, lam_k1, lam_q2,
           lam_k2, diff_norm, w_in_odd, w_out_odd, ffn_up, ffn_conv, ffn_conv_b, ffn_down):
    bsz, seq, d = x.shape
    depth = norm_mix.shape[0]
    dff = ffn_down.shape[1]
    cw = 256
    n_pairs_c = d // LANES
    n = bsz * seq

    swa_bias = _swa_bias(rel_bias[:, :A_HEADS])
    diff_bias = _diff_bias(rel_bias[:, A_HEADS:])
    tri = _suffix_matrix()

    xf = x.reshape(n, d)
    for layer in range(depth):
        if layer % 2 == 0:
            e = layer // 2
            proj = _norm_proj(xf, norm_mix[layer], _even_in_weight(w_in_even[e]))
            proj = proj.reshape(bsz, seq, proj.shape[1])
            oa = _swa(proj, sinks[e], swa_bias, bsz=bsz, seq=seq)
            lambda_init = 0.8 - 0.6 * math.exp(-0.3 * layer)
            pad = lambda v: jnp.pad(v, (0, LANES - HEAD_DIM))
            lam_rows = jnp.stack([pad(lam_q1[e]), pad(lam_k1[e]), pad(lam_q2[e]), pad(lam_k2[e])]
                                 + [jnp.zeros((LANES,), F32)] * (SUBLANES - 4))
            ob = _diff(proj, diff_bias, lam_rows, diff_norm[e], bsz=bsz, seq=seq, lambda_init=lambda_init)
            w_out = w_out_even[e].astype(BF16)
            half = oa.shape[-1]
            mixed, w_outs = [oa.reshape(n, half), ob.reshape(n, half)], [w_out[:half], w_out[half:]]
        else:
            o = layer // 2
            w_in = jnp.concatenate([w_in_odd[o][:, :d] * (SCALE * LOG2E), w_in_odd[o][:, d:]],
                                   axis=1).astype(BF16)
            proj = _norm_proj(xf, norm_mix[layer], w_in).reshape(bsz, seq, 3 * d)
            oc = _stick(proj, tri, bsz=bsz, seq=seq, n_pairs_total=n_pairs_c)
            mixed, w_outs = [oc.reshape(n, d)], [w_out_odd[o].astype(BF16)]
        xf = _mix_ffn(xf, mixed, w_outs, norm_ffn[layer], ffn_up[layer].astype(BF16), ffn_conv[layer],
                      ffn_conv_b[layer].reshape(1, 2 * dff), ffn_down[layer].astype(BF16), norm_final,
                      seq=seq, cw=cw, final_norm=(layer == depth - 1))
    return xf.reshape(bsz, seq, d)
```

```python
import functools
import math

import jax
import jax.numpy as jnp
from jax import lax
from jax.experimental import pallas as pl
from jax.experimental.pallas import tpu as pltpu

LANES = 128
SUBLANES = 8
VMEM_LIMIT_BYTES = 56 * 1024 * 1024

HEAD_DIM = 64
BLOCK = 128
TILE = 256
STICK_TILE = 128
STICK_MERGE = 3
SWA_UNROLL = 2
V_ROWS = LANES + 2 * SUBLANES
WINDOW = 128
A_HEADS, A_KV_HEADS = 8, 2
B_HEADS = 4
N_BUCKETS, MAX_EXACT, MAX_DISTANCE = 32, 16, 128
CONV_W = 3
EPS = 1e-6
SCALE = HEAD_DIM ** -0.5

LOG2E = math.log2(math.e)
EXP2_MAX = 126.0
R_STOP = 152.0

F32 = jnp.float32
BF16 = jnp.bfloat16
NEG_INF = float("-inf")


def _cparams(n_axes):
    return pltpu.CompilerParams(
        dimension_semantics=("arbitrary",) * n_axes, vmem_limit_bytes=VMEM_LIMIT_BYTES)


def _rms(x, g):
    ms = jnp.mean(x * x, axis=-1, keepdims=True)
    return x * lax.rsqrt(ms + EPS) * g


def _lane_halves(rows):
    lane = lax.broadcasted_iota(jnp.int32, (rows, LANES), 1)
    return lane < HEAD_DIM


def _block_diag_rows(t):
    lo = _lane_halves(t.shape[0])
    zero = jnp.zeros_like(t)
    return jnp.concatenate([jnp.where(lo, t, zero), jnp.where(lo, zero, t)], axis=0)


def _dot_nt(a, b):
    return lax.dot_general(a, b, (((1,), (1,)), ((), ())), preferred_element_type=F32)


def _norm_proj_kernel(x_ref, g_ref, w_ref, o_ref, *, col_chunk):
    h = _rms(x_ref[...], g_ref[...]).astype(BF16)
    for n0 in range(0, o_ref.shape[1], col_chunk):
        o_ref[:, n0:n0 + col_chunk] = jnp.dot(
            h, w_ref[:, n0:n0 + col_chunk], preferred_element_type=F32).astype(o_ref.dtype)


def _norm_proj(x, g, w, *, tm=1024, col_chunk=512):
    n, d = x.shape
    nout = w.shape[1]
    assert n % tm == 0 and nout % col_chunk == 0, (n, nout)
    return pl.pallas_call(
        functools.partial(_norm_proj_kernel, col_chunk=col_chunk),
        grid=(n // tm,),
        in_specs=[pl.BlockSpec((tm, d), lambda i: (i, 0)),
                  pl.BlockSpec((1, d), lambda i: (0, 0)),
                  pl.BlockSpec((d, nout), lambda i: (0, 0), pipeline_mode=pl.Buffered(1))],
        out_specs=pl.BlockSpec((tm, nout), lambda i: (i, 0)),
        out_shape=jax.ShapeDtypeStruct((n, nout), BF16),
        compiler_params=_cparams(1),
        name="norm_proj",
    )(x, g.reshape(1, d), w)


def _mix_ffn_kernel(*refs, n_mix, tiles_per_seq, n_chunks, cw, final_norm):
    x_ref = refs[0]
    o_refs, wo_refs = refs[1:1 + n_mix], refs[1 + n_mix:1 + 2 * n_mix]
    g_ref, wup_ref, wc_ref, bc_ref, wdn_ref, gf_ref, out_ref, tail_ref, act_ref = refs[1 + 2 * n_mix:]
    tm = x_ref.shape[0]
    dff = wdn_ref.shape[0]

    @pl.when((pl.program_id(0) % tiles_per_seq) == 0)
    def _():
        tail_ref[...] = jnp.zeros(tail_ref.shape, F32)

    x = x_ref[...]
    for o_ref, wo_ref in zip(o_refs, wo_refs):
        x = x + jnp.dot(o_ref[...], wo_ref[...], preferred_element_type=F32)
    h = _rms(x, g_ref[...]).astype(BF16)
    for c in range(n_chunks):
        conv = []
        for part in range(2):
            cols = slice(part * dff + c * cw, part * dff + (c + 1) * cw)
            u = jnp.dot(h, wup_ref[:, cols], preferred_element_type=F32)
            ext = jnp.concatenate([tail_ref[2 * c + part], u], axis=0)
            tail_ref[2 * c + part] = u[tm - SUBLANES:, :]
            u1 = pltpu.roll(ext, 1, axis=0)[SUBLANES:]
            u2 = pltpu.roll(ext, 2, axis=0)[SUBLANES:]
            wc = wc_ref[:, cols]
            conv.append((u2 * wc[0:1] + u1 * wc[1:2] + u * wc[2:3]) + bc_ref[:, cols])
        gate, val = conv
        act_ref[:, c * cw:(c + 1) * cw] = (gate * jax.nn.sigmoid(gate) * val).astype(BF16)
    y = x + jnp.dot(act_ref[...], wdn_ref[...], preferred_element_type=F32)
    if final_norm:
        y = _rms(y, gf_ref[...])
    out_ref[...] = y


def _mix_ffn(x, outs, w_outs, g, wup, wc, bc, wdn, g_final, *, seq, cw, tm=1024, final_norm=False):
    n, d = x.shape
    dff = wdn.shape[0]
    n_chunks = dff // cw
    assert seq % tm == 0 and n % seq == 0 and dff % cw == 0 and wc.shape[0] == CONV_W, (n, seq, dff)
    const = lambda i: (0, 0)
    rows = lambda i: (i, 0)
    resident = lambda a: pl.BlockSpec(a.shape, const, pipeline_mode=pl.Buffered(1))
    return pl.pallas_call(
        functools.partial(_mix_ffn_kernel, n_mix=len(outs), tiles_per_seq=seq // tm, n_chunks=n_chunks,
                          cw=cw, final_norm=final_norm),
        grid=(n // tm,),
        in_specs=([pl.BlockSpec((tm, d), rows)]
                  + [pl.BlockSpec((tm, o.shape[1]), rows) for o in outs]
                  + [resident(w) for w in w_outs]
                  + [pl.BlockSpec((1, d), const), resident(wup), pl.BlockSpec(wc.shape, const),
                     pl.BlockSpec(bc.shape, const), resident(wdn), pl.BlockSpec((1, d), const)]),
        out_specs=pl.BlockSpec((tm, d), rows),
        out_shape=jax.ShapeDtypeStruct((n, d), F32),
        scratch_shapes=[pltpu.VMEM((2 * n_chunks, SUBLANES, cw), F32),
                        pltpu.VMEM((tm, dff), BF16)],
        compiler_params=_cparams(1),
        name="mix_ffn",
    )(x, *outs, *w_outs, g.reshape(1, d), wup, wc, bc, wdn, g_final.reshape(1, d))


def _swa_kernel(sink_ref, q_ref, k_ref, v_ref, bias_ref, o_ref, *, n_pairs, pairs_per_kv):
    nb = q_ref.shape[0] // BLOCK

    def band(ref, j, kv_cols):
        cur = pl.multiple_of(j * BLOCK, BLOCK)
        prev = pl.multiple_of(jnp.maximum(j - 1, 0) * BLOCK, BLOCK)
        return jnp.concatenate([ref[pl.ds(prev, BLOCK), kv_cols], ref[pl.ds(cur, BLOCK), kv_cols]], axis=0)

    def body(i, carry):
        chains = [(i * SWA_UNROLL + u, c) for u in range(SWA_UNROLL) for c in range(n_pairs)]
        kv = lambda c: slice((c // pairs_per_kv) * LANES, (c // pairs_per_kv + 1) * LANES)
        scores = []
        for j, c in chains:
            q2 = q_ref[pl.ds(pl.multiple_of(j * BLOCK, BLOCK), BLOCK), c * LANES:(c + 1) * LANES]
            scores.append(_dot_nt(q2, _block_diag_rows(band(k_ref, j, kv(c)))) + bias_ref[jnp.minimum(j, 1), c])
        weights = []
        for (j, c), s in zip(chains, scores):
            w_halves = []
            for i2 in range(2):
                half = s[:, i2 * 2 * BLOCK:(i2 + 1) * 2 * BLOCK]
                sink = sink_ref[2 * c + i2]
                m = jnp.maximum(jnp.max(half, axis=-1, keepdims=True), sink)
                p = jnp.exp(half - m)
                w_halves.append(p / (jnp.sum(p, axis=-1, keepdims=True) + jnp.exp(sink - m)))
            weights.append(jnp.concatenate(w_halves, axis=1).astype(BF16))
        for (j, c), w in zip(chains, weights):
            o = jnp.dot(w, _block_diag_rows(band(v_ref, j, kv(c))), preferred_element_type=F32)
            o_ref[pl.ds(pl.multiple_of(j * BLOCK, BLOCK), BLOCK), c * LANES:(c + 1) * LANES] = (
                o.astype(o_ref.dtype))
        return carry

    lax.fori_loop(0, nb // SWA_UNROLL, body, 0)


def _swa(proj, sinks, bias, *, bsz, seq):
    n_pairs = A_HEADS // 2
    pairs_per_kv = n_pairs // A_KV_HEADS
    q_w, kv_w = n_pairs * LANES, A_KV_HEADS * LANES
    assert seq % (BLOCK * SWA_UNROLL) == 0 and WINDOW == BLOCK, seq
    return pl.pallas_call(
        functools.partial(_swa_kernel, n_pairs=n_pairs, pairs_per_kv=pairs_per_kv),
        grid=(bsz,),
        in_specs=[pl.BlockSpec(memory_space=pltpu.SMEM),
                  pl.BlockSpec((None, seq, q_w), lambda b: (b, 0, 0)),
                  pl.BlockSpec((None, seq, kv_w), lambda b: (b, 0, q_w // kv_w)),
                  pl.BlockSpec((None, seq, kv_w), lambda b: (b, 0, q_w // kv_w + 1)),
                  pl.BlockSpec(bias.shape, lambda b: (0, 0, 0, 0))],
        out_specs=pl.BlockSpec((None, seq, q_w), lambda b: (b, 0, 0)),
        out_shape=jax.ShapeDtypeStruct((bsz, seq, q_w), BF16),
        compiler_params=_cparams(1),
        name="swa_sink",
    )(sinks, proj, proj, proj, bias)


def _diff_kernel(q_ref, k_ref, v_ref, bias_ref, lam_ref, gain_ref, o_ref, vt_ref, m_ref, acc_ref, s_ref, *,
                 lambda_init, n_heads):
    nt = q_ref.shape[0] // TILE
    lp = lam_ref[...]
    lam = (jnp.exp(jnp.sum(lp[0:1] * lp[1:2], axis=-1, keepdims=True))
           - jnp.exp(jnp.sum(lp[2:3] * lp[3:4], axis=-1, keepdims=True)) + lambda_init)

    def transpose_v(kt, carry):
        k0 = pl.multiple_of(kt * TILE, TILE)
        for h in range(n_heads):
            vt = v_ref[pl.ds(k0, TILE), h * LANES:(h + 1) * LANES].astype(F32).T
            vt_ref[h, kt] = jnp.concatenate([vt.astype(BF16), jnp.ones((V_ROWS - LANES, TILE), BF16)], axis=0)
        return carry

    lax.fori_loop(0, nt, transpose_v, 0)

    def q_tile(qi, carry):
        q0 = pl.multiple_of(qi * TILE, TILE)
        m_ref[...] = jnp.full(m_ref.shape, NEG_INF, F32)
        acc_ref[...] = jnp.zeros(acc_ref.shape, F32)

        def raw_scores(h, kt, q_start=q0):
            cols = slice(h * LANES, (h + 1) * LANES)
            kbd = _block_diag_rows(k_ref[pl.ds(pl.multiple_of(kt * TILE, TILE), TILE), cols])
            return _dot_nt(kbd, q_ref[pl.ds(q_start, TILE), cols])

        def update(get_scores, tiles):
            scores = {}
            for h in range(n_heads):
                for t, (_, bias_idx) in enumerate(tiles):
                    st = get_scores(h, t)
                    halves = [st[:TILE], st[TILE:]]
                    if bias_idx is not None:
                        halves = [x + bias_ref[h, bias_idx] for x in halves]
                    scores[h, t] = halves
            probs, alphas = [], []
            for h in range(n_heads):
                m_old = m_ref[h]
                m_blk = jnp.concatenate(
                    [functools.reduce(jnp.maximum, [jnp.max(scores[h, t][i], axis=0, keepdims=True)
                                                    for t in range(len(tiles))]) for i in range(2)], axis=1)
                m_new = jnp.maximum(m_old, m_blk)
                alphas.append(jnp.exp2(m_old - m_new)[0:1])
                m_ref[h] = m_new
                probs.append(jnp.concatenate(
                    [jnp.concatenate([jnp.exp2(scores[h, t][i] - m_new[0:1, i * TILE:(i + 1) * TILE]
                                               ).astype(BF16) for i in range(2)], axis=1)
                     for t in range(len(tiles))], axis=0))
            for h in range(n_heads):
                vt = jnp.concatenate([vt_ref[h, kt] for kt, _ in tiles], axis=1)
                pv = jnp.dot(vt, probs[h], preferred_element_type=F32)
                acc_ref[h] = acc_ref[h] * alphas[h] + pv

        def step(tiles):
            update(lambda h, t: raw_scores(h, tiles[t][0]), tiles)

        def fill(slot, kt, q_start=q0):
            for h in range(n_heads):
                for t in range(2):
                    s_ref[slot, h, t] = raw_scores(h, kt + t, q_start)

        n_far = jnp.maximum(qi - 1, 0)
        first = n_far % 2
        n_pairs = n_far // 2

        @pl.when(first == 1)
        def _():
            step([(0, None)])

        @pl.when(qi == 0)
        def _():
            step([(0, 0)])

        @pl.when(qi >= 1)
        def _():
            from_slot = lambda slot: (lambda h, t: s_ref[slot, h, t])
            last = [(qi - 1, 1), (qi, 0)]

            def two_pairs(j, c2):
                kt = first + 4 * j
                fill(1, kt + 2)
                update(from_slot(0), [(kt, None), (kt + 1, None)])
                fill(0, kt + 4)
                update(from_slot(1), [(kt + 2, None), (kt + 3, None)])
                return c2

            lax.fori_loop(0, n_pairs // 2, two_pairs, 0)

            @pl.when(n_pairs % 2 == 1)
            def _():
                kt = qi - 3
                fill(1, kt + 2)
                update(from_slot(0), [(kt, None), (kt + 1, None)])
                update(from_slot(1), last)

            @pl.when(n_pairs % 2 == 0)
            def _():
                update(from_slot(0), last)

        nxt = jnp.minimum(qi + 1, nt - 1)
        fill(0, jnp.maximum(nxt - 1, 0) % 2, pl.multiple_of(nxt * TILE, TILE))

        for h in range(n_heads):
            acc = acc_ref[h]
            o1 = acc[:LANES, :TILE] / acc[LANES:LANES + 1, :TILE]
            o2 = acc[:LANES, TILE:] / acc[LANES:LANES + 1, TILE:]
            o = _rms((o1 - lam * o2).T, gain_ref[...]) * (1.0 - lambda_init)
            o_ref[pl.ds(q0, TILE), h * LANES:(h + 1) * LANES] = o.astype(o_ref.dtype)
        return carry

    lax.fori_loop(0, nt, q_tile, 0)


def _diff(proj, bias, lam_rows, gain, *, bsz, seq, lambda_init, n_heads=4):
    assert seq % TILE == 0 and seq // TILE >= 3 and B_HEADS % n_heads == 0, seq
    q_col0 = A_HEADS // 2 + 2 * A_KV_HEADS
    groups = B_HEADS // n_heads
    w = n_heads * LANES
    qb, kb, vb = (c // n_heads for c in (q_col0, q_col0 + B_HEADS, q_col0 + 2 * B_HEADS))
    return pl.pallas_call(
        functools.partial(_diff_kernel, lambda_init=lambda_init, n_heads=n_heads),
        grid=(bsz, groups),
        in_specs=[pl.BlockSpec((None, seq, w), lambda b, g: (b, 0, qb + g)),
                  pl.BlockSpec((None, seq, w), lambda b, g: (b, 0, kb + g)),
                  pl.BlockSpec((None, seq, w), lambda b, g: (b, 0, vb + g), pipeline_mode=pl.Buffered(1)),
                  pl.BlockSpec((n_heads, 2, TILE, TILE), lambda b, g: (g, 0, 0, 0),
                               pipeline_mode=pl.Buffered(1)),
                  pl.BlockSpec((SUBLANES, LANES), lambda b, g: (0, 0)),
                  pl.BlockSpec((1, LANES), lambda b, g: (0, 0))],
        out_specs=pl.BlockSpec((None, seq, w), lambda b, g: (b, 0, g)),
        out_shape=jax.ShapeDtypeStruct((bsz, seq, B_HEADS * LANES), BF16),
        scratch_shapes=[pltpu.VMEM((n_heads, seq // TILE, V_ROWS, TILE), BF16),
                        pltpu.VMEM((n_heads, SUBLANES, 2 * TILE), F32),
                        pltpu.VMEM((n_heads, V_ROWS, 2 * TILE), F32),
                        pltpu.VMEM((2, n_heads, 2, 2 * TILE, TILE), F32)],
        compiler_params=_cparams(2),
        name="diff_attn",
    )(proj, proj, proj, bias, lam_rows, gain.reshape(1, LANES))


def _stick_kernel(q_ref, k_ref, v_ref, tri_ref, o_ref, r_ref, acc_ref, z_ref, *, n_pairs):
    TILE = STICK_TILE
    nt = q_ref.shape[0] // TILE
    nsub = TILE // BLOCK
    row = lax.broadcasted_iota(jnp.int32, (TILE, 2 * TILE), 0)
    col = lax.broadcasted_iota(jnp.int32, (TILE, 2 * TILE), 1) & (TILE - 1)
    diag_mask = jnp.where(col < row, 0.0, NEG_INF)

    def q_tile(qi, carry):
        q0 = pl.multiple_of(qi * TILE, TILE)
        r_ref[...] = jnp.zeros(r_ref.shape, F32)
        acc_ref[...] = jnp.zeros(acc_ref.shape, F32)

        def scores(p, kt, q_start):
            cols = slice(p * LANES, (p + 1) * LANES)
            kbd = _block_diag_rows(k_ref[pl.ds(pl.multiple_of(kt * TILE, TILE), TILE), cols])
            return _dot_nt(q_ref[pl.ds(q_start, TILE), cols], kbd)

        def step(tiles, prefetched=False, prefetch_tile=None):
            k0s = [pl.multiple_of(kt * TILE, TILE) for kt, _ in tiles]
            work = [(p, t) for p in range(n_pairs) for t in range(len(tiles))]
            tri = tri_ref[...]
            z, hi, res = {}, {}, {}
            for p, t in work:
                z[p, t] = z_ref[p, t] if prefetched else scores(p, tiles[t][0], q0)
                if tiles[t][1]:
                    z[p, t] = z[p, t] + diag_mask
            for p, t in work:
                sp = jnp.maximum(z[p, t], jnp.log2(1.0 + jnp.exp2(jnp.minimum(z[p, t], EXP2_MAX))))
                hi[p, t] = sp.astype(BF16)
            for t in range(len(tiles)):
                blocks = [(p, head, sub) for p in range(n_pairs) for head in range(2) for sub in range(nsub)]
                lhs = jnp.concatenate(
                    [hi[p, t][:, head * TILE + sub * BLOCK:head * TILE + (sub + 1) * BLOCK]
                     for p, head, sub in blocks], axis=0)
                sums = jnp.dot(lhs, tri, preferred_element_type=F32)
                for i, (p, head, sub) in enumerate(blocks):
                    res[p, t, head, sub] = sums[i * TILE:(i + 1) * TILE]
            if prefetch_tile is not None:
                q_next = pl.multiple_of(prefetch_tile * TILE, TILE)
                for p in range(n_pairs):
                    for t in range(STICK_MERGE):
                        z_ref[p, t] = scores(p, prefetch_tile - t, q_next)
            weights, r_all = [], []
            for p in range(n_pairs):
                r = r_ref[p]
                r_heads = [r[:, :LANES], r[:, LANES:]]
                a_tiles = []
                for t in range(len(tiles)):
                    pieces = [None] * (2 * nsub)
                    for head in range(2):
                        for sub in reversed(range(nsub)):
                            pieces[head * nsub + sub] = res[p, t, head, sub][:, :BLOCK] + r_heads[head]
                            r_heads[head] = r_heads[head] + res[p, t, head, sub][:, BLOCK:]
                    a = jnp.exp2(z[p, t] - jnp.concatenate(pieces, axis=1))
                    a_tiles.append(a.astype(BF16))
                r_ref[p] = jnp.concatenate(r_heads, axis=1)
                r_all += r_heads
                weights.append(jnp.concatenate(a_tiles, axis=1))
            for p in range(n_pairs):
                cols = slice(p * LANES, (p + 1) * LANES)
                vbd = jnp.concatenate([_block_diag_rows(v_ref[pl.ds(k0, TILE), cols]) for k0 in k0s], axis=0)
                acc_ref[p] += jnp.dot(weights[p], vbd, preferred_element_type=F32)
            return jnp.min(functools.reduce(jnp.minimum, r_all))

        def back(state):
            kt, _ = state
            return kt - 1, step([(kt, False)])

        for n in range(1, STICK_MERGE):
            @pl.when(qi == n - 1)
            def _(n=n):
                step([(n - 1 - i, i == 0) for i in range(n)],
                     prefetch_tile=qi + 1 if n == STICK_MERGE - 1 else None)

        @pl.when(qi >= STICK_MERGE - 1)
        def _():
            r_low = step([(qi - i, i == 0) for i in range(STICK_MERGE)], prefetched=True,
                         prefetch_tile=jnp.minimum(qi + 1, nt - 1))
            lax.while_loop(lambda state: (state[0] >= 0) & (state[1] < R_STOP), back, (qi - STICK_MERGE, r_low))
        for p in range(n_pairs):
            o_ref[pl.ds(q0, TILE), p * LANES:(p + 1) * LANES] = acc_ref[p].astype(o_ref.dtype)
        return carry

    lax.fori_loop(0, nt, q_tile, 0)


def _stick(proj, tri, *, bsz, seq, n_pairs_total, n_pairs=4):
    assert seq % STICK_TILE == 0 and seq // STICK_TILE >= STICK_MERGE and n_pairs_total % n_pairs == 0, (
        seq, n_pairs_total)
    groups = n_pairs_total // n_pairs
    w = n_pairs * LANES
    return pl.pallas_call(
        functools.partial(_stick_kernel, n_pairs=n_pairs),
        grid=(bsz, groups),
        in_specs=[pl.BlockSpec((None, seq, w), lambda b, g: (b, 0, g)),
                  pl.BlockSpec((None, seq, w), lambda b, g: (b, 0, groups + g)),
                  pl.BlockSpec((None, seq, w), lambda b, g: (b, 0, 2 * groups + g)),
                  pl.BlockSpec((BLOCK, 2 * BLOCK), lambda b, g: (0, 0))],
        out_specs=pl.BlockSpec((None, seq, w), lambda b, g: (b, 0, g)),
        out_shape=jax.ShapeDtypeStruct((bsz, seq, n_pairs_total * LANES), BF16),
        scratch_shapes=[pltpu.VMEM((n_pairs, STICK_TILE, 2 * LANES), F32),
                        pltpu.VMEM((n_pairs, STICK_TILE, LANES), F32),
                        pltpu.VMEM((n_pairs, STICK_MERGE, STICK_TILE, 2 * STICK_TILE), F32)],
        compiler_params=_cparams(2),
        name="stick_breaking",
    )(proj, proj, proj, tri)


def _t5_bucket(dist):
    n = jnp.maximum(dist, 0)
    nf = jnp.maximum(n, 1).astype(F32)
    large = MAX_EXACT + (jnp.log(nf / MAX_EXACT) / math.log(MAX_DISTANCE / MAX_EXACT)
                         * (N_BUCKETS - MAX_EXACT)).astype(jnp.int32)
    large = jnp.minimum(large, N_BUCKETS - 1)
    return jnp.where(n < MAX_EXACT, n, large)


def _band_bias(rel_bias, blk):
    qi = jnp.arange(blk)[:, None]
    kj = jnp.arange(2 * blk)[None, :]
    dist = qi + blk - kj
    onehot = jax.nn.one_hot(_t5_bucket(dist), N_BUCKETS, dtype=F32)
    return jnp.einsum("qkn,nh->hqk", onehot, rel_bias, precision=lax.Precision.HIGHEST), dist


def _swa_bias(rel_bias_a):
    band, dist = _band_bias(rel_bias_a, BLOCK)
    in_window = (dist >= 0) & (dist < WINDOW)
    general = jnp.where(in_window[None], band, NEG_INF)
    first = jnp.where((in_window & (jnp.arange(2 * BLOCK)[None, :] >= BLOCK))[None], band, NEG_INF)
    both = jnp.stack([first, general])
    return both.reshape(2, A_HEADS // 2, 2, BLOCK, 2 * BLOCK).transpose(0, 1, 3, 2, 4).reshape(
        2, A_HEADS // 2, BLOCK, 4 * BLOCK)


def _diff_bias(rel_bias_b):
    band, dist = _band_bias(rel_bias_b, TILE)
    band = (band - rel_bias_b[N_BUCKETS - 1][:, None, None]) * LOG2E
    diag = jnp.where((dist[:, TILE:] >= 0)[None], band[:, :, TILE:], NEG_INF)
    return jnp.stack([diag, band[:, :, :TILE]], axis=1).swapaxes(-1, -2)


def _even_in_weight(w):
    a_q, a_kv, b_qk = A_HEADS * HEAD_DIM, A_KV_HEADS * HEAD_DIM, B_HEADS * 2 * HEAD_DIM
    splits = [a_q, a_q + a_kv, a_q + 2 * a_kv, a_q + 2 * a_kv + b_qk, a_q + 2 * a_kv + 2 * b_qk]
    aq, ak, av, bq, bk, bv = jnp.split(w, splits, axis=1)
    dup = lambda t: jnp.concatenate(
        [t[:, h * HEAD_DIM:(h + 1) * HEAD_DIM] for h in range(A_KV_HEADS) for _ in range(2)], axis=1)
    return jnp.concatenate([aq * SCALE, dup(ak), dup(av), bq * (SCALE * LOG2E), bk, bv], axis=1).astype(BF16)


def _suffix_matrix():
    j = jnp.arange(BLOCK)[:, None]
    s = jnp.arange(BLOCK)[None, :]
    return jnp.concatenate([(j >= s).astype(BF16), jnp.ones((BLOCK, BLOCK), BF16)], axis=1)


def kernel(x, rel_bias, norm_mix, norm_ffn, norm_final, w_in_even, w_out_even, sinks, lam_q1, lam_k1, lam_q2,
           lam_k2, diff_norm, w_in_odd, w_out_odd, ffn_up, ffn_conv, ffn_conv_b, ffn_down):
    bsz, seq, d = x.shape
    depth = norm_mix.shape[0]
    dff = ffn_down.shape[1]
    cw = 256
    n_pairs_c = d // LANES
    n = bsz * seq

    swa_bias = _swa_bias(rel_bias[:, :A_HEADS])
    diff_bias = _diff_bias(rel_bias[:, A_HEADS:])
    tri = _suffix_matrix()

    xf = x.reshape(n, d)
    for layer in range(depth):
        if layer % 2 == 0:
            e = layer // 2
            proj = _norm_proj(xf, norm_mix[layer], _even_in_weight(w_in_even[e]))
            proj = proj.reshape(bsz, seq, proj.shape[1])
            oa = _swa(proj, sinks[e], swa_bias, bsz=bsz, seq=seq)
            lambda_init = 0.8 - 0.6 * math.exp(-0.3 * layer)
            pad = lambda v: jnp.pad(v, (0, LANES - HEAD_DIM))
            lam_rows = jnp.stack([pad(lam_q1[e]), pad(lam_k1[e]), pad(lam_q2[e]), pad(lam_k2[e])]
                                 + [jnp.zeros((LANES,), F32)] * (SUBLANES - 4))
            ob = _diff(proj, diff_bias, lam_rows, diff_norm[e], bsz=bsz, seq=seq, lambda_init=lambda_init)
            w_out = w_out_even[e].astype(BF16)
            half = oa.shape[-1]
            mixed, w_outs = [oa.reshape(n, half), ob.reshape(n, half)], [w_out[:half], w_out[half:]]
        else:
            o = layer // 2
            w_in = jnp.concatenate([w_in_odd[o][:, :d] * (SCALE * LOG2E), w_in_odd[o][:, d:]],
                                   axis=1).astype(BF16)
            proj = _norm_proj(xf, norm_mix[layer], w_in).reshape(bsz, seq, 3 * d)
            oc = _stick(proj, tri, bsz=bsz, seq=seq, n_pairs_total=n_pairs_c)
            mixed, w_outs = [oc.reshape(n, d)], [w_out_odd[o].astype(BF16)]
        xf = _mix_ffn(xf, mixed, w_outs, norm_ffn[layer], ffn_up[layer].astype(BF16), ffn_conv[layer],
                      ffn_conv_b[layer].reshape(1, 2 * dff), ffn_down[layer].astype(BF16), norm_final,
                      seq=seq, cw=cw, final_norm=(layer == depth - 1))
    return xf.reshape(bsz, seq, d)
```

```python
import functools
import math

import jax
import jax.numpy as jnp
from jax import lax
from jax.experimental import pallas as pl
from jax.experimental.pallas import tpu as pltpu

LANES = 128
SUBLANES = 8
VMEM_LIMIT_BYTES = 56 * 1024 * 1024

HEAD_DIM = 64
BLOCK = 128
TILE = 256
STICK_TILE = 128
STICK_MERGE = 3
SWA_UNROLL = 2
V_ROWS = LANES + 2 * SUBLANES
WINDOW = 128
A_HEADS, A_KV_HEADS = 8, 2
B_HEADS = 4
N_BUCKETS, MAX_EXACT, MAX_DISTANCE = 32, 16, 128
CONV_W = 3
EPS = 1e-6
SCALE = HEAD_DIM ** -0.5

LOG2E = math.log2(math.e)
EXP2_MAX = 126.0
R_STOP = 152.0

F32 = jnp.float32
BF16 = jnp.bfloat16
NEG_INF = float("-inf")


def _cparams(n_axes):
    return pltpu.CompilerParams(
        dimension_semantics=("arbitrary",) * n_axes, vmem_limit_bytes=VMEM_LIMIT_BYTES)


def _rms(x, g):
    ms = jnp.mean(x * x, axis=-1, keepdims=True)
    return x * lax.rsqrt(ms + EPS) * g


def _lane_halves(rows):
    lane = lax.broadcasted_iota(jnp.int32, (rows, LANES), 1)
    return lane < HEAD_DIM


def _block_diag_rows(t):
    lo = _lane_halves(t.shape[0])
    zero = jnp.zeros_like(t)
    return jnp.concatenate([jnp.where(lo, t, zero), jnp.where(lo, zero, t)], axis=0)


def _dot_nt(a, b):
    return lax.dot_general(a, b, (((1,), (1,)), ((), ())), preferred_element_type=F32)


def _norm_proj_kernel(x_ref, g_ref, w_ref, o_ref, *, col_chunk):
    h = _rms(x_ref[...], g_ref[...]).astype(BF16)
    for n0 in range(0, o_ref.shape[1], col_chunk):
        o_ref[:, n0:n0 + col_chunk] = jnp.dot(
            h, w_ref[:, n0:n0 + col_chunk], preferred_element_type=F32).astype(o_ref.dtype)


def _norm_proj(x, g, w, *, tm=1024, col_chunk=512):
    n, d = x.shape
    nout = w.shape[1]
    assert n % tm == 0 and nout % col_chunk == 0, (n, nout)
    return pl.pallas_call(
        functools.partial(_norm_proj_kernel, col_chunk=col_chunk),
        grid=(n // tm,),
        in_specs=[pl.BlockSpec((tm, d), lambda i: (i, 0)),
                  pl.BlockSpec((1, d), lambda i: (0, 0)),
                  pl.BlockSpec((d, nout), lambda i: (0, 0), pipeline_mode=pl.Buffered(1))],
        out_specs=pl.BlockSpec((tm, nout), lambda i: (i, 0)),
        out_shape=jax.ShapeDtypeStruct((n, nout), BF16),
        compiler_params=_cparams(1),
        name="norm_proj",
    )(x, g.reshape(1, d), w)


def _mix_ffn_kernel(*refs, n_mix, tiles_per_seq, n_chunks, cw, final_norm):
    x_ref = refs[0]
    o_refs, wo_refs = refs[1:1 + n_mix], refs[1 + n_mix:1 + 2 * n_mix]
    g_ref, wup_ref, wc_ref, bc_ref, wdn_ref, gf_ref, out_ref, tail_ref, act_ref = refs[1 + 2 * n_mix:]
    tm = x_ref.shape[0]
    dff = wdn_ref.shape[0]

    @pl.when((pl.program_id(0) % tiles_per_seq) == 0)
    def _():
        tail_ref[...] = jnp.zeros(tail_ref.shape, F32)

    x = x_ref[...]
    for o_ref, wo_ref in zip(o_refs, wo_refs):
        x = x + jnp.dot(o_ref[...], wo_ref[...], preferred_element_type=F32)
    h = _rms(x, g_ref[...]).astype(BF16)
    for c in range(n_chunks):
        conv = []
        for part in range(2):
            cols = slice(part * dff + c * cw, part * dff + (c + 1) * cw)
            u = jnp.dot(h, wup_ref[:, cols], preferred_element_type=F32)
            ext = jnp.concatenate([tail_ref[2 * c + part], u], axis=0)
            tail_ref[2 * c + part] = u[tm - SUBLANES:, :]
            u1 = pltpu.roll(ext, 1, axis=0)[SUBLANES:]
            u2 = pltpu.roll(ext, 2, axis=0)[SUBLANES:]
            wc = wc_ref[:, cols]
            conv.append((u2 * wc[0:1] + u1 * wc[1:2] + u * wc[2:3]) + bc_ref[:, cols])
        gate, val = conv
        act_ref[:, c * cw:(c + 1) * cw] = (gate * jax.nn.sigmoid(gate) * val).astype(BF16)
    y = x + jnp.dot(act_ref[...], wdn_ref[...], preferred_element_type=F32)
    if final_norm:
        y = _rms(y, gf_ref[...])
    out_ref[...] = y


def _mix_ffn(x, outs, w_outs, g, wup, wc, bc, wdn, g_final, *, seq, cw, tm=1024, final_norm=False):
    n, d = x.shape
    dff = wdn.shape[0]
    n_chunks = dff // cw
    assert seq % tm == 0 and n % seq == 0 and dff % cw == 0 and wc.shape[0] == CONV_W, (n, seq, dff)
    const = lambda i: (0, 0)
    rows = lambda i: (i, 0)
    resident = lambda a: pl.BlockSpec(a.shape, const, pipeline_mode=pl.Buffered(1))
    return pl.pallas_call(
        functools.partial(_mix_ffn_kernel, n_mix=len(outs), tiles_per_seq=seq // tm, n_chunks=n_chunks,
                          cw=cw, final_norm=final_norm),
        grid=(n // tm,),
        in_specs=([pl.BlockSpec((tm, d), rows)]
                  + [pl.BlockSpec((tm, o.shape[1]), rows) for o in outs]
                  + [resident(w) for w in w_outs]
                  + [pl.BlockSpec((1, d), const), resident(wup), pl.BlockSpec(wc.shape, const),
                     pl.BlockSpec(bc.shape, const), resident(wdn), pl.BlockSpec((1, d), const)]),
        out_specs=pl.BlockSpec((tm, d), rows),
        out_shape=jax.ShapeDtypeStruct((n, d), F32),
        scratch_shapes=[pltpu.VMEM((2 * n_chunks, SUBLANES, cw), F32),
                        pltpu.VMEM((tm, dff), BF16)],
        compiler_params=_cparams(1),
        name="mix_ffn",
    )(x, *outs, *w_outs, g.reshape(1, d), wup, wc, bc, wdn, g_final.reshape(1, d))


def _swa_kernel(sink_ref, q_ref, k_ref, v_ref, bias_ref, o_ref, *, n_pairs, pairs_per_kv):
    nb = q_ref.shape[0] // BLOCK

    def band(ref, j, kv_cols):
        cur = pl.multiple_of(j * BLOCK, BLOCK)
        prev = pl.multiple_of(jnp.maximum(j - 1, 0) * BLOCK, BLOCK)
        return jnp.concatenate([ref[pl.ds(prev, BLOCK), kv_cols], ref[pl.ds(cur, BLOCK), kv_cols]], axis=0)

    def body(i, carry):
        chains = [(i * SWA_UNROLL + u, c) for u in range(SWA_UNROLL) for c in range(n_pairs)]
        kv = lambda c: slice((c // pairs_per_kv) * LANES, (c // pairs_per_kv + 1) * LANES)
        scores = []
        for j, c in chains:
            q2 = q_ref[pl.ds(pl.multiple_of(j * BLOCK, BLOCK), BLOCK), c * LANES:(c + 1) * LANES]
            scores.append(_dot_nt(q2, _block_diag_rows(band(k_ref, j, kv(c)))) + bias_ref[jnp.minimum(j, 1), c])
        weights = []
        for (j, c), s in zip(chains, scores):
            w_halves = []
            for i2 in range(2):
                half = s[:, i2 * 2 * BLOCK:(i2 + 1) * 2 * BLOCK]
                sink = sink_ref[2 * c + i2]
                m = jnp.maximum(jnp.max(half, axis=-1, keepdims=True), sink)
                p = jnp.exp(half - m)
                w_halves.append(p / (jnp.sum(p, axis=-1, keepdims=True) + jnp.exp(sink - m)))
            weights.append(jnp.concatenate(w_halves, axis=1).astype(BF16))
        for (j, c), w in zip(chains, weights):
            o = jnp.dot(w, _block_diag_rows(band(v_ref, j, kv(c))), preferred_element_type=F32)
            o_ref[pl.ds(pl.multiple_of(j * BLOCK, BLOCK), BLOCK), c * LANES:(c + 1) * LANES] = (
                o.astype(o_ref.dtype))
        return carry

    lax.fori_loop(0, nb // SWA_UNROLL, body, 0)


def _swa(proj, sinks, bias, *, bsz, seq):
    n_pairs = A_HEADS // 2
    pairs_per_kv = n_pairs // A_KV_HEADS
    q_w, kv_w = n_pairs * LANES, A_KV_HEADS * LANES
    assert seq % (BLOCK * SWA_UNROLL) == 0 and WINDOW == BLOCK, seq
    return pl.pallas_call(
        functools.partial(_swa_kernel, n_pairs=n_pairs, pairs_per_kv=pairs_per_kv),
        grid=(bsz,),
        in_specs=[pl.BlockSpec(memory_space=pltpu.SMEM),
                  pl.BlockSpec((None, seq, q_w), lambda b: (b, 0, 0)),
                  pl.BlockSpec((None, seq, kv_w), lambda b: (b, 0, q_w // kv_w)),
                  pl.BlockSpec((None, seq, kv_w), lambda b: (b, 0, q_w // kv_w + 1)),
                  pl.BlockSpec(bias.shape, lambda b: (0, 0, 0, 0))],
        out_specs=pl.BlockSpec((None, seq, q_w), lambda b: (b, 0, 0)),
        out_shape=jax.ShapeDtypeStruct((bsz, seq, q_w), BF16),
        compiler_params=_cparams(1),
        name="swa_sink",
    )(sinks, proj, proj, proj, bias)


def _diff_kernel(q_ref, k_ref, v_ref, bias_ref, lam_ref, gain_ref, o_ref, vt_ref, m_ref, acc_ref, s_ref, *,
                 lambda_init, n_heads):
    nt = q_ref.shape[0] // TILE
    lp = lam_ref[...]
    lam = (jnp.exp(jnp.sum(lp[0:1] * lp[1:2], axis=-1, keepdims=True))
           - jnp.exp(jnp.sum(lp[2:3] * lp[3:4], axis=-1, keepdims=True)) + lambda_init)

    def eye(n):
        return (lax.broadcasted_iota(jnp.int32, (n, n), 0)
                == lax.broadcasted_iota(jnp.int32, (n, n), 1)).astype(BF16)

    def transpose_v(kt, carry):
        k0 = pl.multiple_of(kt * TILE, TILE)
        for h in range(n_heads):
            vt = _dot_nt(eye(LANES), v_ref[pl.ds(k0, TILE), h * LANES:(h + 1) * LANES])
            vt_ref[h, kt] = jnp.concatenate([vt.astype(BF16), jnp.ones((V_ROWS - LANES, TILE), BF16)], axis=0)
        return carry

    lax.fori_loop(0, nt, transpose_v, 0)

    def q_tile(qi, carry):
        q0 = pl.multiple_of(qi * TILE, TILE)
        m_ref[...] = jnp.full(m_ref.shape, NEG_INF, F32)
        acc_ref[...] = jnp.zeros(acc_ref.shape, F32)

        def raw_scores(h, kt, q_start=q0):
            cols = slice(h * LANES, (h + 1) * LANES)
            kbd = _block_diag_rows(k_ref[pl.ds(pl.multiple_of(kt * TILE, TILE), TILE), cols])
            return _dot_nt(kbd, q_ref[pl.ds(q_start, TILE), cols])

        def update(get_scores, tiles):
            scores = {}
            for h in range(n_heads):
                for t, (_, bias_idx) in enumerate(tiles):
                    st = get_scores(h, t)
                    halves = [st[:TILE], st[TILE:]]
                    if bias_idx is not None:
                        halves = [x + bias_ref[h, bias_idx] for x in halves]
                    scores[h, t] = halves
            probs, alphas = [], []
            for h in range(n_heads):
                m_old = m_ref[h]
                m_blk = jnp.concatenate(
                    [functools.reduce(jnp.maximum, [jnp.max(scores[h, t][i], axis=0, keepdims=True)
                                                    for t in range(len(tiles))]) for i in range(2)], axis=1)
                m_new = jnp.maximum(m_old, m_blk)
                alphas.append(jnp.exp2(m_old - m_new)[0:1])
                m_ref[h] = m_new
                probs.append(jnp.concatenate(
                    [jnp.concatenate([jnp.exp2(scores[h, t][i] - m_new[0:1, i * TILE:(i + 1) * TILE]
                                               ).astype(BF16) for i in range(2)], axis=1)
                     for t in range(len(tiles))], axis=0))
            for h in range(n_heads):
                vt = jnp.concatenate([vt_ref[h, kt] for kt, _ in tiles], axis=1)
                pv = jnp.dot(vt, probs[h], preferred_element_type=F32)
                acc_ref[h] = acc_ref[h] * alphas[h] + pv

        def step(tiles):
            update(lambda h, t: raw_scores(h, tiles[t][0]), tiles)

        def fill(slot, kt, q_start=q0):
            for h in range(n_heads):
                for t in range(2):
                    s_ref[slot, h, t] = raw_scores(h, kt + t, q_start)

        n_far = jnp.maximum(qi - 1, 0)
        first = n_far % 2
        n_pairs = n_far // 2

        @pl.when(first == 1)
        def _():
            step([(0, None)])

        @pl.when(qi == 0)
        def _():
            step([(0, 0)])

        @pl.when(qi >= 1)
        def _():
            from_slot = lambda slot: (lambda h, t: s_ref[slot, h, t])
            last = [(qi - 1, 1), (qi, 0)]

            def two_pairs(j, c2):
                kt = first + 4 * j
                fill(1, kt + 2)
                update(from_slot(0), [(kt, None), (kt + 1, None)])
                fill(0, kt + 4)
                update(from_slot(1), [(kt + 2, None), (kt + 3, None)])
                return c2

            lax.fori_loop(0, n_pairs // 2, two_pairs, 0)

            @pl.when(n_pairs % 2 == 1)
            def _():
                kt = qi - 3
                fill(1, kt + 2)
                update(from_slot(0), [(kt, None), (kt + 1, None)])
                update(from_slot(1), last)

            @pl.when(n_pairs % 2 == 0)
            def _():
                update(from_slot(0), last)

        nxt = jnp.minimum(qi + 1, nt - 1)
        fill(0, jnp.maximum(nxt - 1, 0) % 2, pl.multiple_of(nxt * TILE, TILE))

        for h in range(n_heads):
            acc = acc_ref[h]
            o1 = acc[:LANES, :TILE] / acc[LANES:LANES + 1, :TILE]
            o2 = acc[:LANES, TILE:] / acc[LANES:LANES + 1, TILE:]
            d = o1 - lam * o2
            y = d * lax.rsqrt(jnp.mean(d * d, axis=0, keepdims=True) + EPS) * gain_ref[...] * (1.0 - lambda_init)
            o = _dot_nt(eye(TILE), y.astype(o_ref.dtype))
            o_ref[pl.ds(q0, TILE), h * LANES:(h + 1) * LANES] = o.astype(o_ref.dtype)
        return carry

    lax.fori_loop(0, nt, q_tile, 0)


def _diff(proj, bias, lam_rows, gain, *, bsz, seq, lambda_init, n_heads=4):
    assert seq % TILE == 0 and seq // TILE >= 3 and B_HEADS % n_heads == 0, seq
    q_col0 = A_HEADS // 2 + 2 * A_KV_HEADS
    groups = B_HEADS // n_heads
    w = n_heads * LANES
    qb, kb, vb = (c // n_heads for c in (q_col0, q_col0 + B_HEADS, q_col0 + 2 * B_HEADS))
    return pl.pallas_call(
        functools.partial(_diff_kernel, lambda_init=lambda_init, n_heads=n_heads),
        grid=(bsz, groups),
        in_specs=[pl.BlockSpec((None, seq, w), lambda b, g: (b, 0, qb + g)),
                  pl.BlockSpec((None, seq, w), lambda b, g: (b, 0, kb + g)),
                  pl.BlockSpec((None, seq, w), lambda b, g: (b, 0, vb + g), pipeline_mode=pl.Buffered(1)),
                  pl.BlockSpec((n_heads, 2, TILE, TILE), lambda b, g: (g, 0, 0, 0),
                               pipeline_mode=pl.Buffered(1)),
                  pl.BlockSpec((SUBLANES, LANES), lambda b, g: (0, 0)),
                  pl.BlockSpec((LANES, TILE), lambda b, g: (0, 0))],
        out_specs=pl.BlockSpec((None, seq, w), lambda b, g: (b, 0, g)),
        out_shape=jax.ShapeDtypeStruct((bsz, seq, B_HEADS * LANES), BF16),
        scratch_shapes=[pltpu.VMEM((n_heads, seq // TILE, V_ROWS, TILE), BF16),
                        pltpu.VMEM((n_heads, SUBLANES, 2 * TILE), F32),
                        pltpu.VMEM((n_heads, V_ROWS, 2 * TILE), F32),
                        pltpu.VMEM((2, n_heads, 2, 2 * TILE, TILE), F32)],
        compiler_params=_cparams(2),
        name="diff_attn",
    )(proj, proj, proj, bias, lam_rows, jnp.broadcast_to(gain[:, None], (LANES, TILE)))


def _stick_kernel(q_ref, k_ref, v_ref, tri_ref, o_ref, r_ref, acc_ref, z_ref, *, n_pairs):
    TILE = STICK_TILE
    nt = q_ref.shape[0] // TILE
    nsub = TILE // BLOCK
    row = lax.broadcasted_iota(jnp.int32, (TILE, 2 * TILE), 0)
    col = lax.broadcasted_iota(jnp.int32, (TILE, 2 * TILE), 1) & (TILE - 1)
    diag_mask = jnp.where(col < row, 0.0, NEG_INF)

    def q_tile(qi, carry):
        q0 = pl.multiple_of(qi * TILE, TILE)
        r_ref[...] = jnp.zeros(r_ref.shape, F32)
        acc_ref[...] = jnp.zeros(acc_ref.shape, F32)

        def scores(p, kt, q_start):
            cols = slice(p * LANES, (p + 1) * LANES)
            kbd = _block_diag_rows(k_ref[pl.ds(pl.multiple_of(kt * TILE, TILE), TILE), cols])
            return _dot_nt(q_ref[pl.ds(q_start, TILE), cols], kbd)

        def step(tiles, prefetched=False, prefetch_tile=None):
            k0s = [pl.multiple_of(kt * TILE, TILE) for kt, _ in tiles]
            work = [(p, t) for p in range(n_pairs) for t in range(len(tiles))]
            tri = tri_ref[...]
            z, hi, res = {}, {}, {}
            for p, t in work:
                z[p, t] = z_ref[p, t] if prefetched else scores(p, tiles[t][0], q0)
                if tiles[t][1]:
                    z[p, t] = z[p, t] + diag_mask
            for p, t in work:
                sp = jnp.maximum(z[p, t], jnp.log2(1.0 + jnp.exp2(jnp.minimum(z[p, t], EXP2_MAX))))
                hi[p, t] = sp.astype(BF16)
            for t in range(len(tiles)):
                blocks = [(p, head, sub) for p in range(n_pairs) for head in range(2) for sub in range(nsub)]
                lhs = jnp.concatenate(
                    [hi[p, t][:, head * TILE + sub * BLOCK:head * TILE + (sub + 1) * BLOCK]
                     for p, head, sub in blocks], axis=0)
                sums = jnp.dot(lhs, tri, preferred_element_type=F32)
                for i, (p, head, sub) in enumerate(blocks):
                    res[p, t, head, sub] = sums[i * TILE:(i + 1) * TILE]
            if prefetch_tile is not None:
                q_next = pl.multiple_of(prefetch_tile * TILE, TILE)
                for p in range(n_pairs):
                    for t in range(STICK_MERGE):
                        z_ref[p, t] = scores(p, prefetch_tile - t, q_next)
            weights, r_all = [], []
            for p in range(n_pairs):
                r = r_ref[p]
                r_heads = [r[:, :LANES], r[:, LANES:]]
                a_tiles = []
                for t in range(len(tiles)):
                    pieces = [None] * (2 * nsub)
                    for head in range(2):
                        for sub in reversed(range(nsub)):
                            pieces[head * nsub + sub] = res[p, t, head, sub][:, :BLOCK] + r_heads[head]
                            r_heads[head] = r_heads[head] + res[p, t, head, sub][:, BLOCK:]
                    a = jnp.exp2(z[p, t] - jnp.concatenate(pieces, axis=1))
                    a_tiles.append(a.astype(BF16))
                r_ref[p] = jnp.concatenate(r_heads, axis=1)
                r_all += r_heads
                weights.append(jnp.concatenate(a_tiles, axis=1))
            for p in range(n_pairs):
                cols = slice(p * LANES, (p + 1) * LANES)
                vbd = jnp.concatenate([_block_diag_rows(v_ref[pl.ds(k0, TILE), cols]) for k0 in k0s], axis=0)
                acc_ref[p] += jnp.dot(weights[p], vbd, preferred_element_type=F32)
            return jnp.min(functools.reduce(jnp.minimum, r_all))

        def back(state):
            kt, _ = state
            return kt - 1, step([(kt, False)])

        for n in range(1, STICK_MERGE):
            @pl.when(qi == n - 1)
            def _(n=n):
                step([(n - 1 - i, i == 0) for i in range(n)],
                     prefetch_tile=qi + 1 if n == STICK_MERGE - 1 else None)

        @pl.when(qi >= STICK_MERGE - 1)
        def _():
            r_low = step([(qi - i, i == 0) for i in range(STICK_MERGE)], prefetched=True,
                         prefetch_tile=jnp.minimum(qi + 1, nt - 1))
            lax.while_loop(lambda state: (state[0] >= 0) & (state[1] < R_STOP), back, (qi - STICK_MERGE, r_low))
        for p in range(n_pairs):
            o_ref[pl.ds(q0, TILE), p * LANES:(p + 1) * LANES] = acc_ref[p].astype(o_ref.dtype)
        return carry

    lax.fori_loop(0, nt, q_tile, 0)


def _stick(proj, tri, *, bsz, seq, n_pairs_total, n_pairs=4):
    assert seq % STICK_TILE == 0 and seq // STICK_TILE >= STICK_MERGE and n_pairs_total % n_pairs == 0, (
        seq, n_pairs_total)
    groups = n_pairs_total // n_pairs
    w = n_pairs * LANES
    return pl.pallas_call(
        functools.partial(_stick_kernel, n_pairs=n_pairs),
        grid=(bsz, groups),
        in_specs=[pl.BlockSpec((None, seq, w), lambda b, g: (b, 0, g)),
                  pl.BlockSpec((None, seq, w), lambda b, g: (b, 0, groups + g)),
                  pl.BlockSpec((None, seq, w), lambda b, g: (b, 0, 2 * groups + g)),
                  pl.BlockSpec((BLOCK, 2 * BLOCK), lambda b, g: (0, 0))],
        out_specs=pl.BlockSpec((None, seq, w), lambda b, g: (b, 0, g)),
        out_shape=jax.ShapeDtypeStruct((bsz, seq, n_pairs_total * LANES), BF16),
        scratch_shapes=[pltpu.VMEM((n_pairs, STICK_TILE, 2 * LANES), F32),
                        pltpu.VMEM((n_pairs, STICK_TILE, LANES), F32),
                        pltpu.VMEM((n_pairs, STICK_MERGE, STICK_TILE, 2 * STICK_TILE), F32)],
        compiler_params=_cparams(2),
        name="stick_breaking",
    )(proj, proj, proj, tri)


def _t5_bucket(dist):
    n = jnp.maximum(dist, 0)
    nf = jnp.maximum(n, 1).astype(F32)
    large = MAX_EXACT + (jnp.log(nf / MAX_EXACT) / math.log(MAX_DISTANCE / MAX_EXACT)
                         * (N_BUCKETS - MAX_EXACT)).astype(jnp.int32)
    large = jnp.minimum(large, N_BUCKETS - 1)
    return jnp.where(n < MAX_EXACT, n, large)


def _band_bias(rel_bias, blk):
    qi = jnp.arange(blk)[:, None]
    kj = jnp.arange(2 * blk)[None, :]
    dist = qi + blk - kj
    onehot = jax.nn.one_hot(_t5_bucket(dist), N_BUCKETS, dtype=F32)
    return jnp.einsum("qkn,nh->hqk", onehot, rel_bias, precision=lax.Precision.HIGHEST), dist


def _swa_bias(rel_bias_a):
    band, dist = _band_bias(rel_bias_a, BLOCK)
    in_window = (dist >= 0) & (dist < WINDOW)
    general = jnp.where(in_window[None], band, NEG_INF)
    first = jnp.where((in_window & (jnp.arange(2 * BLOCK)[None, :] >= BLOCK))[None], band, NEG_INF)
    both = jnp.stack([first, general])
    return both.reshape(2, A_HEADS // 2, 2, BLOCK, 2 * BLOCK).transpose(0, 1, 3, 2, 4).reshape(
        2, A_HEADS // 2, BLOCK, 4 * BLOCK)


def _diff_bias(rel_bias_b):
    band, dist = _band_bias(rel_bias_b, TILE)
    band = (band - rel_bias_b[N_BUCKETS - 1][:, None, None]) * LOG2E
    diag = jnp.where((dist[:, TILE:] >= 0)[None], band[:, :, TILE:], NEG_INF)
    return jnp.stack([diag, band[:, :, :TILE]], axis=1).swapaxes(-1, -2)


def _even_in_weight(w):
    a_q, a_kv, b_qk = A_HEADS * HEAD_DIM, A_KV_HEADS * HEAD_DIM, B_HEADS * 2 * HEAD_DIM
    splits = [a_q, a_q + a_kv, a_q + 2 * a_kv, a_q + 2 * a_kv + b_qk, a_q + 2 * a_kv + 2 * b_qk]
    aq, ak, av, bq, bk, bv = jnp.split(w, splits, axis=1)
    dup = lambda t: jnp.concatenate(
        [t[:, h * HEAD_DIM:(h + 1) * HEAD_DIM] for h in range(A_KV_HEADS) for _ in range(2)], axis=1)
    return jnp.concatenate([aq * SCALE, dup(ak), dup(av), bq * (SCALE * LOG2E), bk, bv], axis=1).astype(BF16)


def _suffix_matrix():
    j = jnp.arange(BLOCK)[:, None]
    s = jnp.arange(BLOCK)[None, :]
    return jnp.concatenate([(j >= s).astype(BF16), jnp.ones((BLOCK, BLOCK), BF16)], axis=1)


def kernel(x, rel_bias, norm_mix, norm_ffn, norm_final, w_in_even, w_out_even, sinks, lam_q1, lam_k1, lam_q2,
           lam_k2, diff_norm, w_in_odd, w_out_odd, ffn_up, ffn_conv, ffn_conv_b, ffn_down):
    bsz, seq, d = x.shape
    depth = norm_mix.shape[0]
    dff = ffn_down.shape[1]
    cw = 256
    n_pairs_c = d // LANES
    n = bsz * seq

    swa_bias = _swa_bias(rel_bias[:, :A_HEADS])
    diff_bias = _diff_bias(rel_bias[:, A_HEADS:])
    tri = _suffix_matrix()

    xf = x.reshape(n, d)
    for layer in range(depth):
        if layer % 2 == 0:
            e = layer // 2
            proj = _norm_proj(xf, norm_mix[layer], _even_in_weight(w_in_even[e]))
            proj = proj.reshape(bsz, seq, proj.shape[1])
            oa = _swa(proj, sinks[e], swa_bias, bsz=bsz, seq=seq)
            lambda_init = 0.8 - 0.6 * math.exp(-0.3 * layer)
            pad = lambda v: jnp.pad(v, (0, LANES - HEAD_DIM))
            lam_rows = jnp.stack([pad(lam_q1[e]), pad(lam_k1[e]), pad(lam_q2[e]), pad(lam_k2[e])]
                                 + [jnp.zeros((LANES,), F32)] * (SUBLANES - 4))
            ob = _diff(proj, diff_bias, lam_rows, diff_norm[e], bsz=bsz, seq=seq, lambda_init=lambda_init)
            w_out = w_out_even[e].astype(BF16)
            half = oa.shape[-1]
            mixed, w_outs = [oa.reshape(n, half), ob.reshape(n, half)], [w_out[:half], w_out[half:]]
        else:
            o = layer // 2
            w_in = jnp.concatenate([w_in_odd[o][:, :d] * (SCALE * LOG2E), w_in_odd[o][:, d:]],
                                   axis=1).astype(BF16)
            proj = _norm_proj(xf, norm_mix[layer], w_in).reshape(bsz, seq, 3 * d)
            oc = _stick(proj, tri, bsz=bsz, seq=seq, n_pairs_total=n_pairs_c)
            mixed, w_outs = [oc.reshape(n, d)], [w_out_odd[o].astype(BF16)]
        xf = _mix_ffn(xf, mixed, w_outs, norm_ffn[layer], ffn_up[layer].astype(BF16), ffn_conv[layer],
                      ffn_conv_b[layer].reshape(1, 2 * dff), ffn_down[layer].astype(BF16), norm_final,
                      seq=seq, cw=cw, final_norm=(layer == depth - 1))
    return xf.reshape(bsz, seq, d)
```
